```python
import math
import jax, jax.numpy as jnp
from jax import lax
import numpy as np

D_MODEL = 1024
BATCH = 4
SEQ = 8192
DEPTH = 1

ATTN_HEADS = 8
ATTN_HEAD_DIM = 64
ATTN_WIDTH = ATTN_HEADS * ATTN_HEAD_DIM
DILATED_PATTERNS = ((128, 1), (512, 4), (2048, 16))
ATTN_BLOCK = 128
N_BUCKETS = 32
MAX_DISTANCE = 2048
NEG_INF = -1e30
HGRN_HEADS = 8
HGRN_KEY_DIM = 128
HGRN_VAL_DIM = 128
HGRN_FDIM = HGRN_HEADS * HGRN_KEY_DIM
HGRN_WIDTH = HGRN_HEADS * HGRN_VAL_DIM
HGRN_CHUNK = 64
IN_WIDTH = 4 * ATTN_WIDTH + 2 * HGRN_FDIM + 2 * HGRN_WIDTH + 2 * D_MODEL
EPS = 1e-6

kernel_name = "hybrid_dilated_attn_hgrn2_gated_merge"


def rmsnorm(x, g):
    xf = x.astype(jnp.float32)
    y = xf * lax.rsqrt(jnp.mean(xf * xf, axis=-1, keepdims=True) + EPS)
    return (y * g.astype(jnp.float32)).astype(x.dtype)


def t5_bucket(dist):
    max_exact = N_BUCKETS // 2
    n = dist.astype(jnp.float32)
    large = max_exact + (jnp.log(jnp.maximum(n, 1.0) / max_exact)
                         / math.log(MAX_DISTANCE / max_exact)
                         * (N_BUCKETS - max_exact)).astype(jnp.int32)
    large = jnp.minimum(large, N_BUCKETS - 1)
    return jnp.where(dist < max_exact, dist, large)


def dilated_pattern(q, k, v, rel_bias, window, dilation):
    B, S, H, E = q.shape
    L = S // dilation
    span = window // dilation
    nb = -(-L // ATTN_BLOCK)
    Lp = nb * ATTN_BLOCK

    def to_sub(t):
        t = t.reshape(B, L, dilation, H, E).transpose(0, 2, 3, 1, 4)
        return jnp.pad(t, ((0, 0), (0, 0), (0, 0), (0, Lp - L), (0, 0)))

    def kv_blocks(t):
        tp = jnp.pad(to_sub(t), ((0, 0), (0, 0), (0, 0), (ATTN_BLOCK, 0), (0, 0)))
        prev = tp[:, :, :, :Lp].reshape(B, dilation, H, nb, ATTN_BLOCK, E)
        cur = tp[:, :, :, ATTN_BLOCK:].reshape(B, dilation, H, nb, ATTN_BLOCK, E)
        return jnp.concatenate([prev, cur], axis=4)

    qs = to_sub(q).reshape(B, dilation, H, nb, ATTN_BLOCK, E)
    ks, vs = kv_blocks(k), kv_blocks(v)

    qi = jnp.arange(ATTN_BLOCK)[:, None]
    kj = jnp.arange(2 * ATTN_BLOCK)[None, :]
    delta = qi + ATTN_BLOCK - kj
    band = (delta >= 0) & (delta <= span)
    key_pos = jnp.arange(nb)[:, None, None] * ATTN_BLOCK + kj[None] - ATTN_BLOCK
    mask = band[None] & (key_pos >= 0)
    bucket = t5_bucket(jnp.clip(delta, 0, None) * dilation)
    bias = rel_bias.astype(jnp.float32)[bucket].transpose(2, 0, 1)

    s = jnp.einsum('bdhnqe,bdhnke->bdhnqk', qs, ks) * (E ** -0.5) + bias[None, None, :, None]
    s = jnp.where(mask, s, NEG_INF)
    m = jnp.max(s, axis=-1, keepdims=True)
    p = jnp.exp(s - m)
    den = jnp.sum(p, axis=-1, keepdims=True)
    o = jnp.einsum('bdhnqk,bdhnke->bdhnqe', p, vs) / den
    lse = (m + jnp.log(den))[..., 0]

    o = o.reshape(B, dilation, H, Lp, E)[:, :, :, :L].transpose(0, 3, 1, 2, 4).reshape(B, S, H, E)
    lse = lse.reshape(B, dilation, H, Lp)[..., :L].transpose(0, 3, 1, 2).reshape(B, S, H)
    return o, lse


def dilated_attention(q, k, v, rel_bias):
    outs, lses = [], []
    for window, dilation in DILATED_PATTERNS:
        o, lse = dilated_pattern(q, k, v, rel_bias, window, dilation)
        outs.append(o)
        lses.append(lse)
    w = jax.nn.softmax(jnp.stack(lses, 0), axis=0)
    return jnp.einsum('gbsh,gbshe->bshe', w, jnp.stack(outs, 0))


def hgrn2_recurrence(q, f_raw, i, lb):
    B, S, H, DK = q.shape
    DV = i.shape[-1]
    C = HGRN_CHUNK
    nc = S // C
    f = lb + (1.0 - lb) * jax.nn.sigmoid(f_raw)
    g = jnp.log(f)
    k = 1.0 - f

    def chunks(t):
        return t.reshape(B, nc, C, H, t.shape[-1]).transpose(1, 0, 3, 2, 4)

    causal = jnp.tril(jnp.ones((C, C), dtype=bool))

    def step(state, inp):
        qc, kc, vc, gc = inp
        b = jnp.cumsum(gc, axis=2)
        o_inter = jnp.einsum('bhtk,bhkv->bhtv', qc * jnp.exp(b), state)
        diff = b[:, :, :, None, :] - b[:, :, None, :, :]
        decay = jnp.exp(jnp.where(causal[:, :, None], diff, -jnp.inf))
        a = jnp.einsum('bhtk,bhsk,bhtsk->bhts', qc, kc, decay)
        o_intra = jnp.einsum('bhts,bhsv->bhtv', a, vc)
        b_last = b[:, :, -1:, :]
        new_state = (jnp.exp(b_last[:, :, 0, :])[..., None] * state
                     + jnp.einsum('bhsk,bhsv->bhkv', kc * jnp.exp(b_last - b), vc))
        return new_state, o_inter + o_intra

    s0 = jnp.zeros((B, H, DK, DV), jnp.float32)
    _, o = lax.scan(step, s0, (chunks(q), chunks(k), chunks(i), chunks(g)))
    return o.transpose(1, 0, 3, 2, 4).reshape(B, S, H, DV)


def setup_inputs(seed: int = 0) -> dict:
    key = jax.random.key(seed)
    ks = jax.random.split(key, 13)
    D = D_MODEL
    nrm = lambda k, shape, fan_in: jax.random.normal(k, shape, jnp.float32) * fan_in ** -0.5
    return {
        "x": jax.random.normal(ks[0], (BATCH, SEQ, D), jnp.float32),
        "c": jax.random.normal(ks[1], (BATCH, D), jnp.float32),
        "w_ada": nrm(ks[2], (DEPTH, D, 3 * D), D),
        "b_ada": 0.02 * jax.random.normal(ks[3], (DEPTH, 3 * D), jnp.float32),
        "norm_g": 1.0 + 0.05 * jax.random.normal(ks[4], (DEPTH, D), jnp.float32),
        "w_in": nrm(ks[5], (DEPTH, D, IN_WIDTH), D),
        "hgrn_onorm_g": 1.0 + 0.05 * jax.random.normal(ks[6], (DEPTH, HGRN_VAL_DIM), jnp.float32),
        "w_branch_a": nrm(ks[7], (DEPTH, ATTN_WIDTH, D), ATTN_WIDTH),
        "w_branch_b": nrm(ks[8], (DEPTH, HGRN_WIDTH, D), HGRN_WIDTH),
        "w_out": nrm(ks[9], (DEPTH, D, D), D),
        "rel_bias": 0.5 * jax.random.normal(ks[10], (N_BUCKETS, ATTN_HEADS), jnp.float32),
        "hgrn_lb": 0.5 * jax.random.normal(ks[11], (DEPTH + 1, HGRN_FDIM), jnp.float32),
        "final_g": 1.0 + 0.05 * jax.random.normal(ks[12], (D,), jnp.float32),
    }


def reference(x, c, w_ada, b_ada, norm_g, w_in, hgrn_onorm_g, w_branch_a, w_branch_b,
              w_out, rel_bias, hgrn_lb, final_g):
    B, S, D = x.shape
    sizes = [ATTN_WIDTH] * 4 + [HGRN_FDIM, HGRN_FDIM, HGRN_WIDTH, HGRN_WIDTH, D_MODEL, D_MODEL]
    cuts = [int(v) for v in np.cumsum(sizes)[:-1]]
    lower_bounds = jnp.cumsum(jax.nn.softmax(hgrn_lb.astype(jnp.float32), axis=0), axis=0)
    for l in range(DEPTH):
        mod = jax.nn.silu(c) @ w_ada[l] + b_ada[l]
        shift, scale, gate = jnp.split(mod, 3, axis=-1)
        h = rmsnorm(x, norm_g[l]) * (1.0 + scale[:, None]) + shift[:, None]
        proj = h @ w_in[l]
        q_a, k_a, v_a, z_a, q_b, f_b, i_b, z_b, g_a, g_b = jnp.split(proj, cuts, axis=-1)

        heads_a = lambda t: t.astype(jnp.float32).reshape(B, S, ATTN_HEADS, ATTN_HEAD_DIM)
        o_a = dilated_attention(heads_a(q_a), heads_a(k_a), heads_a(v_a), rel_bias)
        o_a = o_a.reshape(B, S, ATTN_WIDTH).astype(x.dtype) * jax.nn.silu(z_a)

        heads_b = lambda t, e: t.astype(jnp.float32).reshape(B, S, HGRN_HEADS, e)
        lb = lower_bounds[l].reshape(HGRN_HEADS, HGRN_KEY_DIM)
        o_b = hgrn2_recurrence(jax.nn.silu(heads_b(q_b, HGRN_KEY_DIM)), heads_b(f_b, HGRN_KEY_DIM),
                               heads_b(i_b, HGRN_VAL_DIM), lb)
        o_b = rmsnorm(o_b, hgrn_onorm_g[l]).reshape(B, S, HGRN_WIDTH).astype(x.dtype) * jax.nn.silu(z_b)

        y = jax.nn.sigmoid(g_a) * (o_a @ w_branch_a[l]) + jax.nn.sigmoid(g_b) * (o_b @ w_branch_b[l])
        x = x + gate[:, None] * (y @ w_out[l])
    return rmsnorm(x, final_g)
```

```python
import functools
import math

import numpy as np
import jax
import jax.numpy as jnp
from jax import lax
from jax.experimental import pallas as pl
from jax.experimental.pallas import tpu as pltpu

D_MODEL = 1024
ATTN_HEADS = 8
ATTN_HEAD_DIM = 64
ATTN_WIDTH = ATTN_HEADS * ATTN_HEAD_DIM
DILATED_PATTERNS = ((128, 1), (512, 4), (2048, 16))
ATTN_BLOCK = 128
N_BUCKETS = 32
MAX_DISTANCE = 2048
NEG_INF = -1e30
HGRN_HEADS = 8
HGRN_KEY_DIM = 128
HGRN_VAL_DIM = 128
HGRN_WIDTH = HGRN_HEADS * HGRN_VAL_DIM
EPS = 1e-6

V7X_LANES = 128

F32 = jnp.float32
BF16 = jnp.bfloat16

_MAIN_COLS = {}
_off = 0
for _name, _w in (("qa", ATTN_WIDTH), ("ka", ATTN_WIDTH), ("va", ATTN_WIDTH), ("za", ATTN_WIDTH),
                  ("qb", HGRN_WIDTH), ("ib", HGRN_WIDTH), ("zb", HGRN_WIDTH),
                  ("ga", D_MODEL), ("gb", D_MODEL)):
    _MAIN_COLS[_name] = (_off, _w)
    _off += _w
MAIN_WIDTH = _off
PROJ_TN = 512
_PROJ_KINDS = (["qscale"] + ["id"] * 2 + ["silu"] + ["silu"] * 2 + ["forget"] * 2
               + ["id"] * 2 + ["silu"] * 2 + ["sigmoid"] * 4)
_F_TILE0 = _PROJ_KINDS.index("forget")
_F_TILES = _PROJ_KINDS.count("forget")


def _sigmoid(x):
    return 1.0 / (1.0 + jnp.exp(-x))


def _any_eq(j, values):
    return functools.reduce(jnp.logical_or, [j == v for v in values])


def _mod_kernel(c_ref, w_ref, b_ref, o_ref):
    c = c_ref[...]
    sc = c * _sigmoid(c)
    o_ref[...] = jnp.dot(sc, w_ref[...], precision=lax.Precision.HIGHEST,
                         preferred_element_type=F32) + b_ref[...]


def _lower_bound_kernel(hl_ref, o_ref):
    hl = hl_ref[...]
    m = jnp.max(hl, axis=0, keepdims=True)
    e = jnp.exp(hl - m)
    o_ref[...] = e[0:1, :] / jnp.sum(e, axis=0, keepdims=True)


def _bias_table_kernel(rb_ref, bucket_ref, o_ref):
    bk = bucket_ref[...]
    for h in range(ATTN_HEADS):
        acc = jnp.full(bk.shape, NEG_INF, F32)
        for u in range(N_BUCKETS):
            acc = jnp.where(bk == u, rb_ref[u, h], acc)
        o_ref[h] = acc


def _bucket_tables():
    qi = np.arange(ATTN_BLOCK)[:, None]
    kj = np.arange(2 * ATTN_BLOCK)[None, :]
    delta = qi + ATTN_BLOCK - kj
    max_exact = N_BUCKETS // 2
    tabs = []
    for window, dilation in DILATED_PATTERNS:
        span = window // dilation
        band = (delta >= 0) & (delta <= span)
        dist = np.clip(delta, 0, None) * dilation
        n = dist.astype(np.float32)
        large = max_exact + (np.log(np.maximum(n, 1.0) / max_exact)
                             / math.log(MAX_DISTANCE / max_exact)
                             * (N_BUCKETS - max_exact)).astype(np.int32)
        large = np.minimum(large, N_BUCKETS - 1)
        bucket = np.where(dist < max_exact, dist, large)
        tabs.append(np.where(band, bucket, -1).astype(np.int32))
    return np.stack(tabs, 0)


def _inproj_kernel(x_ref, mod_ref, g_ref, lb_ref, w_ref, om_ref, of_ref, h_ref, acc_ref):
    j = pl.program_id(1)

    @pl.when(j == 0)
    def _():
        x = x_ref[...]
        ms = jnp.mean(x * x, axis=-1, keepdims=True)
        y = x * lax.rsqrt(ms + EPS) * g_ref[...]
        shift = mod_ref[:, 0:D_MODEL]
        scale = mod_ref[:, D_MODEL:2 * D_MODEL]
        h_ref[...] = (y * (1.0 + scale) + shift).astype(BF16)

    acc_ref[...] = jnp.dot(h_ref[...], w_ref[...], preferred_element_type=F32)

    def tiles(kind):
        return [t for t, k in enumerate(_PROJ_KINDS) if k == kind]

    @pl.when(_any_eq(j, tiles("qscale")))
    def _():
        om_ref[...] = (acc_ref[...] * (ATTN_HEAD_DIM ** -0.5)).astype(BF16)

    @pl.when(_any_eq(j, tiles("id")))
    def _():
        om_ref[...] = acc_ref[...].astype(BF16)

    @pl.when(_any_eq(j, tiles("silu")))
    def _():
        a = acc_ref[...]
        om_ref[...] = (a * _sigmoid(a)).astype(BF16)

    @pl.when(_any_eq(j, tiles("sigmoid")))
    def _():
        om_ref[...] = _sigmoid(acc_ref[...]).astype(BF16)

    @pl.when(_any_eq(j, tiles("forget")))
    def _():
        lb = lb_ref[...]
        of_ref[...] = lb + (1.0 - lb) * _sigmoid(acc_ref[...])


def _attn_kernel(q_ref, kp_ref, kc_ref, vp_ref, vc_ref, bias_ref, o_ref, lse_ref):
    n = pl.program_id(2)
    blk = ATTN_BLOCK
    lane = lax.broadcasted_iota(jnp.int32, (blk, V7X_LANES), 1)
    col = lax.broadcasted_iota(jnp.int32, (blk, 2 * blk), 1)
    no_prev = jnp.logical_and(col < blk, n == 0)
    lse_all = jnp.zeros((blk, V7X_LANES), F32)
    for hp in range(ATTN_HEADS // 2):
        sl = slice(hp * V7X_LANES, (hp + 1) * V7X_LANES)
        qp = q_ref[:, sl]
        kpair = jnp.concatenate([kp_ref[:, sl], kc_ref[:, sl]], axis=0)
        vpair = jnp.concatenate([vp_ref[:, sl], vc_ref[:, sl]], axis=0)
        outs = []
        for hh in range(2):
            h = 2 * hp + hh
            in_head = (lane >= hh * ATTN_HEAD_DIM) & (lane < (hh + 1) * ATTN_HEAD_DIM)
            qh = jnp.where(in_head, qp, jnp.zeros_like(qp))
            s = lax.dot_general(qh, kpair, (((1,), (1,)), ((), ())),
                                preferred_element_type=F32)
            s = s + jnp.where(no_prev, NEG_INF, bias_ref[h])
            m = jnp.max(s, axis=-1, keepdims=True)
            p = jnp.exp(s - m)
            den = jnp.sum(p, axis=-1, keepdims=True)
            o = jnp.dot(p.astype(BF16), vpair, preferred_element_type=F32)
            outs.append(o / den)
            lse_all = jnp.where(lane == h, m + jnp.log(den), lse_all)
        o_pair = jnp.where(lane < ATTN_HEAD_DIM, outs[0], outs[1])
        o_ref[:, sl] = o_pair.astype(BF16)
    lse_ref[...] = lse_all


def _hgrn_kernel(q_ref, f_ref, i_ref, z_ref, g_ref, o_ref, state_ref, *, chunk, n_chunks):
    @pl.when(pl.program_id(1) == 0)
    def _():
        state_ref[...] = jnp.zeros_like(state_ref)

    C = chunk
    n_levels = C.bit_length() - 1
    row = lax.broadcasted_iota(jnp.int32, (C, HGRN_KEY_DIM), 0)
    tt = lax.broadcasted_iota(jnp.int32, (C, C), 0)
    ss = lax.broadcasted_iota(jnp.int32, (C, C), 1)
    txs = jnp.bitwise_xor(tt, ss)
    lower = tt > ss
    g_on = g_ref[...]
    nt = (((1,), (1,)), ((), ()))

    def chunk_body(ci, carry):
        r0 = pl.multiple_of(ci * C, C)
        rows = pl.ds(r0, C)
        for h in range(HGRN_HEADS):
            hs = slice(h * HGRN_KEY_DIM, (h + 1) * HGRN_KEY_DIM)
            f = f_ref[rows, hs]
            q = q_ref[rows, hs].astype(F32)
            k = 1.0 - f
            v = i_ref[rows, hs]
            a = jnp.where(tt == ss,
                          lax.dot_general(q.astype(BF16), k.astype(BF16), nt,
                                          preferred_element_type=F32), 0.0)
            pq = f
            sk = jnp.ones_like(f)
            tot = f
            for lvl in range(n_levels):
                m = 1 << lvl
                odd = jnp.bitwise_and(row, m) != 0
                e = jnp.where(odd, pq, sk)
                pm = lax.dot_general((q * e).astype(BF16), (k * e).astype(BF16), nt,
                                     preferred_element_type=F32)
                mask = lower & (txs >= m) & (txs < 2 * m)
                a = jnp.where(mask, pm, a)
                tot_dn = pltpu.roll(tot, m, 0)
                tot_up = pltpu.roll(tot, C - m, 0)
                pq = jnp.where(odd, pq * tot_dn, pq)
                sk = jnp.where(odd, sk, sk * tot_up)
                tot = tot * jnp.where(odd, tot_dn, tot_up)
            st = state_ref[h]
            o_intra = jnp.dot(a.astype(BF16), v, preferred_element_type=F32)
            o_inter = lax.dot_general((q * pq).astype(BF16), st.astype(BF16), nt,
                                      preferred_element_type=F32)
            o = o_intra + o_inter
            upd = lax.dot_general(v, (k * sk).astype(BF16), (((0,), (0,)), ((), ())),
                                  preferred_element_type=F32)
            state_ref[h] = st * tot[0:1, :] + upd
            ms = jnp.mean(o * o, axis=-1, keepdims=True)
            y = o * lax.rsqrt(ms + EPS) * g_on
            o_ref[rows, hs] = (y * z_ref[rows, hs].astype(F32)).astype(BF16)
        return carry

    lax.fori_loop(0, n_chunks, chunk_body, 0)


def _merge_kernel(o1_ref, o2_ref, o3_ref, l1_ref, l2_ref, l3_ref, za_ref, ob_ref,
                  sga_ref, sgb_ref, x_ref, mod_ref, wa_ref, wb_ref, wo_ref, fg_ref,
                  ex_ref, out_ref):
    l1, l2, l3 = l1_ref[...], l2_ref[...], l3_ref[...]
    mx = jnp.maximum(jnp.maximum(l1, l2), l3)
    e1, e2, e3 = jnp.exp(l1 - mx), jnp.exp(l2 - mx), jnp.exp(l3 - mx)
    inv = 1.0 / (e1 + e2 + e3)
    ex = ex_ref[...]

    def expand(w):
        hi = w.astype(BF16)
        lo = (w - hi.astype(F32)).astype(BF16)
        return (jnp.dot(hi, ex, preferred_element_type=F32)
                + jnp.dot(lo, ex, preferred_element_type=F32))

    oa = (expand(e1 * inv) * o1_ref[...].astype(F32)
          + expand(e2 * inv) * o2_ref[...].astype(F32)
          + expand(e3 * inv) * o3_ref[...].astype(F32))
    oa = (oa * za_ref[...].astype(F32)).astype(BF16)
    ya = jnp.dot(oa, wa_ref[...], preferred_element_type=F32)
    yb = jnp.dot(ob_ref[...], wb_ref[...], preferred_element_type=F32)
    y = sga_ref[...].astype(F32) * ya + sgb_ref[...].astype(F32) * yb
    z = jnp.dot(y.astype(BF16), wo_ref[...], preferred_element_type=F32)
    gate = mod_ref[:, 2 * D_MODEL:3 * D_MODEL]
    xo = x_ref[...] + gate * z
    ms = jnp.mean(xo * xo, axis=-1, keepdims=True)
    out_ref[...] = xo * lax.rsqrt(ms + EPS) * fg_ref[...]


def _cparams(sem, vmem_mb):
    return pltpu.CompilerParams(dimension_semantics=sem,
                                vmem_limit_bytes=vmem_mb * 1024 * 1024)


def kernel(x, c, w_ada, b_ada, norm_g, w_in, hgrn_onorm_g, w_branch_a, w_branch_b, w_out,
           rel_bias, hgrn_lb, final_g):
    B, S, D = x.shape
    assert D == D_MODEL and w_ada.shape[0] == 1, "single-layer kernel"
    N = B * S
    x2 = x.reshape(N, D)

    c8 = jnp.pad(c, ((0, 8 - B), (0, 0)))
    mod = pl.pallas_call(
        _mod_kernel,
        grid=(3 * D // 512,),
        in_specs=[pl.BlockSpec((8, D), lambda j: (0, 0)),
                  pl.BlockSpec((D, 512), lambda j: (0, j)),
                  pl.BlockSpec((1, 512), lambda j: (0, j))],
        out_specs=pl.BlockSpec((8, 512), lambda j: (0, j)),
        out_shape=jax.ShapeDtypeStruct((8, 3 * D), F32),
        name="adaln_mod",
    )(c8, w_ada[0], b_ada[0].reshape(1, 3 * D))
    mod3 = mod.reshape(8, 1, 3 * D)

    lb = pl.pallas_call(
        _lower_bound_kernel,
        out_shape=jax.ShapeDtypeStruct((1, HGRN_WIDTH), F32),
        name="hgrn_lower_bound",
    )(hgrn_lb)

    n_pat = len(DILATED_PATTERNS)
    bias_tab = pl.pallas_call(
        _bias_table_kernel,
        grid=(n_pat,),
        in_specs=[pl.BlockSpec(memory_space=pltpu.SMEM),
                  pl.BlockSpec((None, ATTN_BLOCK, 2 * ATTN_BLOCK), lambda g: (g, 0, 0))],
        out_specs=pl.BlockSpec((None, ATTN_HEADS, ATTN_BLOCK, 2 * ATTN_BLOCK),
                               lambda g: (g, 0, 0, 0)),
        out_shape=jax.ShapeDtypeStruct((n_pat, ATTN_HEADS, ATTN_BLOCK, 2 * ATTN_BLOCK), F32),
        name="rel_bias_table",
    )(rel_bias, jnp.asarray(_bucket_tables()))

    tm = 512
    tiles_per_b = S // tm
    n_jt = len(_PROJ_KINDS)

    def main_col(j):
        return jnp.where(j < _F_TILE0, j, jnp.maximum(j - _F_TILES, _F_TILE0 - 1))

    def f_col(j):
        return jnp.clip(j - _F_TILE0, 0, _F_TILES - 1)

    main, fgate = pl.pallas_call(
        _inproj_kernel,
        grid=(N // tm, n_jt),
        in_specs=[pl.BlockSpec((tm, D), lambda i, j: (i, 0)),
                  pl.BlockSpec((None, 1, 3 * D), lambda i, j: (i // tiles_per_b, 0, 0)),
                  pl.BlockSpec((1, D), lambda i, j: (0, 0)),
                  pl.BlockSpec((1, PROJ_TN), lambda i, j: (0, f_col(j))),
                  pl.BlockSpec((D, PROJ_TN), lambda i, j: (0, j))],
        out_specs=[pl.BlockSpec((tm, PROJ_TN), lambda i, j: (i, main_col(j))),
                   pl.BlockSpec((tm, PROJ_TN), lambda i, j: (i, f_col(j)))],
        out_shape=[jax.ShapeDtypeStruct((N, MAIN_WIDTH), BF16),
                   jax.ShapeDtypeStruct((N, HGRN_WIDTH), F32)],
        scratch_shapes=[pltpu.VMEM((tm, D), BF16), pltpu.VMEM((tm, PROJ_TN), F32)],
        compiler_params=_cparams(("arbitrary", "arbitrary"), 40),
        name="inproj",
    )(x2, mod3, norm_g[0].reshape(1, D), lb, w_in[0].astype(BF16))

    qa_t = _MAIN_COLS["qa"][0] // ATTN_WIDTH
    ka_t = _MAIN_COLS["ka"][0] // ATTN_WIDTH
    va_t = _MAIN_COLS["va"][0] // ATTN_WIDTH
    main_tiles = MAIN_WIDTH // ATTN_WIDTH
    attn_outs = []
    for g, (window, dil) in enumerate(DILATED_PATTERNS):
        assert window // dil == ATTN_BLOCK
        L = S // dil
        nb = L // ATTN_BLOCK
        main_v = main.reshape(B, L, dil * MAIN_WIDTH)

        def in_spec(col_tile, prev):
            def imap(b, r, n, col_tile=col_tile, prev=prev):
                nn = jnp.maximum(n - 1, 0) if prev else n
                return (b, nn, r * main_tiles + col_tile)
            return pl.BlockSpec((None, ATTN_BLOCK, ATTN_WIDTH), imap)

        o_g, lse_g = pl.pallas_call(
            _attn_kernel,
            grid=(B, dil, nb),
            in_specs=[in_spec(qa_t, False), in_spec(ka_t, True), in_spec(ka_t, False),
                      in_spec(va_t, True), in_spec(va_t, False),
                      pl.BlockSpec((None, ATTN_HEADS, ATTN_BLOCK, 2 * ATTN_BLOCK),
                                   lambda b, r, n, g=g: (g, 0, 0, 0))],
            out_specs=[pl.BlockSpec((None, ATTN_BLOCK, ATTN_WIDTH), lambda b, r, n: (b, n, r)),
                       pl.BlockSpec((None, ATTN_BLOCK, V7X_LANES), lambda b, r, n: (b, n, r))],
            out_shape=[jax.ShapeDtypeStruct((B, L, dil * ATTN_WIDTH), BF16),
                       jax.ShapeDtypeStruct((B, L, dil * V7X_LANES), F32)],
            compiler_params=_cparams(("arbitrary", "arbitrary", "arbitrary"), 32),
            name=f"dilated_attn_d{dil}",
        )(main_v, main_v, main_v, main_v, main_v, bias_tab)
        attn_outs.append((o_g.reshape(N, ATTN_WIDTH), lse_g.reshape(N, V7X_LANES)))

    th = 512
    chunk = 64
    hw_t = HGRN_WIDTH
    qb_t = _MAIN_COLS["qb"][0] // hw_t
    ib_t = _MAIN_COLS["ib"][0] // hw_t
    zb_t = _MAIN_COLS["zb"][0] // hw_t
    main_b = main.reshape(B, S, MAIN_WIDTH)
    ob = pl.pallas_call(
        functools.partial(_hgrn_kernel, chunk=chunk, n_chunks=th // chunk),
        grid=(B, S // th),
        in_specs=[pl.BlockSpec((None, th, hw_t), lambda b, s: (b, s, qb_t)),
                  pl.BlockSpec((None, th, hw_t), lambda b, s: (b, s, 0)),
                  pl.BlockSpec((None, th, hw_t), lambda b, s: (b, s, ib_t)),
                  pl.BlockSpec((None, th, hw_t), lambda b, s: (b, s, zb_t)),
                  pl.BlockSpec((1, HGRN_VAL_DIM), lambda b, s: (0, 0))],
        out_specs=pl.BlockSpec((None, th, hw_t), lambda b, s: (b, s, 0)),
        out_shape=jax.ShapeDtypeStruct((B, S, hw_t), BF16),
        scratch_shapes=[pltpu.VMEM((HGRN_HEADS, HGRN_VAL_DIM, HGRN_KEY_DIM), F32)],
        compiler_params=_cparams(("arbitrary", "arbitrary"), 32),
        name="hgrn2",
    )(main_b, fgate.reshape(B, S, hw_t), main_b, main_b, hgrn_onorm_g[0].reshape(1, HGRN_VAL_DIM))
    ob = ob.reshape(N, hw_t)

    tk = 256
    tiles_per_b5 = S // tk
    za_t = _MAIN_COLS["za"][0] // ATTN_WIDTH
    ga_t = _MAIN_COLS["ga"][0] // D
    gb_t = _MAIN_COLS["gb"][0] // D
    expand_mat = np.zeros((V7X_LANES, ATTN_WIDTH), np.float32)
    for h in range(ATTN_HEADS):
        expand_mat[h, h * ATTN_HEAD_DIM:(h + 1) * ATTN_HEAD_DIM] = 1.0
    (o1, l1), (o2, l2), (o3, l3) = attn_outs
    row_spec = lambda w, t=0: pl.BlockSpec((tk, w), lambda i, t=t: (i, t))
    full_spec = lambda a, b: pl.BlockSpec((a, b), lambda i: (0, 0))
    out = pl.pallas_call(
        _merge_kernel,
        grid=(N // tk,),
        in_specs=[row_spec(ATTN_WIDTH), row_spec(ATTN_WIDTH), row_spec(ATTN_WIDTH),
                  row_spec(V7X_LANES), row_spec(V7X_LANES), row_spec(V7X_LANES),
                  row_spec(ATTN_WIDTH, za_t), row_spec(HGRN_WIDTH),
                  row_spec(D, ga_t), row_spec(D, gb_t), row_spec(D),
                  pl.BlockSpec((None, 1, 3 * D), lambda i: (i // tiles_per_b5, 0, 0)),
                  full_spec(ATTN_WIDTH, D), full_spec(HGRN_WIDTH, D), full_spec(D, D),
                  full_spec(1, D), full_spec(V7X_LANES, ATTN_WIDTH)],
        out_specs=pl.BlockSpec((tk, D), lambda i: (i, 0)),
        out_shape=jax.ShapeDtypeStruct((N, D), F32),
        compiler_params=_cparams(("arbitrary",), 40),
        name="gated_merge",
    )(o1, o2, o3, l1, l2, l3, main, ob, main, main, x2, mod3,
      w_branch_a[0].astype(BF16), w_branch_b[0].astype(BF16), w_out[0].astype(BF16),
      final_g.reshape(1, D), jnp.asarray(expand_mat, BF16))
    return out.reshape(B, S, D)
```

```python
import functools
import math

import numpy as np
import jax
import jax.numpy as jnp
from jax import lax
from jax.experimental import pallas as pl
from jax.experimental.pallas import tpu as pltpu

D_MODEL = 1024
ATTN_HEADS = 8
ATTN_HEAD_DIM = 64
ATTN_WIDTH = ATTN_HEADS * ATTN_HEAD_DIM
DILATED_PATTERNS = ((128, 1), (512, 4), (2048, 16))
ATTN_BLOCK = 128
N_BUCKETS = 32
MAX_DISTANCE = 2048
NEG_INF = -1e30
HGRN_HEADS = 8
HGRN_KEY_DIM = 128
HGRN_VAL_DIM = 128
HGRN_WIDTH = HGRN_HEADS * HGRN_VAL_DIM
EPS = 1e-6

V7X_LANES = 128

F32 = jnp.float32
BF16 = jnp.bfloat16

_MAIN_COLS = {}
_off = 0
for _name, _w in (("qa", ATTN_WIDTH), ("ka", ATTN_WIDTH), ("va", ATTN_WIDTH), ("za", ATTN_WIDTH),
                  ("qb", HGRN_WIDTH), ("ib", HGRN_WIDTH), ("zb", HGRN_WIDTH),
                  ("ga", D_MODEL), ("gb", D_MODEL)):
    _MAIN_COLS[_name] = (_off, _w)
    _off += _w
MAIN_WIDTH = _off
PROJ_TN = 512
_PROJ_KINDS = (["qscale"] + ["kv"] * 2 + ["silu"] + ["silu"] * 2 + ["forget"] * 2
               + ["id"] * 2 + ["silu"] * 2 + ["sigmoid"] * 4)
_F_TILE0 = _PROJ_KINDS.index("forget")
_F_TILES = _PROJ_KINDS.count("forget")
_QKV_TILES = 3

PERM_DIL = 16
PERM_TILE = 512
PERM_SLAB = PERM_TILE // PERM_DIL
ATTN_SUPER = PERM_DIL * ATTN_BLOCK


def _sigmoid(x):
    return 1.0 / (1.0 + jnp.exp(-x))


def _any_eq(j, values):
    return functools.reduce(jnp.logical_or, [j == v for v in values])


def _mod_kernel(c_ref, w_ref, b_ref, o_ref):
    c = c_ref[...]
    sc = c * _sigmoid(c)
    o_ref[...] = jnp.dot(sc, w_ref[...], precision=lax.Precision.HIGHEST,
                         preferred_element_type=F32) + b_ref[...]


def _lower_bound_kernel(hl_ref, o_ref):
    hl = hl_ref[...]
    m = jnp.max(hl, axis=0, keepdims=True)
    e = jnp.exp(hl - m)
    o_ref[...] = e[0:1, :] / jnp.sum(e, axis=0, keepdims=True)


def _bias_table_kernel(rb_ref, bucket_ref, o_ref):
    bk = bucket_ref[...]
    for h in range(ATTN_HEADS):
        acc = jnp.full(bk.shape, NEG_INF, F32)
        for u in range(N_BUCKETS):
            acc = jnp.where(bk == u, rb_ref[u, h], acc)
        o_ref[h] = acc


def _bucket_tables():
    qi = np.arange(ATTN_BLOCK)[:, None]
    kj = np.arange(2 * ATTN_BLOCK)[None, :]
    delta = qi + ATTN_BLOCK - kj
    max_exact = N_BUCKETS // 2
    tabs = []
    for window, dilation in DILATED_PATTERNS:
        span = window // dilation
        band = (delta >= 0) & (delta <= span)
        dist = np.clip(delta, 0, None) * dilation
        n = dist.astype(np.float32)
        large = max_exact + (np.log(np.maximum(n, 1.0) / max_exact)
                             / math.log(MAX_DISTANCE / max_exact)
                             * (N_BUCKETS - max_exact)).astype(np.int32)
        large = np.minimum(large, N_BUCKETS - 1)
        bucket = np.where(dist < max_exact, dist, large)
        tab = np.where(band, bucket, -1).astype(np.int32)
        if dilation > 1:
            order = _gather_order(dilation)
            cols = np.concatenate([order, ATTN_BLOCK + order])
            tab = tab[order][:, cols]
        tabs.append(tab)
    return np.stack(tabs, 0)


def _gather_order(dilation):
    per_tile = PERM_DIL // dilation
    slab = np.arange(ATTN_BLOCK) // PERM_SLAB
    m = np.arange(ATTN_BLOCK) % PERM_SLAB
    if per_tile == 1:
        return slab * PERM_SLAB + m
    assert per_tile * PERM_SLAB == ATTN_BLOCK
    return per_tile * m + slab


def _inproj_kernel(x_ref, mod_ref, g_ref, lb_ref, w_ref, om_ref, of_ref, op_ref, h_ref, acc_ref,
                   accl_ref):
    j = pl.program_id(1)

    def write_perm(scale):
        for c in range(PROJ_TN // V7X_LANES):
            cs = slice(c * V7X_LANES, (c + 1) * V7X_LANES)
            accl_ref[c] = acc_ref[:, cs]
            for r in range(PERM_DIL):
                rows = accl_ref[c, pl.ds(r, PERM_SLAB, stride=PERM_DIL), :]
                op_ref[r * PERM_SLAB:(r + 1) * PERM_SLAB, cs] = (rows * scale).astype(BF16)

    @pl.when(j == 0)
    def _():
        x = x_ref[...]
        ms = jnp.mean(x * x, axis=-1, keepdims=True)
        y = x * lax.rsqrt(ms + EPS) * g_ref[...]
        shift = mod_ref[:, 0:D_MODEL]
        scale = mod_ref[:, D_MODEL:2 * D_MODEL]
        h_ref[...] = (y * (1.0 + scale) + shift).astype(BF16)

    acc_ref[...] = jnp.dot(h_ref[...], w_ref[...], preferred_element_type=F32)

    def tiles(kind):
        return [t for t, k in enumerate(_PROJ_KINDS) if k == kind]

    @pl.when(_any_eq(j, tiles("qscale")))
    def _():
        om_ref[...] = (acc_ref[...] * (ATTN_HEAD_DIM ** -0.5)).astype(BF16)
        write_perm(ATTN_HEAD_DIM ** -0.5)

    @pl.when(_any_eq(j, tiles("kv")))
    def _():
        om_ref[...] = acc_ref[...].astype(BF16)
        write_perm(1.0)

    @pl.when(_any_eq(j, tiles("id")))
    def _():
        om_ref[...] = acc_ref[...].astype(BF16)

    @pl.when(_any_eq(j, tiles("silu")))
    def _():
        a = acc_ref[...]
        om_ref[...] = (a * _sigmoid(a)).astype(BF16)

    @pl.when(_any_eq(j, tiles("sigmoid")))
    def _():
        om_ref[...] = _sigmoid(acc_ref[...]).astype(BF16)

    @pl.when(_any_eq(j, tiles("forget")))
    def _():
        lb = lb_ref[...]
        of_ref[...] = lb + (1.0 - lb) * _sigmoid(acc_ref[...])


def _attn_kernel(q_ref, kp_ref, kc_ref, vp_ref, vc_ref, bias_ref, o_ref, lse_ref):
    n = pl.program_id(2)
    blk = ATTN_BLOCK
    lane = lax.broadcasted_iota(jnp.int32, (blk, V7X_LANES), 1)
    col = lax.broadcasted_iota(jnp.int32, (blk, 2 * blk), 1)
    no_prev = jnp.logical_and(col < blk, n == 0)
    lse_all = jnp.zeros((blk, V7X_LANES), F32)
    for hp in range(ATTN_HEADS // 2):
        sl = slice(hp * V7X_LANES, (hp + 1) * V7X_LANES)
        qp = q_ref[:, sl]
        kpair = jnp.concatenate([kp_ref[:, sl], kc_ref[:, sl]], axis=0)
        vpair = jnp.concatenate([vp_ref[:, sl], vc_ref[:, sl]], axis=0)
        o_pair, lse_all = _attn_head_pair(qp, kpair, vpair, bias_ref, hp, no_prev, lane, lse_all)
        o_ref[:, sl] = o_pair.astype(BF16)
    lse_ref[...] = lse_all


def _attn_head_pair(q, k, v, bias_ref, hp, no_prev, lane, lse_all):
    outs = []
    for hh in range(2):
        h = 2 * hp + hh
        in_head = (lane >= hh * ATTN_HEAD_DIM) & (lane < (hh + 1) * ATTN_HEAD_DIM)
        qh = jnp.where(in_head, q, jnp.zeros_like(q))
        s = lax.dot_general(qh, k, (((1,), (1,)), ((), ())), preferred_element_type=F32)
        s = s + jnp.where(no_prev, NEG_INF, bias_ref[h])
        m = jnp.max(s, axis=-1, keepdims=True)
        p = jnp.exp(s - m)
        den = jnp.sum(p, axis=-1, keepdims=True)
        o = jnp.dot(p.astype(BF16), v, preferred_element_type=F32)
        outs.append(o / den)
        lse_all = jnp.where(lane == h, m + jnp.log(den), lse_all)
    return jnp.where(lane < ATTN_HEAD_DIM, outs[0], outs[1]), lse_all


def _attn_perm_kernel(q_ref, kp_ref, kc_ref, vp_ref, vc_ref, bias_ref, o_ref, lse_ref,
                      o_scr, *ext, dil):
    n = pl.program_id(1)
    blk_rows = ATTN_BLOCK
    n_slab = ATTN_BLOCK // PERM_SLAB
    lane = lax.broadcasted_iota(jnp.int32, (blk_rows, V7X_LANES), 1)
    col = lax.broadcasted_iota(jnp.int32, (blk_rows, 2 * blk_rows), 1)
    prev_rows = kp_ref.shape[0]
    if ext:
        kext, vext = ext
        kext[0:prev_rows] = kp_ref[...]
        kext[prev_rows:] = kc_ref[...]
        vext[0:prev_rows] = vp_ref[...]
        vext[prev_rows:] = vc_ref[...]

    def body(blk, carry):
        if dil == PERM_DIL:
            offs = [t * PERM_TILE + blk * PERM_SLAB for t in range(n_slab)]
            tok0 = [t * PERM_TILE + blk for t in range(n_slab)]
            first = n == 0
        else:
            tile = lax.shift_right_logical(blk, 2)
            res = jnp.bitwise_and(blk, dil - 1)
            offs = [tile * PERM_TILE + (res + dil * j) * PERM_SLAB for j in range(n_slab)]
            tok0 = [tile * PERM_TILE + res + dil * j for j in range(n_slab)]
            first = jnp.logical_and(n == 0, tile == 0)
        offs = [pl.multiple_of(o, PERM_SLAB) for o in offs]
        no_prev = jnp.logical_and(col < blk_rows, first)
        lse_all = jnp.zeros((blk_rows, V7X_LANES), F32)
        for hp in range(ATTN_HEADS // 2):
            sl = slice(hp * V7X_LANES, (hp + 1) * V7X_LANES)
            q = jnp.concatenate([q_ref[pl.ds(o, PERM_SLAB), sl] for o in offs], axis=0)
            if ext:
                k = jnp.concatenate([kext[pl.ds(o, PERM_SLAB), sl] for o in offs]
                                    + [kext[pl.ds(o + prev_rows, PERM_SLAB), sl] for o in offs], axis=0)
                v = jnp.concatenate([vext[pl.ds(o, PERM_SLAB), sl] for o in offs]
                                    + [vext[pl.ds(o + prev_rows, PERM_SLAB), sl] for o in offs], axis=0)
            else:
                k = jnp.concatenate([kp_ref[pl.ds(o, PERM_SLAB), sl] for o in offs]
                                    + [kc_ref[pl.ds(o, PERM_SLAB), sl] for o in offs], axis=0)
                v = jnp.concatenate([vp_ref[pl.ds(o, PERM_SLAB), sl] for o in offs]
                                    + [vc_ref[pl.ds(o, PERM_SLAB), sl] for o in offs], axis=0)
            o_pair, lse_all = _attn_head_pair(q, k, v, bias_ref, hp, no_prev, lane, lse_all)
            for j, t0 in enumerate(tok0):
                o_scr[hp, pl.ds(t0, PERM_SLAB, stride=PERM_DIL), :] = o_pair[j * PERM_SLAB:(j + 1) * PERM_SLAB]
        for j, t0 in enumerate(tok0):
            lse_ref[pl.ds(t0, PERM_SLAB, stride=PERM_DIL), :] = lse_all[j * PERM_SLAB:(j + 1) * PERM_SLAB]
        return carry

    lax.fori_loop(0, ATTN_SUPER // ATTN_BLOCK, body, 0)
    for hp in range(ATTN_HEADS // 2):
        o_ref[:, hp * V7X_LANES:(hp + 1) * V7X_LANES] = o_scr[hp].astype(BF16)


def _hgrn_kernel(q_ref, f_ref, i_ref, z_ref, g_ref, o_ref, state_ref, *, chunk, n_chunks):
    @pl.when(pl.program_id(1) == 0)
    def _():
        state_ref[...] = jnp.zeros_like(state_ref)

    C = chunk
    n_levels = C.bit_length() - 1
    row = lax.broadcasted_iota(jnp.int32, (C, HGRN_KEY_DIM), 0)
    tt = lax.broadcasted_iota(jnp.int32, (C, C), 0)
    ss = lax.broadcasted_iota(jnp.int32, (C, C), 1)
    txs = jnp.bitwise_xor(tt, ss)
    lower = tt > ss
    g_on = g_ref[...]
    nt = (((1,), (1,)), ((), ()))

    def chunk_body(ci, carry):
        r0 = pl.multiple_of(ci * C, C)
        rows = pl.ds(r0, C)
        for h in range(HGRN_HEADS):
            hs = slice(h * HGRN_KEY_DIM, (h + 1) * HGRN_KEY_DIM)
            f = f_ref[rows, hs]
            q = q_ref[rows, hs].astype(F32)
            k = 1.0 - f
            v = i_ref[rows, hs]
            a = jnp.where(tt == ss,
                          lax.dot_general(q.astype(BF16), k.astype(BF16), nt,
                                          preferred_element_type=F32), 0.0)
            pq = f
            sk = jnp.ones_like(f)
            tot = f
            for lvl in range(n_levels):
                m = 1 << lvl
                odd = jnp.bitwise_and(row, m) != 0
                e = jnp.where(odd, pq, sk)
                pm = lax.dot_general((q * e).astype(BF16), (k * e).astype(BF16), nt,
                                     preferred_element_type=F32)
                mask = lower & (txs >= m) & (txs < 2 * m)
                a = jnp.where(mask, pm, a)
                tot_dn = pltpu.roll(tot, m, 0)
                tot_up = pltpu.roll(tot, C - m, 0)
                pq = jnp.where(odd, pq * tot_dn, pq)
                sk = jnp.where(odd, sk, sk * tot_up)
                tot = tot * jnp.where(odd, tot_dn, tot_up)
            st = state_ref[h]
            o_intra = jnp.dot(a.astype(BF16), v, preferred_element_type=F32)
            o_inter = lax.dot_general((q * pq).astype(BF16), st.astype(BF16), nt,
                                      preferred_element_type=F32)
            o = o_intra + o_inter
            upd = lax.dot_general(v, (k * sk).astype(BF16), (((0,), (0,)), ((), ())),
                                  preferred_element_type=F32)
            state_ref[h] = st * tot[0:1, :] + upd
            ms = jnp.mean(o * o, axis=-1, keepdims=True)
            y = o * lax.rsqrt(ms + EPS) * g_on
            o_ref[rows, hs] = (y * z_ref[rows, hs].astype(F32)).astype(BF16)
        return carry

    lax.fori_loop(0, n_chunks, chunk_body, 0)


def _merge_kernel(o1_ref, o2_ref, o3_ref, l1_ref, l2_ref, l3_ref, za_ref, ob_ref,
                  sga_ref, sgb_ref, x_ref, mod_ref, wa_ref, wb_ref, wo_ref, fg_ref,
                  ex_ref, out_ref):
    l1, l2, l3 = l1_ref[...], l2_ref[...], l3_ref[...]
    mx = jnp.maximum(jnp.maximum(l1, l2), l3)
    e1, e2, e3 = jnp.exp(l1 - mx), jnp.exp(l2 - mx), jnp.exp(l3 - mx)
    inv = 1.0 / (e1 + e2 + e3)
    ex = ex_ref[...]

    def expand(w):
        hi = w.astype(BF16)
        lo = (w - hi.astype(F32)).astype(BF16)
        return (jnp.dot(hi, ex, preferred_element_type=F32)
                + jnp.dot(lo, ex, preferred_element_type=F32))

    oa = (expand(e1 * inv) * o1_ref[...].astype(F32)
          + expand(e2 * inv) * o2_ref[...].astype(F32)
          + expand(e3 * inv) * o3_ref[...].astype(F32))
    oa = (oa * za_ref[...].astype(F32)).astype(BF16)
    ya = jnp.dot(oa, wa_ref[...], preferred_element_type=F32)
    yb = jnp.dot(ob_ref[...], wb_ref[...], preferred_element_type=F32)
    y = sga_ref[...].astype(F32) * ya + sgb_ref[...].astype(F32) * yb
    z = jnp.dot(y.astype(BF16), wo_ref[...], preferred_element_type=F32)
    gate = mod_ref[:, 2 * D_MODEL:3 * D_MODEL]
    xo = x_ref[...] + gate * z
    ms = jnp.mean(xo * xo, axis=-1, keepdims=True)
    out_ref[...] = xo * lax.rsqrt(ms + EPS) * fg_ref[...]


def _cparams(sem, vmem_mb):
    return pltpu.CompilerParams(dimension_semantics=sem,
                                vmem_limit_bytes=vmem_mb * 1024 * 1024)


def kernel(x, c, w_ada, b_ada, norm_g, w_in, hgrn_onorm_g, w_branch_a, w_branch_b, w_out,
           rel_bias, hgrn_lb, final_g):
    B, S, D = x.shape
    assert D == D_MODEL and w_ada.shape[0] == 1, "single-layer kernel"
    N = B * S
    x2 = x.reshape(N, D)

    c8 = jnp.pad(c, ((0, 8 - B), (0, 0)))
    mod = pl.pallas_call(
        _mod_kernel,
        grid=(3 * D // 512,),
        in_specs=[pl.BlockSpec((8, D), lambda j: (0, 0)),
                  pl.BlockSpec((D, 512), lambda j: (0, j)),
                  pl.BlockSpec((1, 512), lambda j: (0, j))],
        out_specs=pl.BlockSpec((8, 512), lambda j: (0, j)),
        out_shape=jax.ShapeDtypeStruct((8, 3 * D), F32),
        name="adaln_mod",
    )(c8, w_ada[0], b_ada[0].reshape(1, 3 * D))
    mod3 = mod.reshape(8, 1, 3 * D)

    lb = pl.pallas_call(
        _lower_bound_kernel,
        out_shape=jax.ShapeDtypeStruct((1, HGRN_WIDTH), F32),
        name="hgrn_lower_bound",
    )(hgrn_lb)

    n_pat = len(DILATED_PATTERNS)
    bias_tab = pl.pallas_call(
        _bias_table_kernel,
        grid=(n_pat,),
        in_specs=[pl.BlockSpec(memory_space=pltpu.SMEM),
                  pl.BlockSpec((None, ATTN_BLOCK, 2 * ATTN_BLOCK), lambda g: (g, 0, 0))],
        out_specs=pl.BlockSpec((None, ATTN_HEADS, ATTN_BLOCK, 2 * ATTN_BLOCK),
                               lambda g: (g, 0, 0, 0)),
        out_shape=jax.ShapeDtypeStruct((n_pat, ATTN_HEADS, ATTN_BLOCK, 2 * ATTN_BLOCK), F32),
        name="rel_bias_table",
    )(rel_bias, jnp.asarray(_bucket_tables()))

    tm = 512
    tiles_per_b = S // tm
    n_jt = len(_PROJ_KINDS)

    def main_col(j):
        return jnp.where(j < _F_TILE0, j, jnp.maximum(j - _F_TILES, _F_TILE0 - 1))

    def f_col(j):
        return jnp.clip(j - _F_TILE0, 0, _F_TILES - 1)

    assert tm == PERM_TILE and PROJ_TN == ATTN_WIDTH
    main, fgate, qkv_p = pl.pallas_call(
        _inproj_kernel,
        grid=(N // tm, n_jt),
        in_specs=[pl.BlockSpec((tm, D), lambda i, j: (i, 0)),
                  pl.BlockSpec((None, 1, 3 * D), lambda i, j: (i // tiles_per_b, 0, 0)),
                  pl.BlockSpec((1, D), lambda i, j: (0, 0)),
                  pl.BlockSpec((1, PROJ_TN), lambda i, j: (0, f_col(j))),
                  pl.BlockSpec((D, PROJ_TN), lambda i, j: (0, j))],
        out_specs=[pl.BlockSpec((tm, PROJ_TN), lambda i, j: (i, main_col(j))),
                   pl.BlockSpec((tm, PROJ_TN), lambda i, j: (i, f_col(j))),
                   pl.BlockSpec((tm, PROJ_TN), lambda i, j: (i, jnp.minimum(j, _QKV_TILES - 1)))],
        out_shape=[jax.ShapeDtypeStruct((N, MAIN_WIDTH), BF16),
                   jax.ShapeDtypeStruct((N, HGRN_WIDTH), F32),
                   jax.ShapeDtypeStruct((N, _QKV_TILES * ATTN_WIDTH), BF16)],
        scratch_shapes=[pltpu.VMEM((tm, D), BF16), pltpu.VMEM((tm, PROJ_TN), F32),
                        pltpu.VMEM((PROJ_TN // V7X_LANES, tm, V7X_LANES), F32)],
        compiler_params=_cparams(("arbitrary", "arbitrary"), 40),
        name="inproj",
    )(x2, mod3, norm_g[0].reshape(1, D), lb, w_in[0].astype(BF16))

    qa_t = _MAIN_COLS["qa"][0] // ATTN_WIDTH
    ka_t = _MAIN_COLS["ka"][0] // ATTN_WIDTH
    va_t = _MAIN_COLS["va"][0] // ATTN_WIDTH
    main_tiles = MAIN_WIDTH // ATTN_WIDTH
    attn_outs = []
    qkv_pv = qkv_p.reshape(B, S, _QKV_TILES * ATTN_WIDTH)
    n_super = S // ATTN_SUPER
    for g, (window, dil) in enumerate(DILATED_PATTERNS):
        assert window // dil == ATTN_BLOCK
        if dil > 1:
            prev_rows = ATTN_SUPER if dil == PERM_DIL else PERM_TILE
            per_step = ATTN_SUPER // prev_rows

            def cur_spec(t):
                return pl.BlockSpec((None, ATTN_SUPER, ATTN_WIDTH), lambda b, n, t=t: (b, n, t))

            def prev_spec(t):
                return pl.BlockSpec((None, prev_rows, ATTN_WIDTH),
                                    lambda b, n, t=t: (b, jnp.maximum(n * per_step - 1, 0), t))

            scratch = [pltpu.VMEM((ATTN_WIDTH // V7X_LANES, ATTN_SUPER, V7X_LANES), F32)]
            if dil != PERM_DIL:
                scratch += [pltpu.VMEM((prev_rows + ATTN_SUPER, ATTN_WIDTH), BF16)] * 2
            o_g, lse_g = pl.pallas_call(
                functools.partial(_attn_perm_kernel, dil=dil),
                grid=(B, n_super),
                in_specs=[cur_spec(0), prev_spec(1), cur_spec(1), prev_spec(2), cur_spec(2),
                          pl.BlockSpec((None, ATTN_HEADS, ATTN_BLOCK, 2 * ATTN_BLOCK),
                                       lambda b, n, g=g: (g, 0, 0, 0))],
                out_specs=[pl.BlockSpec((None, ATTN_SUPER, ATTN_WIDTH), lambda b, n: (b, n, 0)),
                           pl.BlockSpec((None, ATTN_SUPER, V7X_LANES), lambda b, n: (b, n, 0))],
                out_shape=[jax.ShapeDtypeStruct((B, S, ATTN_WIDTH), BF16),
                           jax.ShapeDtypeStruct((B, S, V7X_LANES), F32)],
                scratch_shapes=scratch,
                compiler_params=_cparams(("arbitrary", "arbitrary"), 48),
                name=f"dilated_attn_d{dil}",
            )(qkv_pv, qkv_pv, qkv_pv, qkv_pv, qkv_pv, bias_tab)
            attn_outs.append((o_g.reshape(N, ATTN_WIDTH), lse_g.reshape(N, V7X_LANES)))
            continue
        L = S // dil
        nb = L // ATTN_BLOCK
        main_v = main.reshape(B, L, dil * MAIN_WIDTH)

        def in_spec(col_tile, prev):
            def imap(b, r, n, col_tile=col_tile, prev=prev):
                nn = jnp.maximum(n - 1, 0) if prev else n
                return (b, nn, r * main_tiles + col_tile)
            return pl.BlockSpec((None, ATTN_BLOCK, ATTN_WIDTH), imap)

        o_g, lse_g = pl.pallas_call(
            _attn_kernel,
            grid=(B, dil, nb),
            in_specs=[in_spec(qa_t, False), in_spec(ka_t, True), in_spec(ka_t, False),
                      in_spec(va_t, True), in_spec(va_t, False),
                      pl.BlockSpec((None, ATTN_HEADS, ATTN_BLOCK, 2 * ATTN_BLOCK),
                                   lambda b, r, n, g=g: (g, 0, 0, 0))],
            out_specs=[pl.BlockSpec((None, ATTN_BLOCK, ATTN_WIDTH), lambda b, r, n: (b, n, r)),
                       pl.BlockSpec((None, ATTN_BLOCK, V7X_LANES), lambda b, r, n: (b, n, r))],
            out_shape=[jax.ShapeDtypeStruct((B, L, dil * ATTN_WIDTH), BF16),
                       jax.ShapeDtypeStruct((B, L, dil * V7X_LANES), F32)],
            compiler_params=_cparams(("arbitrary", "arbitrary", "arbitrary"), 32),
            name=f"dilated_attn_d{dil}",
        )(main_v, main_v, main_v, main_v, main_v, bias_tab)
        attn_outs.append((o_g.reshape(N, ATTN_WIDTH), lse_g.reshape(N, V7X_LANES)))

    th = 512
    chunk = 64
    hw_t = HGRN_WIDTH
    qb_t = _MAIN_COLS["qb"][0] // hw_t
    ib_t = _MAIN_COLS["ib"][0] // hw_t
    zb_t = _MAIN_COLS["zb"][0] // hw_t
    main_b = main.reshape(B, S, MAIN_WIDTH)
    ob = pl.pallas_call(
        functools.partial(_hgrn_kernel, chunk=chunk, n_chunks=th // chunk),
        grid=(B, S // th),
        in_specs=[pl.BlockSpec((None, th, hw_t), lambda b, s: (b, s, qb_t)),
                  pl.BlockSpec((None, th, hw_t), lambda b, s: (b, s, 0)),
                  pl.BlockSpec((None, th, hw_t), lambda b, s: (b, s, ib_t)),
                  pl.BlockSpec((None, th, hw_t), lambda b, s: (b, s, zb_t)),
                  pl.BlockSpec((1, HGRN_VAL_DIM), lambda b, s: (0, 0))],
        out_specs=pl.BlockSpec((None, th, hw_t), lambda b, s: (b, s, 0)),
        out_shape=jax.ShapeDtypeStruct((B, S, hw_t), BF16),
        scratch_shapes=[pltpu.VMEM((HGRN_HEADS, HGRN_VAL_DIM, HGRN_KEY_DIM), F32)],
        compiler_params=_cparams(("arbitrary", "arbitrary"), 32),
        name="hgrn2",
    )(main_b, fgate.reshape(B, S, hw_t), main_b, main_b, hgrn_onorm_g[0].reshape(1, HGRN_VAL_DIM))
    ob = ob.reshape(N, hw_t)

    tk = 256
    tiles_per_b5 = S // tk
    za_t = _MAIN_COLS["za"][0] // ATTN_WIDTH
    ga_t = _MAIN_COLS["ga"][0] // D
    gb_t = _MAIN_COLS["gb"][0] // D
    expand_mat = np.zeros((V7X_LANES, ATTN_WIDTH), np.float32)
    for h in range(ATTN_HEADS):
        expand_mat[h, h * ATTN_HEAD_DIM:(h + 1) * ATTN_HEAD_DIM] = 1.0
    (o1, l1), (o2, l2), (o3, l3) = attn_outs
    row_spec = lambda w, t=0: pl.BlockSpec((tk, w), lambda i, t=t: (i, t))
    full_spec = lambda a, b: pl.BlockSpec((a, b), lambda i: (0, 0))
    out = pl.pallas_call(
        _merge_kernel,
        grid=(N // tk,),
        in_specs=[row_spec(ATTN_WIDTH), row_spec(ATTN_WIDTH), row_spec(ATTN_WIDTH),
                  row_spec(V7X_LANES), row_spec(V7X_LANES), row_spec(V7X_LANES),
                  row_spec(ATTN_WIDTH, za_t), row_spec(HGRN_WIDTH),
                  row_spec(D, ga_t), row_spec(D, gb_t), row_spec(D),
                  pl.BlockSpec((None, 1, 3 * D), lambda i: (i // tiles_per_b5, 0, 0)),
                  full_spec(ATTN_WIDTH, D), full_spec(HGRN_WIDTH, D), full_spec(D, D),
                  full_spec(1, D), full_spec(V7X_LANES, ATTN_WIDTH)],
        out_specs=pl.BlockSpec((tk, D), lambda i: (i, 0)),
        out_shape=jax.ShapeDtypeStruct((N, D), F32),
        compiler_params=_cparams(("arbitrary",), 40),
        name="gated_merge",
    )(o1, o2, o3, l1, l2, l3, main, ob, main, main, x2, mod3,
      w_branch_a[0].astype(BF16), w_branch_b[0].astype(BF16), w_out[0].astype(BF16),
      final_g.reshape(1, D), jnp.asarray(expand_mat, BF16))
    return out.reshape(B, S, D)
```

```python
import functools
import math

import numpy as np
import jax
import jax.numpy as jnp
from jax import lax
from jax.experimental import pallas as pl
from jax.experimental.pallas import tpu as pltpu

D_MODEL = 1024
ATTN_HEADS = 8
ATTN_HEAD_DIM = 64
ATTN_WIDTH = ATTN_HEADS * ATTN_HEAD_DIM
DILATED_PATTERNS = ((128, 1), (512, 4), (2048, 16))
ATTN_BLOCK = 128
N_BUCKETS = 32
MAX_DISTANCE = 2048
NEG_INF = -1e30
HGRN_HEADS = 8
HGRN_KEY_DIM = 128
HGRN_VAL_DIM = 128
HGRN_WIDTH = HGRN_HEADS * HGRN_VAL_DIM
EPS = 1e-6

V7X_LANES = 128

F32 = jnp.float32
BF16 = jnp.bfloat16

_MAIN_COLS = {}
_off = 0
for _name, _w in (("qa", ATTN_WIDTH), ("ka", ATTN_WIDTH), ("va", ATTN_WIDTH), ("za", ATTN_WIDTH),
                  ("qb", HGRN_WIDTH), ("ib", HGRN_WIDTH), ("zb", HGRN_WIDTH),
                  ("ga", D_MODEL), ("gb", D_MODEL)):
    _MAIN_COLS[_name] = (_off, _w)
    _off += _w
MAIN_WIDTH = _off
PROJ_TN = 512
_PROJ_KINDS = (["qscale"] + ["kv"] * 2 + ["silu"] + ["silu"] * 2 + ["forget"] * 2
               + ["id"] * 2 + ["silu"] * 2 + ["sigmoid"] * 4)
_F_TILE0 = _PROJ_KINDS.index("forget")
_F_TILES = _PROJ_KINDS.count("forget")
_QKV_TILES = 3

PERM_DIL = 16
PERM_TILE = 512
PERM_SLAB = PERM_TILE // PERM_DIL
ATTN_SUPER = PERM_DIL * ATTN_BLOCK


def _sigmoid(x):
    return 1.0 / (1.0 + jnp.exp(-x))


def _any_eq(j, values):
    return functools.reduce(jnp.logical_or, [j == v for v in values])


def _mod_kernel(c_ref, w_ref, b_ref, o_ref):
    c = c_ref[...]
    sc = c * _sigmoid(c)
    o_ref[...] = jnp.dot(sc, w_ref[...], precision=lax.Precision.HIGHEST,
                         preferred_element_type=F32) + b_ref[...]


def _lower_bound_kernel(hl_ref, o_ref):
    hl = hl_ref[...]
    m = jnp.max(hl, axis=0, keepdims=True)
    e = jnp.exp(hl - m)
    o_ref[...] = e[0:1, :] / jnp.sum(e, axis=0, keepdims=True)


def _bias_table_kernel(rb_ref, bucket_ref, o_ref):
    bk = bucket_ref[...]
    for h in range(ATTN_HEADS):
        acc = jnp.full(bk.shape, NEG_INF, F32)
        for u in range(N_BUCKETS):
            acc = jnp.where(bk == u, rb_ref[u, h], acc)
        o_ref[h] = acc


def _bucket_tables():
    qi = np.arange(ATTN_BLOCK)[:, None]
    kj = np.arange(2 * ATTN_BLOCK)[None, :]
    delta = qi + ATTN_BLOCK - kj
    max_exact = N_BUCKETS // 2
    tabs = []
    for window, dilation in DILATED_PATTERNS:
        span = window // dilation
        band = (delta >= 0) & (delta <= span)
        dist = np.clip(delta, 0, None) * dilation
        n = dist.astype(np.float32)
        large = max_exact + (np.log(np.maximum(n, 1.0) / max_exact)
                             / math.log(MAX_DISTANCE / max_exact)
                             * (N_BUCKETS - max_exact)).astype(np.int32)
        large = np.minimum(large, N_BUCKETS - 1)
        bucket = np.where(dist < max_exact, dist, large)
        tab = np.where(band, bucket, -1).astype(np.int32)
        if dilation > 1:
            order = _gather_order(dilation)
            cols = np.concatenate([order, ATTN_BLOCK + order])
            tab = tab[order][:, cols]
        tabs.append(tab)
    return np.stack(tabs, 0)


def _gather_order(dilation):
    per_tile = PERM_DIL // dilation
    slab = np.arange(ATTN_BLOCK) // PERM_SLAB
    m = np.arange(ATTN_BLOCK) % PERM_SLAB
    if per_tile == 1:
        return slab * PERM_SLAB + m
    assert per_tile * PERM_SLAB == ATTN_BLOCK
    return per_tile * m + slab


def _inproj_kernel(x_ref, mod_ref, g_ref, lb_ref, w_ref, om_ref, of_ref, op_ref, accl_ref):
    x = x_ref[...]
    ms = jnp.mean(x * x, axis=-1, keepdims=True)
    y = x * lax.rsqrt(ms + EPS) * g_ref[...]
    shift = mod_ref[:, 0:D_MODEL]
    scale = mod_ref[:, D_MODEL:2 * D_MODEL]
    h = (y * (1.0 + scale) + shift).astype(BF16)

    def write_perm(j, acc):
        for c in range(PROJ_TN // V7X_LANES):
            accl_ref[c] = acc[:, c * V7X_LANES:(c + 1) * V7X_LANES]
            col0 = j * PROJ_TN + c * V7X_LANES
            for r in range(PERM_DIL):
                rows = accl_ref[c, pl.ds(r, PERM_SLAB, stride=PERM_DIL), :]
                op_ref[r * PERM_SLAB:(r + 1) * PERM_SLAB, col0:col0 + V7X_LANES] = rows.astype(BF16)

    jm = 0
    for j, kind in enumerate(_PROJ_KINDS):
        acc = jnp.dot(h, w_ref[:, j * PROJ_TN:(j + 1) * PROJ_TN], preferred_element_type=F32)
        if kind == "forget":
            jf = j - _F_TILE0
            lb = lb_ref[:, jf * PROJ_TN:(jf + 1) * PROJ_TN]
            of_ref[:, jf * PROJ_TN:(jf + 1) * PROJ_TN] = lb + (1.0 - lb) * _sigmoid(acc)
            continue
        if kind == "qscale":
            acc = acc * (ATTN_HEAD_DIM ** -0.5)
        elif kind == "silu":
            acc = acc * _sigmoid(acc)
        elif kind == "sigmoid":
            acc = _sigmoid(acc)
        om_ref[:, jm * PROJ_TN:(jm + 1) * PROJ_TN] = acc.astype(BF16)
        if j < _QKV_TILES:
            write_perm(j, acc)
        jm += 1


def _attn_kernel(q_ref, kp_ref, kc_ref, vp_ref, vc_ref, bias_ref, o_ref, lse_ref):
    n = pl.program_id(2)
    blk = ATTN_BLOCK
    lane = lax.broadcasted_iota(jnp.int32, (blk, V7X_LANES), 1)
    col = lax.broadcasted_iota(jnp.int32, (blk, 2 * blk), 1)
    no_prev = jnp.logical_and(col < blk, n == 0)
    lse_all = jnp.zeros((blk, V7X_LANES), F32)
    for hp in range(ATTN_HEADS // 2):
        sl = slice(hp * V7X_LANES, (hp + 1) * V7X_LANES)
        qp = q_ref[:, sl]
        kpair = jnp.concatenate([kp_ref[:, sl], kc_ref[:, sl]], axis=0)
        vpair = jnp.concatenate([vp_ref[:, sl], vc_ref[:, sl]], axis=0)
        o_pair, lse_all = _attn_head_pair(qp, kpair, vpair, bias_ref, hp, no_prev, lane, lse_all)
        o_ref[:, sl] = o_pair.astype(BF16)
    lse_ref[...] = lse_all


def _attn_head_pair(q, k, v, bias_ref, hp, no_prev, lane, lse_all):
    outs = []
    for hh in range(2):
        h = 2 * hp + hh
        in_head = (lane >= hh * ATTN_HEAD_DIM) & (lane < (hh + 1) * ATTN_HEAD_DIM)
        qh = jnp.where(in_head, q, jnp.zeros_like(q))
        s = lax.dot_general(qh, k, (((1,), (1,)), ((), ())), preferred_element_type=F32)
        s = s + jnp.where(no_prev, NEG_INF, bias_ref[h])
        m = jnp.max(s, axis=-1, keepdims=True)
        p = jnp.exp(s - m)
        den = jnp.sum(p, axis=-1, keepdims=True)
        o = jnp.dot(p.astype(BF16), v, preferred_element_type=F32)
        outs.append(o / den)
        lse_all = jnp.where(lane == h, m + jnp.log(den), lse_all)
    return jnp.where(lane < ATTN_HEAD_DIM, outs[0], outs[1]), lse_all


def _attn_perm_kernel(q_ref, kp_ref, kc_ref, vp_ref, vc_ref, bias_ref, o_ref, lse_ref,
                      o_scr, *ext, dil):
    n = pl.program_id(1)
    blk_rows = ATTN_BLOCK
    n_slab = ATTN_BLOCK // PERM_SLAB
    lane = lax.broadcasted_iota(jnp.int32, (blk_rows, V7X_LANES), 1)
    col = lax.broadcasted_iota(jnp.int32, (blk_rows, 2 * blk_rows), 1)
    prev_rows = kp_ref.shape[0]
    if ext:
        kext, vext = ext
        kext[0:prev_rows] = kp_ref[...]
        kext[prev_rows:] = kc_ref[...]
        vext[0:prev_rows] = vp_ref[...]
        vext[prev_rows:] = vc_ref[...]

    def body(blk, carry):
        if dil == PERM_DIL:
            offs = [t * PERM_TILE + blk * PERM_SLAB for t in range(n_slab)]
            tok0 = [t * PERM_TILE + blk for t in range(n_slab)]
            first = n == 0
        else:
            tile = lax.shift_right_logical(blk, 2)
            res = jnp.bitwise_and(blk, dil - 1)
            offs = [tile * PERM_TILE + (res + dil * j) * PERM_SLAB for j in range(n_slab)]
            tok0 = [tile * PERM_TILE + res + dil * j for j in range(n_slab)]
            first = jnp.logical_and(n == 0, tile == 0)
        offs = [pl.multiple_of(o, PERM_SLAB) for o in offs]
        no_prev = jnp.logical_and(col < blk_rows, first)
        lse_all = jnp.zeros((blk_rows, V7X_LANES), F32)
        for hp in range(ATTN_HEADS // 2):
            sl = slice(hp * V7X_LANES, (hp + 1) * V7X_LANES)
            q = jnp.concatenate([q_ref[pl.ds(o, PERM_SLAB), sl] for o in offs], axis=0)
            if ext:
                k = jnp.concatenate([kext[pl.ds(o, PERM_SLAB), sl] for o in offs]
                                    + [kext[pl.ds(o + prev_rows, PERM_SLAB), sl] for o in offs], axis=0)
                v = jnp.concatenate([vext[pl.ds(o, PERM_SLAB), sl] for o in offs]
                                    + [vext[pl.ds(o + prev_rows, PERM_SLAB), sl] for o in offs], axis=0)
            else:
                k = jnp.concatenate([kp_ref[pl.ds(o, PERM_SLAB), sl] for o in offs]
                                    + [kc_ref[pl.ds(o, PERM_SLAB), sl] for o in offs], axis=0)
                v = jnp.concatenate([vp_ref[pl.ds(o, PERM_SLAB), sl] for o in offs]
                                    + [vc_ref[pl.ds(o, PERM_SLAB), sl] for o in offs], axis=0)
            o_pair, lse_all = _attn_head_pair(q, k, v, bias_ref, hp, no_prev, lane, lse_all)
            for j, t0 in enumerate(tok0):
                o_scr[hp, pl.ds(t0, PERM_SLAB, stride=PERM_DIL), :] = o_pair[j * PERM_SLAB:(j + 1) * PERM_SLAB]
        for j, t0 in enumerate(tok0):
            lse_ref[pl.ds(t0, PERM_SLAB, stride=PERM_DIL), :] = lse_all[j * PERM_SLAB:(j + 1) * PERM_SLAB]
        return carry

    lax.fori_loop(0, ATTN_SUPER // ATTN_BLOCK, body, 0)
    for hp in range(ATTN_HEADS // 2):
        o_ref[:, hp * V7X_LANES:(hp + 1) * V7X_LANES] = o_scr[hp].astype(BF16)


def _hgrn_kernel(q_ref, f_ref, i_ref, z_ref, g_ref, o_ref, state_ref, *, chunk, n_chunks):
    @pl.when(pl.program_id(1) == 0)
    def _():
        state_ref[...] = jnp.zeros_like(state_ref)

    C = chunk
    n_levels = C.bit_length() - 1
    row = lax.broadcasted_iota(jnp.int32, (C, HGRN_KEY_DIM), 0)
    tt = lax.broadcasted_iota(jnp.int32, (C, C), 0)
    ss = lax.broadcasted_iota(jnp.int32, (C, C), 1)
    txs = jnp.bitwise_xor(tt, ss)
    lower = tt > ss
    g_on = g_ref[...]
    nt = (((1,), (1,)), ((), ()))

    def chunk_body(ci, carry):
        r0 = pl.multiple_of(ci * C, C)
        rows = pl.ds(r0, C)
        for h in range(HGRN_HEADS):
            hs = slice(h * HGRN_KEY_DIM, (h + 1) * HGRN_KEY_DIM)
            f = f_ref[rows, hs]
            q = q_ref[rows, hs].astype(F32)
            k = 1.0 - f
            v = i_ref[rows, hs]
            a = jnp.where(tt == ss,
                          lax.dot_general(q.astype(BF16), k.astype(BF16), nt,
                                          preferred_element_type=F32), 0.0)
            pq = f
            sk = jnp.ones_like(f)
            tot = f
            for lvl in range(n_levels):
                m = 1 << lvl
                odd = jnp.bitwise_and(row, m) != 0
                e = jnp.where(odd, pq, sk)
                pm = lax.dot_general((q * e).astype(BF16), (k * e).astype(BF16), nt,
                                     preferred_element_type=F32)
                mask = lower & (txs >= m) & (txs < 2 * m)
                a = jnp.where(mask, pm, a)
                tot_dn = pltpu.roll(tot, m, 0)
                tot_up = pltpu.roll(tot, C - m, 0)
                pq = jnp.where(odd, pq * tot_dn, pq)
                sk = jnp.where(odd, sk, sk * tot_up)
                tot = tot * jnp.where(odd, tot_dn, tot_up)
            st = state_ref[h]
            o_intra = jnp.dot(a.astype(BF16), v, preferred_element_type=F32)
            o_inter = lax.dot_general((q * pq).astype(BF16), st.astype(BF16), nt,
                                      preferred_element_type=F32)
            o = o_intra + o_inter
            upd = lax.dot_general(v, (k * sk).astype(BF16), (((0,), (0,)), ((), ())),
                                  preferred_element_type=F32)
            state_ref[h] = st * tot[0:1, :] + upd
            ms = jnp.mean(o * o, axis=-1, keepdims=True)
            y = o * lax.rsqrt(ms + EPS) * g_on
            o_ref[rows, hs] = (y * z_ref[rows, hs].astype(F32)).astype(BF16)
        return carry

    lax.fori_loop(0, n_chunks, chunk_body, 0)


def _merge_kernel(o1_ref, o2_ref, o3_ref, l1_ref, l2_ref, l3_ref, za_ref, ob_ref,
                  sga_ref, sgb_ref, x_ref, mod_ref, wa_ref, wb_ref, wo_ref, fg_ref,
                  ex_ref, out_ref):
    l1, l2, l3 = l1_ref[...], l2_ref[...], l3_ref[...]
    mx = jnp.maximum(jnp.maximum(l1, l2), l3)
    e1, e2, e3 = jnp.exp(l1 - mx), jnp.exp(l2 - mx), jnp.exp(l3 - mx)
    inv = 1.0 / (e1 + e2 + e3)
    ex = ex_ref[...]

    def expand(w):
        hi = w.astype(BF16)
        lo = (w - hi.astype(F32)).astype(BF16)
        return (jnp.dot(hi, ex, preferred_element_type=F32)
                + jnp.dot(lo, ex, preferred_element_type=F32))

    oa = (expand(e1 * inv) * o1_ref[...].astype(F32)
          + expand(e2 * inv) * o2_ref[...].astype(F32)
          + expand(e3 * inv) * o3_ref[...].astype(F32))
    oa = (oa * za_ref[...].astype(F32)).astype(BF16)
    ya = jnp.dot(oa, wa_ref[...], preferred_element_type=F32)
    yb = jnp.dot(ob_ref[...], wb_ref[...], preferred_element_type=F32)
    y = sga_ref[...].astype(F32) * ya + sgb_ref[...].astype(F32) * yb
    z = jnp.dot(y.astype(BF16), wo_ref[...], preferred_element_type=F32)
    gate = mod_ref[:, 2 * D_MODEL:3 * D_MODEL]
    xo = x_ref[...] + gate * z
    ms = jnp.mean(xo * xo, axis=-1, keepdims=True)
    out_ref[...] = xo * lax.rsqrt(ms + EPS) * fg_ref[...]


def _cparams(sem, vmem_mb):
    return pltpu.CompilerParams(dimension_semantics=sem,
                                vmem_limit_bytes=vmem_mb * 1024 * 1024)


def kernel(x, c, w_ada, b_ada, norm_g, w_in, hgrn_onorm_g, w_branch_a, w_branch_b, w_out,
           rel_bias, hgrn_lb, final_g):
    B, S, D = x.shape
    assert D == D_MODEL and w_ada.shape[0] == 1, "single-layer kernel"
    N = B * S
    x2 = x.reshape(N, D)

    c8 = jnp.pad(c, ((0, 8 - B), (0, 0)))
    mod = pl.pallas_call(
        _mod_kernel,
        grid=(3 * D // 512,),
        in_specs=[pl.BlockSpec((8, D), lambda j: (0, 0)),
                  pl.BlockSpec((D, 512), lambda j: (0, j)),
                  pl.BlockSpec((1, 512), lambda j: (0, j))],
        out_specs=pl.BlockSpec((8, 512), lambda j: (0, j)),
        out_shape=jax.ShapeDtypeStruct((8, 3 * D), F32),
        name="adaln_mod",
    )(c8, w_ada[0], b_ada[0].reshape(1, 3 * D))
    mod3 = mod.reshape(8, 1, 3 * D)

    lb = pl.pallas_call(
        _lower_bound_kernel,
        out_shape=jax.ShapeDtypeStruct((1, HGRN_WIDTH), F32),
        name="hgrn_lower_bound",
    )(hgrn_lb)

    n_pat = len(DILATED_PATTERNS)
    bias_tab = pl.pallas_call(
        _bias_table_kernel,
        grid=(n_pat,),
        in_specs=[pl.BlockSpec(memory_space=pltpu.SMEM),
                  pl.BlockSpec((None, ATTN_BLOCK, 2 * ATTN_BLOCK), lambda g: (g, 0, 0))],
        out_specs=pl.BlockSpec((None, ATTN_HEADS, ATTN_BLOCK, 2 * ATTN_BLOCK),
                               lambda g: (g, 0, 0, 0)),
        out_shape=jax.ShapeDtypeStruct((n_pat, ATTN_HEADS, ATTN_BLOCK, 2 * ATTN_BLOCK), F32),
        name="rel_bias_table",
    )(rel_bias, jnp.asarray(_bucket_tables()))

    tm = PERM_TILE
    tiles_per_b = S // tm
    in_width = len(_PROJ_KINDS) * PROJ_TN
    assert PROJ_TN == ATTN_WIDTH and w_in.shape[2] == in_width
    resident = dict(pipeline_mode=pl.Buffered(1))
    main, fgate, qkv_p = pl.pallas_call(
        _inproj_kernel,
        grid=(N // tm,),
        in_specs=[pl.BlockSpec((tm, D), lambda i: (i, 0)),
                  pl.BlockSpec((None, 1, 3 * D), lambda i: (i // tiles_per_b, 0, 0)),
                  pl.BlockSpec((1, D), lambda i: (0, 0)),
                  pl.BlockSpec((1, HGRN_WIDTH), lambda i: (0, 0)),
                  pl.BlockSpec((D, in_width), lambda i: (0, 0), **resident)],
        out_specs=[pl.BlockSpec((tm, MAIN_WIDTH), lambda i: (i, 0)),
                   pl.BlockSpec((tm, HGRN_WIDTH), lambda i: (i, 0)),
                   pl.BlockSpec((tm, _QKV_TILES * ATTN_WIDTH), lambda i: (i, 0))],
        out_shape=[jax.ShapeDtypeStruct((N, MAIN_WIDTH), BF16),
                   jax.ShapeDtypeStruct((N, HGRN_WIDTH), F32),
                   jax.ShapeDtypeStruct((N, _QKV_TILES * ATTN_WIDTH), BF16)],
        scratch_shapes=[pltpu.VMEM((PROJ_TN // V7X_LANES, tm, V7X_LANES), F32)],
        compiler_params=_cparams(("arbitrary",), 56),
        name="inproj",
    )(x2, mod3, norm_g[0].reshape(1, D), lb, w_in[0].astype(BF16))

    qa_t = _MAIN_COLS["qa"][0] // ATTN_WIDTH
    ka_t = _MAIN_COLS["ka"][0] // ATTN_WIDTH
    va_t = _MAIN_COLS["va"][0] // ATTN_WIDTH
    main_tiles = MAIN_WIDTH // ATTN_WIDTH
    attn_outs = []
    qkv_pv = qkv_p.reshape(B, S, _QKV_TILES * ATTN_WIDTH)
    n_super = S // ATTN_SUPER
    for g, (window, dil) in enumerate(DILATED_PATTERNS):
        assert window // dil == ATTN_BLOCK
        if dil > 1:
            prev_rows = ATTN_SUPER if dil == PERM_DIL else PERM_TILE
            per_step = ATTN_SUPER // prev_rows

            def cur_spec(t):
                return pl.BlockSpec((None, ATTN_SUPER, ATTN_WIDTH), lambda b, n, t=t: (b, n, t))

            def prev_spec(t):
                return pl.BlockSpec((None, prev_rows, ATTN_WIDTH),
                                    lambda b, n, t=t: (b, jnp.maximum(n * per_step - 1, 0), t))

            scratch = [pltpu.VMEM((ATTN_WIDTH // V7X_LANES, ATTN_SUPER, V7X_LANES), F32)]
            if dil != PERM_DIL:
                scratch += [pltpu.VMEM((prev_rows + ATTN_SUPER, ATTN_WIDTH), BF16)] * 2
            o_g, lse_g = pl.pallas_call(
                functools.partial(_attn_perm_kernel, dil=dil),
                grid=(B, n_super),
                in_specs=[cur_spec(0), prev_spec(1), cur_spec(1), prev_spec(2), cur_spec(2),
                          pl.BlockSpec((None, ATTN_HEADS, ATTN_BLOCK, 2 * ATTN_BLOCK),
                                       lambda b, n, g=g: (g, 0, 0, 0))],
                out_specs=[pl.BlockSpec((None, ATTN_SUPER, ATTN_WIDTH), lambda b, n: (b, n, 0)),
                           pl.BlockSpec((None, ATTN_SUPER, V7X_LANES), lambda b, n: (b, n, 0))],
                out_shape=[jax.ShapeDtypeStruct((B, S, ATTN_WIDTH), BF16),
                           jax.ShapeDtypeStruct((B, S, V7X_LANES), F32)],
                scratch_shapes=scratch,
                compiler_params=_cparams(("arbitrary", "arbitrary"), 48),
                name=f"dilated_attn_d{dil}",
            )(qkv_pv, qkv_pv, qkv_pv, qkv_pv, qkv_pv, bias_tab)
            attn_outs.append((o_g.reshape(N, ATTN_WIDTH), lse_g.reshape(N, V7X_LANES)))
            continue
        L = S // dil
        nb = L // ATTN_BLOCK
        main_v = main.reshape(B, L, dil * MAIN_WIDTH)

        def in_spec(col_tile, prev):
            def imap(b, r, n, col_tile=col_tile, prev=prev):
                nn = jnp.maximum(n - 1, 0) if prev else n
                return (b, nn, r * main_tiles + col_tile)
            return pl.BlockSpec((None, ATTN_BLOCK, ATTN_WIDTH), imap)

        o_g, lse_g = pl.pallas_call(
            _attn_kernel,
            grid=(B, dil, nb),
            in_specs=[in_spec(qa_t, False), in_spec(ka_t, True), in_spec(ka_t, False),
                      in_spec(va_t, True), in_spec(va_t, False),
                      pl.BlockSpec((None, ATTN_HEADS, ATTN_BLOCK, 2 * ATTN_BLOCK),
                                   lambda b, r, n, g=g: (g, 0, 0, 0))],
            out_specs=[pl.BlockSpec((None, ATTN_BLOCK, ATTN_WIDTH), lambda b, r, n: (b, n, r)),
                       pl.BlockSpec((None, ATTN_BLOCK, V7X_LANES), lambda b, r, n: (b, n, r))],
            out_shape=[jax.ShapeDtypeStruct((B, L, dil * ATTN_WIDTH), BF16),
                       jax.ShapeDtypeStruct((B, L, dil * V7X_LANES), F32)],
            compiler_params=_cparams(("arbitrary", "arbitrary", "arbitrary"), 32),
            name=f"dilated_attn_d{dil}",
        )(main_v, main_v, main_v, main_v, main_v, bias_tab)
        attn_outs.append((o_g.reshape(N, ATTN_WIDTH), lse_g.reshape(N, V7X_LANES)))

    th = 512
    chunk = 64
    hw_t = HGRN_WIDTH
    qb_t = _MAIN_COLS["qb"][0] // hw_t
    ib_t = _MAIN_COLS["ib"][0] // hw_t
    zb_t = _MAIN_COLS["zb"][0] // hw_t
    main_b = main.reshape(B, S, MAIN_WIDTH)
    ob = pl.pallas_call(
        functools.partial(_hgrn_kernel, chunk=chunk, n_chunks=th // chunk),
        grid=(B, S // th),
        in_specs=[pl.BlockSpec((None, th, hw_t), lambda b, s: (b, s, qb_t)),
                  pl.BlockSpec((None, th, hw_t), lambda b, s: (b, s, 0)),
                  pl.BlockSpec((None, th, hw_t), lambda b, s: (b, s, ib_t)),
                  pl.BlockSpec((None, th, hw_t), lambda b, s: (b, s, zb_t)),
                  pl.BlockSpec((1, HGRN_VAL_DIM), lambda b, s: (0, 0))],
        out_specs=pl.BlockSpec((None, th, hw_t), lambda b, s: (b, s, 0)),
        out_shape=jax.ShapeDtypeStruct((B, S, hw_t), BF16),
        scratch_shapes=[pltpu.VMEM((HGRN_HEADS, HGRN_VAL_DIM, HGRN_KEY_DIM), F32)],
        compiler_params=_cparams(("arbitrary", "arbitrary"), 32),
        name="hgrn2",
    )(main_b, fgate.reshape(B, S, hw_t), main_b, main_b, hgrn_onorm_g[0].reshape(1, HGRN_VAL_DIM))
    ob = ob.reshape(N, hw_t)

    tk = 256
    tiles_per_b5 = S // tk
    za_t = _MAIN_COLS["za"][0] // ATTN_WIDTH
    ga_t = _MAIN_COLS["ga"][0] // D
    gb_t = _MAIN_COLS["gb"][0] // D
    expand_mat = np.zeros((V7X_LANES, ATTN_WIDTH), np.float32)
    for h in range(ATTN_HEADS):
        expand_mat[h, h * ATTN_HEAD_DIM:(h + 1) * ATTN_HEAD_DIM] = 1.0
    (o1, l1), (o2, l2), (o3, l3) = attn_outs
    row_spec = lambda w, t=0: pl.BlockSpec((tk, w), lambda i, t=t: (i, t))
    full_spec = lambda a, b: pl.BlockSpec((a, b), lambda i: (0, 0))
    out = pl.pallas_call(
        _merge_kernel,
        grid=(N // tk,),
        in_specs=[row_spec(ATTN_WIDTH), row_spec(ATTN_WIDTH), row_spec(ATTN_WIDTH),
                  row_spec(V7X_LANES), row_spec(V7X_LANES), row_spec(V7X_LANES),
                  row_spec(ATTN_WIDTH, za_t), row_spec(HGRN_WIDTH),
                  row_spec(D, ga_t), row_spec(D, gb_t), row_spec(D),
                  pl.BlockSpec((None, 1, 3 * D), lambda i: (i // tiles_per_b5, 0, 0)),
                  full_spec(ATTN_WIDTH, D), full_spec(HGRN_WIDTH, D), full_spec(D, D),
                  full_spec(1, D), full_spec(V7X_LANES, ATTN_WIDTH)],
        out_specs=pl.BlockSpec((tk, D), lambda i: (i, 0)),
        out_shape=jax.ShapeDtypeStruct((N, D), F32),
        compiler_params=_cparams(("arbitrary",), 40),
        name="gated_merge",
    )(o1, o2, o3, l1, l2, l3, main, ob, main, main, x2, mod3,
      w_branch_a[0].astype(BF16), w_branch_b[0].astype(BF16), w_out[0].astype(BF16),
      final_g.reshape(1, D), jnp.asarray(expand_mat, BF16))
    return out.reshape(B, S, D)
```

```python
import functools
import math

import numpy as np
import jax
import jax.numpy as jnp
from jax import lax
from jax.experimental import pallas as pl
from jax.experimental.pallas import tpu as pltpu

D_MODEL = 1024
ATTN_HEADS = 8
ATTN_HEAD_DIM = 64
ATTN_WIDTH = ATTN_HEADS * ATTN_HEAD_DIM
DILATED_PATTERNS = ((128, 1), (512, 4), (2048, 16))
ATTN_BLOCK = 128
N_BUCKETS = 32
MAX_DISTANCE = 2048
NEG_INF = -1e30
HGRN_HEADS = 8
HGRN_KEY_DIM = 128
HGRN_VAL_DIM = 128
HGRN_WIDTH = HGRN_HEADS * HGRN_VAL_DIM
EPS = 1e-6

V7X_LANES = 128
V7X_SUBLANES = 8

F32 = jnp.float32
BF16 = jnp.bfloat16

_MAIN_COLS = {}
_off = 0
for _name, _w in (("qa", ATTN_WIDTH), ("ka", ATTN_WIDTH), ("va", ATTN_WIDTH), ("za", ATTN_WIDTH),
                  ("qb", HGRN_WIDTH), ("ib", HGRN_WIDTH), ("zb", HGRN_WIDTH),
                  ("ga", D_MODEL), ("gb", D_MODEL)):
    _MAIN_COLS[_name] = (_off, _w)
    _off += _w
MAIN_WIDTH = _off
PROJ_TN = 512
_PROJ_KINDS = (["qscale"] + ["kv"] * 2 + ["silu"] + ["silu"] * 2 + ["forget"] * 2
               + ["id"] * 2 + ["silu"] * 2 + ["sigmoid"] * 4)
_F_TILE0 = _PROJ_KINDS.index("forget")
_F_TILES = _PROJ_KINDS.count("forget")
_QKV_TILES = 3

PERM_DIL = 16
PERM_TILE = 512
PERM_SLAB = PERM_TILE // PERM_DIL
ATTN_SUPER = PERM_DIL * ATTN_BLOCK
ATTN_SKEW = 3


def _sigmoid(x):
    return 1.0 / (1.0 + jnp.exp(-x))


def _any_eq(j, values):
    return functools.reduce(jnp.logical_or, [j == v for v in values])


def _mod_kernel(c_ref, w_ref, b_ref, o_ref):
    c = c_ref[...]
    sc = c * _sigmoid(c)
    o_ref[...] = jnp.dot(sc, w_ref[...], precision=lax.Precision.HIGHEST,
                         preferred_element_type=F32) + b_ref[...]


def _lower_bound_kernel(hl_ref, o_ref):
    hl = hl_ref[...]
    m = jnp.max(hl, axis=0, keepdims=True)
    e = jnp.exp(hl - m)
    o_ref[...] = e[0:1, :] / jnp.sum(e, axis=0, keepdims=True)


def _bias_table_kernel(rb_ref, bucket_ref, o_ref):
    bk = bucket_ref[...]
    for h in range(ATTN_HEADS):
        acc = jnp.full(bk.shape, NEG_INF, F32)
        for u in range(N_BUCKETS):
            acc = jnp.where(bk == u, rb_ref[u, h], acc)
        o_ref[h] = acc


def _bucket_tables():
    qi = np.arange(ATTN_BLOCK)[:, None]
    kj = np.arange(2 * ATTN_BLOCK)[None, :]
    delta = qi + ATTN_BLOCK - kj
    max_exact = N_BUCKETS // 2
    tabs = []
    for window, dilation in DILATED_PATTERNS:
        span = window // dilation
        band = (delta >= 0) & (delta <= span)
        dist = np.clip(delta, 0, None) * dilation
        n = dist.astype(np.float32)
        large = max_exact + (np.log(np.maximum(n, 1.0) / max_exact)
                             / math.log(MAX_DISTANCE / max_exact)
                             * (N_BUCKETS - max_exact)).astype(np.int32)
        large = np.minimum(large, N_BUCKETS - 1)
        bucket = np.where(dist < max_exact, dist, large)
        tab = np.where(band, bucket, -1).astype(np.int32)
        if dilation > 1:
            order = _gather_order(dilation)
            cols = np.concatenate([order, ATTN_BLOCK + order])
            tab = tab[order][:, cols]
        tabs.append(tab)
    return np.stack(tabs, 0)


def _gather_order(dilation):
    per_tile = PERM_DIL // dilation
    slab = np.arange(ATTN_BLOCK) // PERM_SLAB
    m = np.arange(ATTN_BLOCK) % PERM_SLAB
    if per_tile == 1:
        return slab * PERM_SLAB + m
    assert per_tile * PERM_SLAB == ATTN_BLOCK
    return per_tile * m + slab


def _inproj_kernel(x_ref, mod_ref, g_ref, lb_ref, w_ref, om_ref, of_ref, op_ref, accl_ref):
    x = x_ref[...]
    ms = jnp.mean(x * x, axis=-1, keepdims=True)
    y = x * lax.rsqrt(ms + EPS) * g_ref[...]
    shift = mod_ref[:, 0:D_MODEL]
    scale = mod_ref[:, D_MODEL:2 * D_MODEL]
    h = (y * (1.0 + scale) + shift).astype(BF16)

    def write_perm(j, acc):
        for c in range(PROJ_TN // V7X_LANES):
            accl_ref[c] = acc[:, c * V7X_LANES:(c + 1) * V7X_LANES]
            col0 = j * PROJ_TN + c * V7X_LANES
            for r in range(PERM_DIL):
                rows = accl_ref[c, pl.ds(r, PERM_SLAB, stride=PERM_DIL), :]
                op_ref[r * PERM_SLAB:(r + 1) * PERM_SLAB, col0:col0 + V7X_LANES] = rows.astype(BF16)

    jm = 0
    for j, kind in enumerate(_PROJ_KINDS):
        acc = jnp.dot(h, w_ref[:, j * PROJ_TN:(j + 1) * PROJ_TN], preferred_element_type=F32)
        if kind == "forget":
            jf = j - _F_TILE0
            lb = lb_ref[:, jf * PROJ_TN:(jf + 1) * PROJ_TN]
            of_ref[:, jf * PROJ_TN:(jf + 1) * PROJ_TN] = lb + (1.0 - lb) * _sigmoid(acc)
            continue
        if kind == "qscale":
            acc = acc * (ATTN_HEAD_DIM ** -0.5)
        elif kind == "silu":
            acc = acc * _sigmoid(acc)
        elif kind == "sigmoid":
            acc = _sigmoid(acc)
        om_ref[:, jm * PROJ_TN:(jm + 1) * PROJ_TN] = acc.astype(BF16)
        if j < _QKV_TILES:
            write_perm(j, acc)
        jm += 1


def _attn_kernel(q_ref, kp_ref, kc_ref, vp_ref, vc_ref, bias_ref, o_ref, lse_ref):
    n = pl.program_id(2)
    blk = ATTN_BLOCK
    lane = lax.broadcasted_iota(jnp.int32, (blk, V7X_LANES), 1)
    col = lax.broadcasted_iota(jnp.int32, (blk, 2 * blk), 1)
    no_prev = jnp.logical_and(col < blk, n == 0)

    def put_o(hp, o_pair):
        o_ref[:, _pair_lanes(hp)] = o_pair.astype(BF16)

    lse_ref[...] = _attn_block(
        lambda hp: q_ref[:, _pair_lanes(hp)],
        lambda hp: jnp.concatenate([kp_ref[:, _pair_lanes(hp)], kc_ref[:, _pair_lanes(hp)]], axis=0),
        lambda hp: jnp.concatenate([vp_ref[:, _pair_lanes(hp)], vc_ref[:, _pair_lanes(hp)]], axis=0),
        put_o, bias_ref, no_prev, lane)


def _pair_lanes(hp):
    return slice(hp * V7X_LANES, (hp + 1) * V7X_LANES)


def _attn_scores(q, k, bias_ref, hp, no_prev, lane):
    blk = q.shape[0]
    low = lane < ATTN_HEAD_DIM
    zero = jnp.zeros_like(q)
    q2 = jnp.concatenate([jnp.where(low, q, zero), jnp.where(low, zero, q)], axis=0)
    s = lax.dot_general(q2, k, (((1,), (1,)), ((), ())), preferred_element_type=F32)
    bias2 = bias_ref[pl.ds(2 * hp, 2)].reshape(2 * blk, 2 * blk)
    s = s + jnp.where(jnp.concatenate([no_prev, no_prev], axis=0), NEG_INF, bias2)
    return s, jnp.max(s, axis=-1, keepdims=True)


def _attn_values(s, m, v, hp, lane, lse_all):
    blk = s.shape[0] // 2
    p = jnp.exp(s - m).astype(BF16)
    h0, h1 = 2 * hp, 2 * hp + 1
    low = lane < ATTN_HEAD_DIM
    low_v = lax.broadcasted_iota(jnp.int32, v.shape, 1) < ATTN_HEAD_DIM
    one = jnp.ones_like(v)
    o0 = jnp.dot(p[:blk], jnp.where(low_v, v, one), preferred_element_type=F32)
    o1 = jnp.dot(p[blk:], jnp.where(low_v, one, v), preferred_element_type=F32)
    num = jnp.where(low, o0, o1)
    den_swapped = jnp.where(low, o1, o0)
    den = pltpu.roll(den_swapped, ATTN_HEAD_DIM, 1)
    is_h1 = lane == h1
    lse = jnp.where(is_h1, m[blk:], m[:blk]) + jnp.log(jnp.where(is_h1, den_swapped, den))
    lse_all = jnp.where(jnp.logical_or(lane == h0, is_h1), lse, lse_all)
    return num / den, lse_all


def _attn_block(get_q, get_k, get_v, put_o, bias_ref, no_prev, lane):
    lse_all = jnp.zeros(lane.shape, F32)
    n_pairs = ATTN_HEADS // 2
    pending = {}
    for step in range(n_pairs + ATTN_SKEW):
        if step < n_pairs:
            pending[step] = _attn_scores(get_q(step), get_k(step), bias_ref, step, no_prev, lane)
        hp = step - ATTN_SKEW
        if hp >= 0:
            o_pair, lse_all = _attn_values(*pending.pop(hp), get_v(hp), hp, lane, lse_all)
            put_o(hp, o_pair)
    return lse_all


def _attn_perm_kernel(q_ref, kp_ref, kc_ref, vp_ref, vc_ref, bias_ref, o_ref, lse_ref,
                      o_scr, *ext, dil):
    n = pl.program_id(1)
    blk_rows = ATTN_BLOCK
    n_slab = ATTN_BLOCK // PERM_SLAB
    lane = lax.broadcasted_iota(jnp.int32, (blk_rows, V7X_LANES), 1)
    col = lax.broadcasted_iota(jnp.int32, (blk_rows, 2 * blk_rows), 1)
    prev_rows = kp_ref.shape[0]
    if ext:
        kext, vext = ext
        kext[0:prev_rows] = kp_ref[...]
        kext[prev_rows:] = kc_ref[...]
        vext[0:prev_rows] = vp_ref[...]
        vext[prev_rows:] = vc_ref[...]
        k_prev, k_cur, v_prev, v_cur, cur_off = kext, kext, vext, vext, prev_rows
    else:
        k_prev, k_cur, v_prev, v_cur, cur_off = kp_ref, kc_ref, vp_ref, vc_ref, 0

    def body(blk, carry):
        if dil == PERM_DIL:
            offs = [t * PERM_TILE + blk * PERM_SLAB for t in range(n_slab)]
            tok0 = [t * PERM_TILE + blk for t in range(n_slab)]
            first = n == 0
        else:
            tile = lax.shift_right_logical(blk, 2)
            res = jnp.bitwise_and(blk, dil - 1)
            offs = [tile * PERM_TILE + (res + dil * j) * PERM_SLAB for j in range(n_slab)]
            tok0 = [tile * PERM_TILE + res + dil * j for j in range(n_slab)]
            first = jnp.logical_and(n == 0, tile == 0)
        offs = [pl.multiple_of(o, PERM_SLAB) for o in offs]
        no_prev = jnp.logical_and(col < blk_rows, first)

        def slabs(ref, hp, shift=0):
            return [ref[pl.ds(o + shift, PERM_SLAB), _pair_lanes(hp)] for o in offs]

        def put_o(hp, o_pair):
            for j, t0 in enumerate(tok0):
                o_scr[hp, pl.ds(t0, PERM_SLAB, stride=PERM_DIL), :] = o_pair[j * PERM_SLAB:(j + 1) * PERM_SLAB]

        lse_all = _attn_block(
            lambda hp: jnp.concatenate(slabs(q_ref, hp), axis=0),
            lambda hp: jnp.concatenate(slabs(k_prev, hp) + slabs(k_cur, hp, cur_off), axis=0),
            lambda hp: jnp.concatenate(slabs(v_prev, hp) + slabs(v_cur, hp, cur_off), axis=0),
            put_o, bias_ref, no_prev, lane)
        for j, t0 in enumerate(tok0):
            lse_ref[pl.ds(t0, PERM_SLAB, stride=PERM_DIL), :] = lse_all[j * PERM_SLAB:(j + 1) * PERM_SLAB]
        return carry

    lax.fori_loop(0, ATTN_SUPER // ATTN_BLOCK, body, 0)
    for hp in range(ATTN_HEADS // 2):
        o_ref[:, hp * V7X_LANES:(hp + 1) * V7X_LANES] = o_scr[hp].astype(BF16)


def _roll_rows(a, shift):
    rows = a.shape[0]
    if abs(shift) >= V7X_SUBLANES:
        return pltpu.roll(a, shift % rows, 0)
    grouped = a.reshape(rows // V7X_SUBLANES, V7X_SUBLANES, a.shape[1])
    return pltpu.roll(grouped, shift % V7X_SUBLANES, 1).reshape(a.shape)


def _hgrn_kernel(q_ref, f_ref, i_ref, z_ref, g_ref, o_ref, state_ref, *, chunk, n_chunks):
    @pl.when(pl.program_id(1) == 0)
    def _():
        state_ref[...] = jnp.zeros_like(state_ref)

    C = chunk
    n_levels = C.bit_length() - 1
    g_on = g_ref[...]
    nt = (((1,), (1,)), ((), ()))

    def chunk_body(ci, carry):
        r0 = pl.multiple_of(ci * C, C)
        rows = pl.ds(r0, C)
        row = lax.broadcasted_iota(jnp.int32, (C, HGRN_KEY_DIM), 0)
        odds = [jnp.bitwise_and(row, 1 << lvl) != 0 for lvl in range(n_levels)]
        tt = lax.broadcasted_iota(jnp.int32, (C, C), 0)
        ss = lax.broadcasted_iota(jnp.int32, (C, C), 1)
        owner = jnp.where(tt > ss, 32 - lax.clz(jnp.bitwise_xor(tt, ss)),
                          jnp.where(tt == ss, 0, -1))
        owned = [owner == lvl for lvl in range(n_levels + 1)]

        def scan(h):
            hs = slice(h * HGRN_KEY_DIM, (h + 1) * HGRN_KEY_DIM)
            f = f_ref[rows, hs]
            q = q_ref[rows, hs]
            k = (1.0 - f).astype(BF16)
            a = jnp.where(owned[0], lax.dot_general(q, k, nt, preferred_element_type=F32), 0.0)
            x = jnp.where(odds[0], f, 1.0)
            y = jnp.where(odds[0], 1.0, f)
            for lvl in range(n_levels):
                m = 1 << lvl
                e = x.astype(BF16)
                pm = lax.dot_general(q * e, k * e, nt, preferred_element_type=F32)
                a = jnp.where(owned[lvl + 1], pm, a)
                tot = x * y if lvl else f
                if 2 * m == V7X_SUBLANES:
                    partner = _roll_rows(tot, m)
                else:
                    partner = jnp.where(odds[lvl], _roll_rows(tot, m), _roll_rows(tot, -m))
                z = x * partner
                keep = odds[lvl] == odds[lvl + 1] if lvl + 1 < n_levels else jnp.logical_not(odds[lvl])
                x, y = jnp.where(keep, z, y), jnp.where(keep, y, z)
            return a.astype(BF16), q * y.astype(BF16), k * x.astype(BF16), x[0:1, :] * y[0:1, :]

        def finish(h, a, q_dec, k_dec, decay):
            hs = slice(h * HGRN_KEY_DIM, (h + 1) * HGRN_KEY_DIM)
            v = i_ref[rows, hs]
            st = state_ref[h]
            o = (jnp.dot(a, v, preferred_element_type=F32)
                 + lax.dot_general(q_dec, st.astype(BF16), nt, preferred_element_type=F32))
            upd = lax.dot_general(v, k_dec, (((0,), (0,)), ((), ())), preferred_element_type=F32)
            state_ref[h] = st * decay + upd
            ms = jnp.mean(o * o, axis=-1, keepdims=True)
            y = o * lax.rsqrt(ms + EPS) * g_on
            o_ref[rows, hs] = (y * z_ref[rows, hs].astype(F32)).astype(BF16)

        pending = None
        for h in range(HGRN_HEADS):
            cur = scan(h)
            if pending is not None:
                finish(h - 1, *pending)
            pending = cur
        finish(HGRN_HEADS - 1, *pending)
        return carry

    lax.fori_loop(0, n_chunks, chunk_body, 0)


def _merge_kernel(o1_ref, o2_ref, o3_ref, l1_ref, l2_ref, l3_ref, za_ref, ob_ref,
                  sga_ref, sgb_ref, x_ref, mod_ref, wa_ref, wb_ref, wo_ref, fg_ref,
                  ex_ref, out_ref):
    l1, l2, l3 = l1_ref[...], l2_ref[...], l3_ref[...]
    mx = jnp.maximum(jnp.maximum(l1, l2), l3)
    e1, e2, e3 = jnp.exp(l1 - mx), jnp.exp(l2 - mx), jnp.exp(l3 - mx)
    inv = 1.0 / (e1 + e2 + e3)
    ex = ex_ref[...]

    def expand(w):
        hi = w.astype(BF16)
        lo = (w - hi.astype(F32)).astype(BF16)
        return (jnp.dot(hi, ex, preferred_element_type=F32)
                + jnp.dot(lo, ex, preferred_element_type=F32))

    oa = (expand(e1 * inv) * o1_ref[...].astype(F32)
          + expand(e2 * inv) * o2_ref[...].astype(F32)
          + expand(e3 * inv) * o3_ref[...].astype(F32))
    oa = (oa * za_ref[...].astype(F32)).astype(BF16)
    ya = jnp.dot(oa, wa_ref[...], preferred_element_type=F32)
    yb = jnp.dot(ob_ref[...], wb_ref[...], preferred_element_type=F32)
    y = sga_ref[...].astype(F32) * ya + sgb_ref[...].astype(F32) * yb
    z = jnp.dot(y.astype(BF16), wo_ref[...], preferred_element_type=F32)
    gate = mod_ref[:, 2 * D_MODEL:3 * D_MODEL]
    xo = x_ref[...] + gate * z
    ms = jnp.mean(xo * xo, axis=-1, keepdims=True)
    out_ref[...] = xo * lax.rsqrt(ms + EPS) * fg_ref[...]


def _cparams(sem, vmem_mb):
    return pltpu.CompilerParams(dimension_semantics=sem,
                                vmem_limit_bytes=vmem_mb * 1024 * 1024)


def kernel(x, c, w_ada, b_ada, norm_g, w_in, hgrn_onorm_g, w_branch_a, w_branch_b, w_out,
           rel_bias, hgrn_lb, final_g):
    B, S, D = x.shape
    assert D == D_MODEL and w_ada.shape[0] == 1, "single-layer kernel"
    N = B * S
    x2 = x.reshape(N, D)

    c8 = jnp.pad(c, ((0, 8 - B), (0, 0)))
    mod = pl.pallas_call(
        _mod_kernel,
        grid=(3 * D // 512,),
        in_specs=[pl.BlockSpec((8, D), lambda j: (0, 0)),
                  pl.BlockSpec((D, 512), lambda j: (0, j)),
                  pl.BlockSpec((1, 512), lambda j: (0, j))],
        out_specs=pl.BlockSpec((8, 512), lambda j: (0, j)),
        out_shape=jax.ShapeDtypeStruct((8, 3 * D), F32),
        name="adaln_mod",
    )(c8, w_ada[0], b_ada[0].reshape(1, 3 * D))
    mod3 = mod.reshape(8, 1, 3 * D)

    lb = pl.pallas_call(
        _lower_bound_kernel,
        out_shape=jax.ShapeDtypeStruct((1, HGRN_WIDTH), F32),
        name="hgrn_lower_bound",
    )(hgrn_lb)

    n_pat = len(DILATED_PATTERNS)
    bias_tab = pl.pallas_call(
        _bias_table_kernel,
        grid=(n_pat,),
        in_specs=[pl.BlockSpec(memory_space=pltpu.SMEM),
                  pl.BlockSpec((None, ATTN_BLOCK, 2 * ATTN_BLOCK), lambda g: (g, 0, 0))],
        out_specs=pl.BlockSpec((None, ATTN_HEADS, ATTN_BLOCK, 2 * ATTN_BLOCK),
                               lambda g: (g, 0, 0, 0)),
        out_shape=jax.ShapeDtypeStruct((n_pat, ATTN_HEADS, ATTN_BLOCK, 2 * ATTN_BLOCK), F32),
        name="rel_bias_table",
    )(rel_bias, jnp.asarray(_bucket_tables()))

    tm = PERM_TILE
    tiles_per_b = S // tm
    in_width = len(_PROJ_KINDS) * PROJ_TN
    assert PROJ_TN == ATTN_WIDTH and w_in.shape[2] == in_width
    resident = dict(pipeline_mode=pl.Buffered(1))
    main, fgate, qkv_p = pl.pallas_call(
        _inproj_kernel,
        grid=(N // tm,),
        in_specs=[pl.BlockSpec((tm, D), lambda i: (i, 0)),
                  pl.BlockSpec((None, 1, 3 * D), lambda i: (i // tiles_per_b, 0, 0)),
                  pl.BlockSpec((1, D), lambda i: (0, 0)),
                  pl.BlockSpec((1, HGRN_WIDTH), lambda i: (0, 0)),
                  pl.BlockSpec((D, in_width), lambda i: (0, 0), **resident)],
        out_specs=[pl.BlockSpec((tm, MAIN_WIDTH), lambda i: (i, 0)),
                   pl.BlockSpec((tm, HGRN_WIDTH), lambda i: (i, 0)),
                   pl.BlockSpec((tm, _QKV_TILES * ATTN_WIDTH), lambda i: (i, 0))],
        out_shape=[jax.ShapeDtypeStruct((N, MAIN_WIDTH), BF16),
                   jax.ShapeDtypeStruct((N, HGRN_WIDTH), F32),
                   jax.ShapeDtypeStruct((N, _QKV_TILES * ATTN_WIDTH), BF16)],
        scratch_shapes=[pltpu.VMEM((PROJ_TN // V7X_LANES, tm, V7X_LANES), F32)],
        compiler_params=_cparams(("arbitrary",), 56),
        name="inproj",
    )(x2, mod3, norm_g[0].reshape(1, D), lb, w_in[0].astype(BF16))

    qa_t = _MAIN_COLS["qa"][0] // ATTN_WIDTH
    ka_t = _MAIN_COLS["ka"][0] // ATTN_WIDTH
    va_t = _MAIN_COLS["va"][0] // ATTN_WIDTH
    main_tiles = MAIN_WIDTH // ATTN_WIDTH
    attn_outs = []
    qkv_pv = qkv_p.reshape(B, S, _QKV_TILES * ATTN_WIDTH)
    n_super = S // ATTN_SUPER
    for g, (window, dil) in enumerate(DILATED_PATTERNS):
        assert window // dil == ATTN_BLOCK
        if dil > 1:
            prev_rows = ATTN_SUPER if dil == PERM_DIL else PERM_TILE
            per_step = ATTN_SUPER // prev_rows

            def cur_spec(t):
                return pl.BlockSpec((None, ATTN_SUPER, ATTN_WIDTH), lambda b, n, t=t: (b, n, t))

            def prev_spec(t):
                return pl.BlockSpec((None, prev_rows, ATTN_WIDTH),
                                    lambda b, n, t=t: (b, jnp.maximum(n * per_step - 1, 0), t))

            scratch = [pltpu.VMEM((ATTN_WIDTH // V7X_LANES, ATTN_SUPER, V7X_LANES), F32)]
            if dil != PERM_DIL:
                scratch += [pltpu.VMEM((prev_rows + ATTN_SUPER, ATTN_WIDTH), BF16)] * 2
            o_g, lse_g = pl.pallas_call(
                functools.partial(_attn_perm_kernel, dil=dil),
                grid=(B, n_super),
                in_specs=[cur_spec(0), prev_spec(1), cur_spec(1), prev_spec(2), cur_spec(2),
                          pl.BlockSpec((None, ATTN_HEADS, ATTN_BLOCK, 2 * ATTN_BLOCK),
                                       lambda b, n, g=g: (g, 0, 0, 0))],
                out_specs=[pl.BlockSpec((None, ATTN_SUPER, ATTN_WIDTH), lambda b, n: (b, n, 0)),
                           pl.BlockSpec((None, ATTN_SUPER, V7X_LANES), lambda b, n: (b, n, 0))],
                out_shape=[jax.ShapeDtypeStruct((B, S, ATTN_WIDTH), BF16),
                           jax.ShapeDtypeStruct((B, S, V7X_LANES), F32)],
                scratch_shapes=scratch,
                compiler_params=_cparams(("arbitrary", "arbitrary"), 48),
                name=f"dilated_attn_d{dil}",
            )(qkv_pv, qkv_pv, qkv_pv, qkv_pv, qkv_pv, bias_tab)
            attn_outs.append((o_g.reshape(N, ATTN_WIDTH), lse_g.reshape(N, V7X_LANES)))
            continue
        L = S // dil
        nb = L // ATTN_BLOCK
        main_v = main.reshape(B, L, dil * MAIN_WIDTH)

        def in_spec(col_tile, prev):
            def imap(b, r, n, col_tile=col_tile, prev=prev):
                nn = jnp.maximum(n - 1, 0) if prev else n
                return (b, nn, r * main_tiles + col_tile)
            return pl.BlockSpec((None, ATTN_BLOCK, ATTN_WIDTH), imap)

        o_g, lse_g = pl.pallas_call(
            _attn_kernel,
            grid=(B, dil, nb),
            in_specs=[in_spec(qa_t, False), in_spec(ka_t, True), in_spec(ka_t, False),
                      in_spec(va_t, True), in_spec(va_t, False),
                      pl.BlockSpec((None, ATTN_HEADS, ATTN_BLOCK, 2 * ATTN_BLOCK),
                                   lambda b, r, n, g=g: (g, 0, 0, 0))],
            out_specs=[pl.BlockSpec((None, ATTN_BLOCK, ATTN_WIDTH), lambda b, r, n: (b, n, r)),
                       pl.BlockSpec((None, ATTN_BLOCK, V7X_LANES), lambda b, r, n: (b, n, r))],
            out_shape=[jax.ShapeDtypeStruct((B, L, dil * ATTN_WIDTH), BF16),
                       jax.ShapeDtypeStruct((B, L, dil * V7X_LANES), F32)],
            compiler_params=_cparams(("arbitrary", "arbitrary", "arbitrary"), 32),
            name=f"dilated_attn_d{dil}",
        )(main_v, main_v, main_v, main_v, main_v, bias_tab)
        attn_outs.append((o_g.reshape(N, ATTN_WIDTH), lse_g.reshape(N, V7X_LANES)))

    th = 512
    chunk = 64
    hw_t = HGRN_WIDTH
    qb_t = _MAIN_COLS["qb"][0] // hw_t
    ib_t = _MAIN_COLS["ib"][0] // hw_t
    zb_t = _MAIN_COLS["zb"][0] // hw_t
    main_b = main.reshape(B, S, MAIN_WIDTH)
    ob = pl.pallas_call(
        functools.partial(_hgrn_kernel, chunk=chunk, n_chunks=th // chunk),
        grid=(B, S // th),
        in_specs=[pl.BlockSpec((None, th, hw_t), lambda b, s: (b, s, qb_t)),
                  pl.BlockSpec((None, th, hw_t), lambda b, s: (b, s, 0)),
                  pl.BlockSpec((None, th, hw_t), lambda b, s: (b, s, ib_t)),
                  pl.BlockSpec((None, th, hw_t), lambda b, s: (b, s, zb_t)),
                  pl.BlockSpec((1, HGRN_VAL_DIM), lambda b, s: (0, 0))],
        out_specs=pl.BlockSpec((None, th, hw_t), lambda b, s: (b, s, 0)),
        out_shape=jax.ShapeDtypeStruct((B, S, hw_t), BF16),
        scratch_shapes=[pltpu.VMEM((HGRN_HEADS, HGRN_VAL_DIM, HGRN_KEY_DIM), F32)],
        compiler_params=_cparams(("arbitrary", "arbitrary"), 32),
        name="hgrn2",
    )(main_b, fgate.reshape(B, S, hw_t), main_b, main_b, hgrn_onorm_g[0].reshape(1, HGRN_VAL_DIM))
    ob = ob.reshape(N, hw_t)

    tk = 256
    tiles_per_b5 = S // tk
    za_t = _MAIN_COLS["za"][0] // ATTN_WIDTH
    ga_t = _MAIN_COLS["ga"][0] // D
    gb_t = _MAIN_COLS["gb"][0] // D
    expand_mat = np.zeros((V7X_LANES, ATTN_WIDTH), np.float32)
    for h in range(ATTN_HEADS):
        expand_mat[h, h * ATTN_HEAD_DIM:(h + 1) * ATTN_HEAD_DIM] = 1.0
    (o1, l1), (o2, l2), (o3, l3) = attn_outs
    row_spec = lambda w, t=0: pl.BlockSpec((tk, w), lambda i, t=t: (i, t))
    full_spec = lambda a, b: pl.BlockSpec((a, b), lambda i: (0, 0))
    out = pl.pallas_call(
        _merge_kernel,
        grid=(N // tk,),
        in_specs=[row_spec(ATTN_WIDTH), row_spec(ATTN_WIDTH), row_spec(ATTN_WIDTH),
                  row_spec(V7X_LANES), row_spec(V7X_LANES), row_spec(V7X_LANES),
                  row_spec(ATTN_WIDTH, za_t), row_spec(HGRN_WIDTH),
                  row_spec(D, ga_t), row_spec(D, gb_t), row_spec(D),
                  pl.BlockSpec((None, 1, 3 * D), lambda i: (i // tiles_per_b5, 0, 0)),
                  full_spec(ATTN_WIDTH, D), full_spec(HGRN_WIDTH, D), full_spec(D, D),
                  full_spec(1, D), full_spec(V7X_LANES, ATTN_WIDTH)],
        out_specs=pl.BlockSpec((tk, D), lambda i: (i, 0)),
        out_shape=jax.ShapeDtypeStruct((N, D), F32),
        compiler_params=_cparams(("arbitrary",), 40),
        name="gated_merge",
    )(o1, o2, o3, l1, l2, l3, main, ob, main, main, x2, mod3,
      w_branch_a[0].astype(BF16), w_branch_b[0].astype(BF16), w_out[0].astype(BF16),
      final_g.reshape(1, D), jnp.asarray(expand_mat, BF16))
    return out.reshape(B, S, D)
```

```python
import functools
import math
import types

import numpy as np
import jax
import jax.numpy as jnp
from jax import lax
from jax.experimental import pallas as pl
from jax.experimental.pallas import tpu as pltpu

D_MODEL = 1024
ATTN_HEADS = 8
ATTN_HEAD_DIM = 64
ATTN_WIDTH = ATTN_HEADS * ATTN_HEAD_DIM
DILATED_PATTERNS = ((128, 1), (512, 4), (2048, 16))
ATTN_BLOCK = 128
N_BUCKETS = 32
MAX_DISTANCE = 2048
NEG_INF = -1e30
HGRN_HEADS = 8
HGRN_KEY_DIM = 128
HGRN_VAL_DIM = 128
HGRN_WIDTH = HGRN_HEADS * HGRN_VAL_DIM
EPS = 1e-6

V7X_LANES = 128
V7X_SUBLANES = 8

F32 = jnp.float32
BF16 = jnp.bfloat16

_MAIN_COLS = {}
_off = 0
for _name, _w in (("qa", ATTN_WIDTH), ("ka", ATTN_WIDTH), ("va", ATTN_WIDTH), ("za", ATTN_WIDTH),
                  ("qb", HGRN_WIDTH), ("ib", HGRN_WIDTH), ("zb", HGRN_WIDTH),
                  ("ga", D_MODEL), ("gb", D_MODEL)):
    _MAIN_COLS[_name] = (_off, _w)
    _off += _w
MAIN_WIDTH = _off
PROJ_TN = 512
_PROJ_KINDS = (["qscale"] + ["kv"] * 2 + ["silu"] + ["silu"] * 2 + ["forget"] * 2
               + ["id"] * 2 + ["silu"] * 2 + ["sigmoid"] * 4)
_F_TILE0 = _PROJ_KINDS.index("forget")
_F_TILES = _PROJ_KINDS.count("forget")
_QKV_TILES = 3

PERM_DIL = 16
PERM_TILE = 512
PERM_SLAB = PERM_TILE // PERM_DIL
ATTN_SUPER = PERM_DIL * ATTN_BLOCK
ATTN_BLOCKS_PER_ITER = 2
ATTN_SKEW = 2
HGRN_CHUNKS_PER_ITER = 2
HGRN_SKEW = 1


def _sigmoid(x):
    return 1.0 / (1.0 + jnp.exp(-x))


def _any_eq(j, values):
    return functools.reduce(jnp.logical_or, [j == v for v in values])


def _mod_kernel(c_ref, w_ref, b_ref, o_ref):
    c = c_ref[...]
    sc = c * _sigmoid(c)
    o_ref[...] = jnp.dot(sc, w_ref[...], precision=lax.Precision.HIGHEST,
                         preferred_element_type=F32) + b_ref[...]


def _lower_bound_kernel(hl_ref, o_ref):
    hl = hl_ref[...]
    m = jnp.max(hl, axis=0, keepdims=True)
    e = jnp.exp(hl - m)
    o_ref[...] = e[0:1, :] / jnp.sum(e, axis=0, keepdims=True)


def _bias_table_kernel(rb_ref, bucket_ref, o_ref):
    bk = bucket_ref[...]
    for h in range(ATTN_HEADS):
        acc = jnp.full(bk.shape, NEG_INF, F32)
        for u in range(N_BUCKETS):
            acc = jnp.where(bk == u, rb_ref[u, h], acc)
        o_ref[h] = acc


def _bucket_tables():
    qi = np.arange(ATTN_BLOCK)[:, None]
    kj = np.arange(2 * ATTN_BLOCK)[None, :]
    delta = qi + ATTN_BLOCK - kj
    max_exact = N_BUCKETS // 2
    tabs = []
    for window, dilation in DILATED_PATTERNS:
        span = window // dilation
        band = (delta >= 0) & (delta <= span)
        dist = np.clip(delta, 0, None) * dilation
        n = dist.astype(np.float32)
        large = max_exact + (np.log(np.maximum(n, 1.0) / max_exact)
                             / math.log(MAX_DISTANCE / max_exact)
                             * (N_BUCKETS - max_exact)).astype(np.int32)
        large = np.minimum(large, N_BUCKETS - 1)
        bucket = np.where(dist < max_exact, dist, large)
        tab = np.where(band, bucket, -1).astype(np.int32)
        if dilation > 1:
            order = _gather_order(dilation)
            cols = np.concatenate([order, ATTN_BLOCK + order])
            tab = tab[order][:, cols]
        tabs.append(tab)
    return np.stack(tabs, 0)


def _gather_order(dilation):
    per_tile = PERM_DIL // dilation
    slab = np.arange(ATTN_BLOCK) // PERM_SLAB
    m = np.arange(ATTN_BLOCK) % PERM_SLAB
    if per_tile == 1:
        return slab * PERM_SLAB + m
    assert per_tile * PERM_SLAB == ATTN_BLOCK
    return per_tile * m + slab


def _inproj_kernel(x_ref, mod_ref, g_ref, lb_ref, w_ref, om_ref, of_ref, op_ref, accl_ref):
    x = x_ref[...]
    ms = jnp.mean(x * x, axis=-1, keepdims=True)
    y = x * lax.rsqrt(ms + EPS) * g_ref[...]
    shift = mod_ref[:, 0:D_MODEL]
    scale = mod_ref[:, D_MODEL:2 * D_MODEL]
    h = (y * (1.0 + scale) + shift).astype(BF16)

    def write_perm(j, acc):
        for c in range(PROJ_TN // V7X_LANES):
            accl_ref[c] = acc[:, c * V7X_LANES:(c + 1) * V7X_LANES]
            col0 = j * PROJ_TN + c * V7X_LANES
            for r in range(PERM_DIL):
                rows = accl_ref[c, pl.ds(r, PERM_SLAB, stride=PERM_DIL), :]
                op_ref[r * PERM_SLAB:(r + 1) * PERM_SLAB, col0:col0 + V7X_LANES] = rows.astype(BF16)

    jm = 0
    for j, kind in enumerate(_PROJ_KINDS):
        acc = jnp.dot(h, w_ref[:, j * PROJ_TN:(j + 1) * PROJ_TN], preferred_element_type=F32)
        if kind == "forget":
            jf = j - _F_TILE0
            lb = lb_ref[:, jf * PROJ_TN:(jf + 1) * PROJ_TN]
            of_ref[:, jf * PROJ_TN:(jf + 1) * PROJ_TN] = lb + (1.0 - lb) * _sigmoid(acc)
            continue
        if kind == "qscale":
            acc = acc * (ATTN_HEAD_DIM ** -0.5)
        elif kind == "silu":
            acc = acc * _sigmoid(acc)
        elif kind == "sigmoid":
            acc = _sigmoid(acc)
        om_ref[:, jm * PROJ_TN:(jm + 1) * PROJ_TN] = acc.astype(BF16)
        if j < _QKV_TILES:
            write_perm(j, acc)
        jm += 1


def _pair_lanes(hp):
    return slice(hp * V7X_LANES, (hp + 1) * V7X_LANES)


def _attn_scores(q, k, bias_ref, hp, no_prev, lane):
    blk = q.shape[0]
    low = lane < ATTN_HEAD_DIM
    zero = jnp.zeros_like(q)
    q2 = jnp.concatenate([jnp.where(low, q, zero), jnp.where(low, zero, q)], axis=0)
    s = lax.dot_general(q2, k, (((1,), (1,)), ((), ())), preferred_element_type=F32)
    bias2 = bias_ref[pl.ds(2 * hp, 2)].reshape(2 * blk, 2 * blk)
    s = s + jnp.where(jnp.concatenate([no_prev, no_prev], axis=0), NEG_INF, bias2)
    return s, jnp.max(s, axis=-1, keepdims=True)


def _attn_values(s, m, v, hp, lane, lse_all):
    blk = s.shape[0] // 2
    p = jnp.exp(s - m).astype(BF16)
    h0, h1 = 2 * hp, 2 * hp + 1
    low = lane < ATTN_HEAD_DIM
    low_v = lax.broadcasted_iota(jnp.int32, v.shape, 1) < ATTN_HEAD_DIM
    one = jnp.ones_like(v)
    o0 = jnp.dot(p[:blk], jnp.where(low_v, v, one), preferred_element_type=F32)
    o1 = jnp.dot(p[blk:], jnp.where(low_v, one, v), preferred_element_type=F32)
    num = jnp.where(low, o0, o1)
    den_swapped = jnp.where(low, o1, o0)
    den = pltpu.roll(den_swapped, ATTN_HEAD_DIM, 1)
    is_h1 = lane == h1
    lse = jnp.where(is_h1, m[blk:], m[:blk]) + jnp.log(jnp.where(is_h1, den_swapped, den))
    lse_all = jnp.where(jnp.logical_or(lane == h0, is_h1), lse, lse_all)
    return num / den, lse_all


def _attn_pipeline(blocks, bias_ref, lane):
    n_pairs = ATTN_HEADS // 2
    items = [(bi, hp) for bi in range(len(blocks)) for hp in range(n_pairs)]
    lse = [jnp.zeros(lane.shape, F32) for _ in blocks]
    pending = {}
    for step in range(len(items) + ATTN_SKEW):
        if step < len(items):
            bi, hp = items[step]
            b = blocks[bi]
            pending[step] = _attn_scores(b.get_q(hp), b.get_k(hp), bias_ref, hp, b.no_prev, lane)
        done = step - ATTN_SKEW
        if done >= 0:
            bi, hp = items[done]
            b = blocks[bi]
            o_pair, lse[bi] = _attn_values(*pending.pop(done), b.get_v(hp), hp, lane, lse[bi])
            b.put_o(hp, o_pair)
            if hp == n_pairs - 1:
                b.put_lse(lse[bi])


def _attn_kernel(q_ref, kp_ref, kc_ref, vp_ref, vc_ref, bias_ref, o_ref, lse_ref, *scratch, dil):
    n = pl.program_id(1)
    blk_rows = ATTN_BLOCK
    lane = lax.broadcasted_iota(jnp.int32, (blk_rows, V7X_LANES), 1)
    col = lax.broadcasted_iota(jnp.int32, (blk_rows, 2 * blk_rows), 1)
    prev_rows = kp_ref.shape[0]
    scratch = list(scratch)
    o_scr = scratch.pop(0) if dil > 1 else None
    if scratch:
        kext, vext = scratch
        kext[0:prev_rows] = kp_ref[...]
        kext[prev_rows:] = kc_ref[...]
        vext[0:prev_rows] = vp_ref[...]
        vext[prev_rows:] = vc_ref[...]
        k_prev, k_cur, v_prev, v_cur, cur_off = kext, kext, vext, vext, prev_rows
    else:
        k_prev, k_cur, v_prev, v_cur, cur_off = kp_ref, kc_ref, vp_ref, vc_ref, 0

    def make_block(blk):
        if dil == 1:
            slab = blk_rows
            offs = [blk * blk_rows]
            first = jnp.logical_and(n == 0, blk == 0)
        elif dil == PERM_DIL:
            slab = PERM_SLAB
            offs = [t * PERM_TILE + blk * PERM_SLAB for t in range(blk_rows // slab)]
            tok0 = [t * PERM_TILE + blk for t in range(blk_rows // slab)]
            first = n == 0
        else:
            slab = PERM_SLAB
            tile = lax.shift_right_logical(blk, 2)
            res = jnp.bitwise_and(blk, dil - 1)
            offs = [tile * PERM_TILE + (res + dil * j) * PERM_SLAB for j in range(blk_rows // slab)]
            tok0 = [tile * PERM_TILE + res + dil * j for j in range(blk_rows // slab)]
            first = jnp.logical_and(n == 0, tile == 0)
        offs = [pl.multiple_of(o, slab) for o in offs]

        def slabs(ref, hp, shift=0, rows=slab):
            return [ref[pl.ds(o + shift, rows), _pair_lanes(hp)] for o in offs]

        def put_o(hp, o_pair):
            if dil == 1:
                o_ref[pl.ds(offs[0], slab), _pair_lanes(hp)] = o_pair.astype(BF16)
                return
            for j, t0 in enumerate(tok0):
                o_scr[hp, pl.ds(t0, slab, stride=PERM_DIL), :] = o_pair[j * slab:(j + 1) * slab]

        def put_lse(lse_all):
            if dil == 1:
                lse_ref[pl.ds(offs[0], slab), :] = lse_all
                return
            for j, t0 in enumerate(tok0):
                lse_ref[pl.ds(t0, slab, stride=PERM_DIL), :] = lse_all[j * slab:(j + 1) * slab]

        if dil == 1:
            get_k = lambda hp: slabs(k_cur, hp, rows=2 * slab)[0]
            get_v = lambda hp: slabs(v_cur, hp, rows=2 * slab)[0]
        else:
            get_k = lambda hp: jnp.concatenate(slabs(k_prev, hp) + slabs(k_cur, hp, cur_off), axis=0)
            get_v = lambda hp: jnp.concatenate(slabs(v_prev, hp) + slabs(v_cur, hp, cur_off), axis=0)
        return types.SimpleNamespace(
            get_q=lambda hp: jnp.concatenate(slabs(q_ref, hp), axis=0), get_k=get_k, get_v=get_v,
            put_o=put_o, put_lse=put_lse, no_prev=jnp.logical_and(col < blk_rows, first))

    def body(it, carry):
        _attn_pipeline([make_block(it * ATTN_BLOCKS_PER_ITER + u) for u in range(ATTN_BLOCKS_PER_ITER)],
                       bias_ref, lane)
        return carry

    lax.fori_loop(0, ATTN_SUPER // ATTN_BLOCK // ATTN_BLOCKS_PER_ITER, body, 0)
    if dil > 1:
        for hp in range(ATTN_HEADS // 2):
            o_ref[:, _pair_lanes(hp)] = o_scr[hp].astype(BF16)


def _roll_rows(a, shift):
    rows = a.shape[0]
    if abs(shift) >= V7X_SUBLANES:
        return pltpu.roll(a, shift % rows, 0)
    grouped = a.reshape(rows // V7X_SUBLANES, V7X_SUBLANES, a.shape[1])
    return pltpu.roll(grouped, shift % V7X_SUBLANES, 1).reshape(a.shape)


def _hgrn_kernel(q_ref, f_ref, i_ref, z_ref, g_ref, o_ref, state_ref, *, chunk, n_chunks):
    @pl.when(pl.program_id(1) == 0)
    def _():
        state_ref[...] = jnp.zeros_like(state_ref)

    C = chunk
    n_levels = C.bit_length() - 1
    g_on = g_ref[...]
    nt = (((1,), (1,)), ((), ()))

    def chunk_body(ci, carry):
        row = lax.broadcasted_iota(jnp.int32, (C, HGRN_KEY_DIM), 0)
        odds = [jnp.bitwise_and(row, 1 << lvl) != 0 for lvl in range(n_levels)]
        tt = lax.broadcasted_iota(jnp.int32, (C, C), 0)
        ss = lax.broadcasted_iota(jnp.int32, (C, C), 1)
        owner = jnp.where(tt > ss, 32 - lax.clz(jnp.bitwise_xor(tt, ss)),
                          jnp.where(tt == ss, 0, -1))
        owned = [owner == lvl for lvl in range(n_levels + 1)]

        def chunk_rows(u):
            return pl.ds(pl.multiple_of((ci * HGRN_CHUNKS_PER_ITER + u) * C, C), C)

        def scan(u, h):
            rows = chunk_rows(u)
            hs = slice(h * HGRN_KEY_DIM, (h + 1) * HGRN_KEY_DIM)
            f = f_ref[rows, hs]
            q = q_ref[rows, hs]
            k = (1.0 - f).astype(BF16)
            a = jnp.where(owned[0], lax.dot_general(q, k, nt, preferred_element_type=F32), 0.0)
            x = jnp.where(odds[0], f, 1.0)
            y = jnp.where(odds[0], 1.0, f)
            for lvl in range(n_levels):
                m = 1 << lvl
                e = x.astype(BF16)
                pm = lax.dot_general(q * e, k * e, nt, preferred_element_type=F32)
                a = jnp.where(owned[lvl + 1], pm, a)
                tot = x * y if lvl else f
                if 2 * m == V7X_SUBLANES:
                    partner = _roll_rows(tot, m)
                else:
                    partner = jnp.where(odds[lvl], _roll_rows(tot, m), _roll_rows(tot, -m))
                z = x * partner
                keep = odds[lvl] == odds[lvl + 1] if lvl + 1 < n_levels else jnp.logical_not(odds[lvl])
                x, y = jnp.where(keep, z, y), jnp.where(keep, y, z)
            return a.astype(BF16), q * y.astype(BF16), k * x.astype(BF16), x[0:1, :] * y[0:1, :]

        def finish(u, h, a, q_dec, k_dec, decay):
            rows = chunk_rows(u)
            hs = slice(h * HGRN_KEY_DIM, (h + 1) * HGRN_KEY_DIM)
            v = i_ref[rows, hs]
            st = state_ref[h]
            o = (jnp.dot(a, v, preferred_element_type=F32)
                 + lax.dot_general(q_dec, st.astype(BF16), nt, preferred_element_type=F32))
            upd = lax.dot_general(v, k_dec, (((0,), (0,)), ((), ())), preferred_element_type=F32)
            state_ref[h] = st * decay + upd
            ms = jnp.mean(o * o, axis=-1, keepdims=True)
            y = o * lax.rsqrt(ms + EPS) * g_on
            o_ref[rows, hs] = (y * z_ref[rows, hs].astype(F32)).astype(BF16)

        items = [(u, h) for u in range(HGRN_CHUNKS_PER_ITER) for h in range(HGRN_HEADS)]
        pending = {}
        for step in range(len(items) + HGRN_SKEW):
            if step < len(items):
                pending[step] = scan(*items[step])
            done = step - HGRN_SKEW
            if done >= 0:
                finish(*items[done], *pending.pop(done))
        return carry

    assert n_chunks % HGRN_CHUNKS_PER_ITER == 0
    lax.fori_loop(0, n_chunks // HGRN_CHUNKS_PER_ITER, chunk_body, 0)


def _merge_kernel(o1_ref, o2_ref, o3_ref, l1_ref, l2_ref, l3_ref, za_ref, ob_ref,
                  sga_ref, sgb_ref, x_ref, mod_ref, wa_ref, wb_ref, wo_ref, fg_ref,
                  ex_ref, out_ref):
    l1, l2, l3 = l1_ref[...], l2_ref[...], l3_ref[...]
    mx = jnp.maximum(jnp.maximum(l1, l2), l3)
    e1, e2, e3 = jnp.exp(l1 - mx), jnp.exp(l2 - mx), jnp.exp(l3 - mx)
    inv = 1.0 / (e1 + e2 + e3)
    ex = ex_ref[...]

    def expand(w):
        hi = w.astype(BF16)
        lo = (w - hi.astype(F32)).astype(BF16)
        return (jnp.dot(hi, ex, preferred_element_type=F32)
                + jnp.dot(lo, ex, preferred_element_type=F32))

    oa = (expand(e1 * inv) * o1_ref[...].astype(F32)
          + expand(e2 * inv) * o2_ref[...].astype(F32)
          + expand(e3 * inv) * o3_ref[...].astype(F32))
    oa = (oa * za_ref[...].astype(F32)).astype(BF16)
    ya = jnp.dot(oa, wa_ref[...], preferred_element_type=F32)
    yb = jnp.dot(ob_ref[...], wb_ref[...], preferred_element_type=F32)
    y = sga_ref[...].astype(F32) * ya + sgb_ref[...].astype(F32) * yb
    z = jnp.dot(y.astype(BF16), wo_ref[...], preferred_element_type=F32)
    gate = mod_ref[:, 2 * D_MODEL:3 * D_MODEL]
    xo = x_ref[...] + gate * z
    ms = jnp.mean(xo * xo, axis=-1, keepdims=True)
    out_ref[...] = xo * lax.rsqrt(ms + EPS) * fg_ref[...]


def _cparams(sem, vmem_mb):
    return pltpu.CompilerParams(dimension_semantics=sem,
                                vmem_limit_bytes=vmem_mb * 1024 * 1024)


def kernel(x, c, w_ada, b_ada, norm_g, w_in, hgrn_onorm_g, w_branch_a, w_branch_b, w_out,
           rel_bias, hgrn_lb, final_g):
    B, S, D = x.shape
    assert D == D_MODEL and w_ada.shape[0] == 1, "single-layer kernel"
    N = B * S
    x2 = x.reshape(N, D)

    c8 = jnp.pad(c, ((0, 8 - B), (0, 0)))
    mod = pl.pallas_call(
        _mod_kernel,
        grid=(3 * D // 512,),
        in_specs=[pl.BlockSpec((8, D), lambda j: (0, 0)),
                  pl.BlockSpec((D, 512), lambda j: (0, j)),
                  pl.BlockSpec((1, 512), lambda j: (0, j))],
        out_specs=pl.BlockSpec((8, 512), lambda j: (0, j)),
        out_shape=jax.ShapeDtypeStruct((8, 3 * D), F32),
        name="adaln_mod",
    )(c8, w_ada[0], b_ada[0].reshape(1, 3 * D))
    mod3 = mod.reshape(8, 1, 3 * D)

    lb = pl.pallas_call(
        _lower_bound_kernel,
        out_shape=jax.ShapeDtypeStruct((1, HGRN_WIDTH), F32),
        name="hgrn_lower_bound",
    )(hgrn_lb)

    n_pat = len(DILATED_PATTERNS)
    bias_tab = pl.pallas_call(
        _bias_table_kernel,
        grid=(n_pat,),
        in_specs=[pl.BlockSpec(memory_space=pltpu.SMEM),
                  pl.BlockSpec((None, ATTN_BLOCK, 2 * ATTN_BLOCK), lambda g: (g, 0, 0))],
        out_specs=pl.BlockSpec((None, ATTN_HEADS, ATTN_BLOCK, 2 * ATTN_BLOCK),
                               lambda g: (g, 0, 0, 0)),
        out_shape=jax.ShapeDtypeStruct((n_pat, ATTN_HEADS, ATTN_BLOCK, 2 * ATTN_BLOCK), F32),
        name="rel_bias_table",
    )(rel_bias, jnp.asarray(_bucket_tables()))

    tm = PERM_TILE
    tiles_per_b = S // tm
    in_width = len(_PROJ_KINDS) * PROJ_TN
    assert PROJ_TN == ATTN_WIDTH and w_in.shape[2] == in_width
    resident = dict(pipeline_mode=pl.Buffered(1))
    main, fgate, qkv_p = pl.pallas_call(
        _inproj_kernel,
        grid=(N // tm,),
        in_specs=[pl.BlockSpec((tm, D), lambda i: (i, 0)),
                  pl.BlockSpec((None, 1, 3 * D), lambda i: (i // tiles_per_b, 0, 0)),
                  pl.BlockSpec((1, D), lambda i: (0, 0)),
                  pl.BlockSpec((1, HGRN_WIDTH), lambda i: (0, 0)),
                  pl.BlockSpec((D, in_width), lambda i: (0, 0), **resident)],
        out_specs=[pl.BlockSpec((tm, MAIN_WIDTH), lambda i: (i, 0)),
                   pl.BlockSpec((tm, HGRN_WIDTH), lambda i: (i, 0)),
                   pl.BlockSpec((tm, _QKV_TILES * ATTN_WIDTH), lambda i: (i, 0))],
        out_shape=[jax.ShapeDtypeStruct((N, MAIN_WIDTH), BF16),
                   jax.ShapeDtypeStruct((N, HGRN_WIDTH), F32),
                   jax.ShapeDtypeStruct((N, _QKV_TILES * ATTN_WIDTH), BF16)],
        scratch_shapes=[pltpu.VMEM((PROJ_TN // V7X_LANES, tm, V7X_LANES), F32)],
        compiler_params=_cparams(("arbitrary",), 56),
        name="inproj",
    )(x2, mod3, norm_g[0].reshape(1, D), lb, w_in[0].astype(BF16))

    qa_t = _MAIN_COLS["qa"][0] // ATTN_WIDTH
    ka_t = _MAIN_COLS["ka"][0] // ATTN_WIDTH
    va_t = _MAIN_COLS["va"][0] // ATTN_WIDTH
    attn_outs = []
    main_b = main.reshape(B, S, MAIN_WIDTH)
    qkv_pv = qkv_p.reshape(B, S, _QKV_TILES * ATTN_WIDTH)
    n_super = S // ATTN_SUPER
    for g, (window, dil) in enumerate(DILATED_PATTERNS):
        assert window // dil == ATTN_BLOCK
        src, tiles = (main_b, (qa_t, ka_t, va_t)) if dil == 1 else (qkv_pv, (0, 1, 2))
        prev_rows = {1: ATTN_BLOCK, PERM_DIL: ATTN_SUPER}.get(dil, PERM_TILE)
        per_step = ATTN_SUPER // prev_rows

        def cur_spec(t):
            return pl.BlockSpec((None, ATTN_SUPER, ATTN_WIDTH), lambda b, n, t=t: (b, n, t))

        def prev_spec(t, prev_rows=prev_rows, per_step=per_step):
            return pl.BlockSpec((None, prev_rows, ATTN_WIDTH),
                                lambda b, n, t=t: (b, jnp.maximum(n * per_step - 1, 0), t))

        scratch = []
        if dil > 1:
            scratch += [pltpu.VMEM((ATTN_WIDTH // V7X_LANES, ATTN_SUPER, V7X_LANES), F32)]
        if dil != PERM_DIL:
            scratch += [pltpu.VMEM((prev_rows + ATTN_SUPER, ATTN_WIDTH), BF16)] * 2
        o_g, lse_g = pl.pallas_call(
            functools.partial(_attn_kernel, dil=dil),
            grid=(B, n_super),
            in_specs=[cur_spec(tiles[0]), prev_spec(tiles[1]), cur_spec(tiles[1]),
                      prev_spec(tiles[2]), cur_spec(tiles[2]),
                      pl.BlockSpec((None, ATTN_HEADS, ATTN_BLOCK, 2 * ATTN_BLOCK),
                                   lambda b, n, g=g: (g, 0, 0, 0))],
            out_specs=[pl.BlockSpec((None, ATTN_SUPER, ATTN_WIDTH), lambda b, n: (b, n, 0)),
                       pl.BlockSpec((None, ATTN_SUPER, V7X_LANES), lambda b, n: (b, n, 0))],
            out_shape=[jax.ShapeDtypeStruct((B, S, ATTN_WIDTH), BF16),
                       jax.ShapeDtypeStruct((B, S, V7X_LANES), F32)],
            scratch_shapes=scratch,
            compiler_params=_cparams(("arbitrary", "arbitrary"), 48),
            name=f"dilated_attn_d{dil}",
        )(src, src, src, src, src, bias_tab)
        attn_outs.append((o_g.reshape(N, ATTN_WIDTH), lse_g.reshape(N, V7X_LANES)))

    th = 512
    chunk = 64
    hw_t = HGRN_WIDTH
    qb_t = _MAIN_COLS["qb"][0] // hw_t
    ib_t = _MAIN_COLS["ib"][0] // hw_t
    zb_t = _MAIN_COLS["zb"][0] // hw_t
    main_b = main.reshape(B, S, MAIN_WIDTH)
    ob = pl.pallas_call(
        functools.partial(_hgrn_kernel, chunk=chunk, n_chunks=th // chunk),
        grid=(B, S // th),
        in_specs=[pl.BlockSpec((None, th, hw_t), lambda b, s: (b, s, qb_t)),
                  pl.BlockSpec((None, th, hw_t), lambda b, s: (b, s, 0)),
                  pl.BlockSpec((None, th, hw_t), lambda b, s: (b, s, ib_t)),
                  pl.BlockSpec((None, th, hw_t), lambda b, s: (b, s, zb_t)),
                  pl.BlockSpec((1, HGRN_VAL_DIM), lambda b, s: (0, 0))],
        out_specs=pl.BlockSpec((None, th, hw_t), lambda b, s: (b, s, 0)),
        out_shape=jax.ShapeDtypeStruct((B, S, hw_t), BF16),
        scratch_shapes=[pltpu.VMEM((HGRN_HEADS, HGRN_VAL_DIM, HGRN_KEY_DIM), F32)],
        compiler_params=_cparams(("arbitrary", "arbitrary"), 32),
        name="hgrn2",
    )(main_b, fgate.reshape(B, S, hw_t), main_b, main_b, hgrn_onorm_g[0].reshape(1, HGRN_VAL_DIM))
    ob = ob.reshape(N, hw_t)

    tk = 256
    tiles_per_b5 = S // tk
    za_t = _MAIN_COLS["za"][0] // ATTN_WIDTH
    ga_t = _MAIN_COLS["ga"][0] // D
    gb_t = _MAIN_COLS["gb"][0] // D
    expand_mat = np.zeros((V7X_LANES, ATTN_WIDTH), np.float32)
    for h in range(ATTN_HEADS):
        expand_mat[h, h * ATTN_HEAD_DIM:(h + 1) * ATTN_HEAD_DIM] = 1.0
    (o1, l1), (o2, l2), (o3, l3) = attn_outs
    row_spec = lambda w, t=0: pl.BlockSpec((tk, w), lambda i, t=t: (i, t))
    full_spec = lambda a, b: pl.BlockSpec((a, b), lambda i: (0, 0))
    out = pl.pallas_call(
        _merge_kernel,
        grid=(N // tk,),
        in_specs=[row_spec(ATTN_WIDTH), row_spec(ATTN_WIDTH), row_spec(ATTN_WIDTH),
                  row_spec(V7X_LANES), row_spec(V7X_LANES), row_spec(V7X_LANES),
                  row_spec(ATTN_WIDTH, za_t), row_spec(HGRN_WIDTH),
                  row_spec(D, ga_t), row_spec(D, gb_t), row_spec(D),
                  pl.BlockSpec((None, 1, 3 * D), lambda i: (i // tiles_per_b5, 0, 0)),
                  full_spec(ATTN_WIDTH, D), full_spec(HGRN_WIDTH, D), full_spec(D, D),
                  full_spec(1, D), full_spec(V7X_LANES, ATTN_WIDTH)],
        out_specs=pl.BlockSpec((tk, D), lambda i: (i, 0)),
        out_shape=jax.ShapeDtypeStruct((N, D), F32),
        compiler_params=_cparams(("arbitrary",), 40),
        name="gated_merge",
    )(o1, o2, o3, l1, l2, l3, main, ob, main, main, x2, mod3,
      w_branch_a[0].astype(BF16), w_branch_b[0].astype(BF16), w_out[0].astype(BF16),
      final_g.reshape(1, D), jnp.asarray(expand_mat, BF16))
    return out.reshape(B, S, D)
```

```python
import functools
import math
import types

import numpy as np
import jax
import jax.numpy as jnp
from jax import lax
from jax.experimental import pallas as pl
from jax.experimental.pallas import tpu as pltpu

D_MODEL = 1024
ATTN_HEADS = 8
ATTN_HEAD_DIM = 64
ATTN_WIDTH = ATTN_HEADS * ATTN_HEAD_DIM
DILATED_PATTERNS = ((128, 1), (512, 4), (2048, 16))
ATTN_BLOCK = 128
N_BUCKETS = 32
MAX_DISTANCE = 2048
NEG_INF = -1e30
HGRN_HEADS = 8
HGRN_KEY_DIM = 128
HGRN_VAL_DIM = 128
HGRN_WIDTH = HGRN_HEADS * HGRN_VAL_DIM
EPS = 1e-6

V7X_LANES = 128
V7X_SUBLANES = 8

F32 = jnp.float32
BF16 = jnp.bfloat16

_MAIN_COLS = {}
_off = 0
for _name, _w in (("qa", ATTN_WIDTH), ("ka", ATTN_WIDTH), ("va", ATTN_WIDTH), ("za", ATTN_WIDTH),
                  ("qb", HGRN_WIDTH), ("ib", HGRN_WIDTH), ("zb", HGRN_WIDTH),
                  ("ga", D_MODEL), ("gb", D_MODEL)):
    _MAIN_COLS[_name] = (_off, _w)
    _off += _w
MAIN_WIDTH = _off
PROJ_TN = 512
PROJ_PIECE_M = 256
PROJ_PIECE_N = 256
PROJ_SKEW = 3
_PROJ_KINDS = (["qscale"] + ["kv"] * 2 + ["silu"] + ["silu"] * 2 + ["forget"] * 2
               + ["id"] * 2 + ["silu"] * 2 + ["sigmoid"] * 4)
_F_TILE0 = _PROJ_KINDS.index("forget")
_F_TILES = _PROJ_KINDS.count("forget")
_QKV_TILES = 3

PERM_DIL = 16
PERM_TILE = 512
PERM_SLAB = PERM_TILE // PERM_DIL
ATTN_SUPER = PERM_DIL * ATTN_BLOCK
ATTN_BLOCKS_PER_ITER = 2
ATTN_SKEW = 2
MERGE_PIECE = 256
HGRN_CHUNKS_PER_ITER = 2
HGRN_SKEW = 1


def _sigmoid(x):
    return 1.0 / (1.0 + jnp.exp(-x))


def _any_eq(j, values):
    return functools.reduce(jnp.logical_or, [j == v for v in values])


def _mod_kernel(c_ref, w_ref, b_ref, o_ref):
    c = c_ref[...]
    sc = c * _sigmoid(c)
    o_ref[...] = jnp.dot(sc, w_ref[...], precision=lax.Precision.HIGHEST,
                         preferred_element_type=F32) + b_ref[...]


def _lower_bound_kernel(hl_ref, o_ref):
    hl = hl_ref[...]
    m = jnp.max(hl, axis=0, keepdims=True)
    e = jnp.exp(hl - m)
    o_ref[...] = e[0:1, :] / jnp.sum(e, axis=0, keepdims=True)


def _bias_table_kernel(rb_ref, bucket_ref, o_ref):
    bk = bucket_ref[...]
    for h in range(ATTN_HEADS):
        acc = jnp.full(bk.shape, NEG_INF, F32)
        for u in range(N_BUCKETS):
            acc = jnp.where(bk == u, rb_ref[u, h], acc)
        o_ref[h] = acc


def _bucket_tables():
    qi = np.arange(ATTN_BLOCK)[:, None]
    kj = np.arange(2 * ATTN_BLOCK)[None, :]
    delta = qi + ATTN_BLOCK - kj
    max_exact = N_BUCKETS // 2
    tabs = []
    for window, dilation in DILATED_PATTERNS:
        span = window // dilation
        band = (delta >= 0) & (delta <= span)
        dist = np.clip(delta, 0, None) * dilation
        n = dist.astype(np.float32)
        large = max_exact + (np.log(np.maximum(n, 1.0) / max_exact)
                             / math.log(MAX_DISTANCE / max_exact)
                             * (N_BUCKETS - max_exact)).astype(np.int32)
        large = np.minimum(large, N_BUCKETS - 1)
        bucket = np.where(dist < max_exact, dist, large)
        tab = np.where(band, bucket, -1).astype(np.int32)
        if dilation > 1:
            order = _gather_order(dilation)
            cols = np.concatenate([order, ATTN_BLOCK + order])
            tab = tab[order][:, cols]
        tabs.append(tab)
    return np.stack(tabs, 0)


def _gather_order(dilation):
    per_tile = PERM_DIL // dilation
    slab = np.arange(ATTN_BLOCK) // PERM_SLAB
    m = np.arange(ATTN_BLOCK) % PERM_SLAB
    if per_tile == 1:
        return slab * PERM_SLAB + m
    assert per_tile * PERM_SLAB == ATTN_BLOCK
    return per_tile * m + slab


def _inproj_kernel(x_ref, mod_ref, g_ref, lb_ref, w_ref, om_ref, of_ref, op_ref, h_ref, accl_ref):
    tm = x_ref.shape[0]
    x = x_ref[...]
    ms = jnp.mean(x * x, axis=-1, keepdims=True)
    y = x * lax.rsqrt(ms + EPS) * g_ref[...]
    shift = mod_ref[:, 0:D_MODEL]
    scale = mod_ref[:, D_MODEL:2 * D_MODEL]
    h_ref[...] = (y * (1.0 + scale) + shift).astype(BF16)

    n_sub = PROJ_TN // PROJ_PIECE_N
    slab = PROJ_PIECE_M // PERM_DIL

    def write_perm(mc, col0, acc, slot):
        for c in range(PROJ_PIECE_N // V7X_LANES):
            accl_ref[slot, c] = acc[:, c * V7X_LANES:(c + 1) * V7X_LANES]
            for r in range(PERM_DIL):
                rows = accl_ref[slot, c, pl.ds(r, slab, stride=PERM_DIL), :]
                row0 = r * PERM_SLAB + mc * slab
                op_ref[row0:row0 + slab, col0 + c * V7X_LANES:col0 + (c + 1) * V7X_LANES] = rows.astype(BF16)

    def epilogue(j, sub, mc, acc):
        kind = _PROJ_KINDS[j]
        rows = slice(mc * PROJ_PIECE_M, (mc + 1) * PROJ_PIECE_M)
        if kind == "forget":
            col0 = (j - _F_TILE0) * PROJ_TN + sub * PROJ_PIECE_N
            lb = lb_ref[:, col0:col0 + PROJ_PIECE_N]
            of_ref[rows, col0:col0 + PROJ_PIECE_N] = lb + (1.0 - lb) * _sigmoid(acc)
            return
        if kind == "qscale":
            acc = acc * (ATTN_HEAD_DIM ** -0.5)
        elif kind == "silu":
            acc = acc * _sigmoid(acc)
        elif kind == "sigmoid":
            acc = _sigmoid(acc)
        jm = j if j < _F_TILE0 else j - _F_TILES
        col0 = jm * PROJ_TN + sub * PROJ_PIECE_N
        om_ref[rows, col0:col0 + PROJ_PIECE_N] = acc.astype(BF16)
        if j < _QKV_TILES:
            write_perm(mc, col0, acc, (sub * (tm // PROJ_PIECE_M) + mc) % accl_ref.shape[0])

    items = [(j, sub, mc) for j in range(len(_PROJ_KINDS)) for sub in range(n_sub)
             for mc in range(tm // PROJ_PIECE_M)]
    pending = {}
    for step in range(len(items) + PROJ_SKEW):
        if step < len(items):
            j, sub, mc = items[step]
            col0 = j * PROJ_TN + sub * PROJ_PIECE_N
            pending[step] = jnp.dot(h_ref[mc * PROJ_PIECE_M:(mc + 1) * PROJ_PIECE_M, :],
                                    w_ref[:, col0:col0 + PROJ_PIECE_N], preferred_element_type=F32)
        done = step - PROJ_SKEW
        if done >= 0:
            epilogue(*items[done], pending.pop(done))


def _pair_lanes(hp):
    return slice(hp * V7X_LANES, (hp + 1) * V7X_LANES)


def _attn_scores(q, k, bias_ref, hp, no_prev, lane):
    blk = q.shape[0]
    low = lane < ATTN_HEAD_DIM
    zero = jnp.zeros_like(q)
    q2 = jnp.concatenate([jnp.where(low, q, zero), jnp.where(low, zero, q)], axis=0)
    s = lax.dot_general(q2, k, (((1,), (1,)), ((), ())), preferred_element_type=F32)
    bias2 = bias_ref[pl.ds(2 * hp, 2)].reshape(2 * blk, 2 * blk)
    s = s + jnp.where(jnp.concatenate([no_prev, no_prev], axis=0), NEG_INF, bias2)
    return s, jnp.max(s, axis=-1, keepdims=True)


def _attn_values(s, m, v, hp, lane, lse_all):
    blk = s.shape[0] // 2
    p = jnp.exp(s - m).astype(BF16)
    h0, h1 = 2 * hp, 2 * hp + 1
    low = lane < ATTN_HEAD_DIM
    low_v = lax.broadcasted_iota(jnp.int32, v.shape, 1) < ATTN_HEAD_DIM
    one = jnp.ones_like(v)
    o0 = jnp.dot(p[:blk], jnp.where(low_v, v, one), preferred_element_type=F32)
    o1 = jnp.dot(p[blk:], jnp.where(low_v, one, v), preferred_element_type=F32)
    num = jnp.where(low, o0, o1)
    den_swapped = jnp.where(low, o1, o0)
    den = pltpu.roll(den_swapped, ATTN_HEAD_DIM, 1)
    is_h1 = lane == h1
    lse = jnp.where(is_h1, m[blk:], m[:blk]) + jnp.log(jnp.where(is_h1, den_swapped, den))
    lse_all = jnp.where(jnp.logical_or(lane == h0, is_h1), lse, lse_all)
    return num / den, lse_all


def _attn_pipeline(blocks, bias_ref, lane):
    n_pairs = ATTN_HEADS // 2
    items = [(bi, hp) for bi in range(len(blocks)) for hp in range(n_pairs)]
    lse = [jnp.zeros(lane.shape, F32) for _ in blocks]
    pending = {}
    for step in range(len(items) + ATTN_SKEW):
        if step < len(items):
            bi, hp = items[step]
            b = blocks[bi]
            pending[step] = _attn_scores(b.get_q(hp), b.get_k(hp), bias_ref, hp, b.no_prev, lane)
        done = step - ATTN_SKEW
        if done >= 0:
            bi, hp = items[done]
            b = blocks[bi]
            o_pair, lse[bi] = _attn_values(*pending.pop(done), b.get_v(hp), hp, lane, lse[bi])
            b.put_o(hp, o_pair)
            if hp == n_pairs - 1:
                b.put_lse(lse[bi])


def _attn_kernel(q_ref, kp_ref, kc_ref, vp_ref, vc_ref, bias_ref, o_ref, lse_ref, *scratch, dil):
    n = pl.program_id(1)
    blk_rows = ATTN_BLOCK
    lane = lax.broadcasted_iota(jnp.int32, (blk_rows, V7X_LANES), 1)
    col = lax.broadcasted_iota(jnp.int32, (blk_rows, 2 * blk_rows), 1)
    prev_rows = kp_ref.shape[0]
    scratch = list(scratch)
    o_scr = scratch.pop(0) if dil > 1 else None
    if scratch:
        kext, vext = scratch
        kext[0:prev_rows] = kp_ref[...]
        kext[prev_rows:] = kc_ref[...]
        vext[0:prev_rows] = vp_ref[...]
        vext[prev_rows:] = vc_ref[...]
        k_prev, k_cur, v_prev, v_cur, cur_off = kext, kext, vext, vext, prev_rows
    else:
        k_prev, k_cur, v_prev, v_cur, cur_off = kp_ref, kc_ref, vp_ref, vc_ref, 0

    def make_block(blk):
        if dil == 1:
            slab = blk_rows
            offs = [blk * blk_rows]
            first = jnp.logical_and(n == 0, blk == 0)
        elif dil == PERM_DIL:
            slab = PERM_SLAB
            offs = [t * PERM_TILE + blk * PERM_SLAB for t in range(blk_rows // slab)]
            tok0 = [t * PERM_TILE + blk for t in range(blk_rows // slab)]
            first = n == 0
        else:
            slab = PERM_SLAB
            tile = lax.shift_right_logical(blk, 2)
            res = jnp.bitwise_and(blk, dil - 1)
            offs = [tile * PERM_TILE + (res + dil * j) * PERM_SLAB for j in range(blk_rows // slab)]
            tok0 = [tile * PERM_TILE + res + dil * j for j in range(blk_rows // slab)]
            first = jnp.logical_and(n == 0, tile == 0)
        offs = [pl.multiple_of(o, slab) for o in offs]

        def slabs(ref, hp, shift=0, rows=slab):
            return [ref[pl.ds(o + shift, rows), _pair_lanes(hp)] for o in offs]

        def put_o(hp, o_pair):
            if dil == 1:
                o_ref[pl.ds(offs[0], slab), _pair_lanes(hp)] = o_pair.astype(BF16)
                return
            for j, t0 in enumerate(tok0):
                o_scr[hp, pl.ds(t0, slab, stride=PERM_DIL), :] = o_pair[j * slab:(j + 1) * slab]

        def put_lse(lse_all):
            if dil == 1:
                lse_ref[pl.ds(offs[0], slab), :] = lse_all
                return
            for j, t0 in enumerate(tok0):
                lse_ref[pl.ds(t0, slab, stride=PERM_DIL), :] = lse_all[j * slab:(j + 1) * slab]

        if dil == 1:
            get_k = lambda hp: slabs(k_cur, hp, rows=2 * slab)[0]
            get_v = lambda hp: slabs(v_cur, hp, rows=2 * slab)[0]
        else:
            get_k = lambda hp: jnp.concatenate(slabs(k_prev, hp) + slabs(k_cur, hp, cur_off), axis=0)
            get_v = lambda hp: jnp.concatenate(slabs(v_prev, hp) + slabs(v_cur, hp, cur_off), axis=0)
        return types.SimpleNamespace(
            get_q=lambda hp: jnp.concatenate(slabs(q_ref, hp), axis=0), get_k=get_k, get_v=get_v,
            put_o=put_o, put_lse=put_lse, no_prev=jnp.logical_and(col < blk_rows, first))

    def body(it, carry):
        _attn_pipeline([make_block(it * ATTN_BLOCKS_PER_ITER + u) for u in range(ATTN_BLOCKS_PER_ITER)],
                       bias_ref, lane)
        return carry

    lax.fori_loop(0, ATTN_SUPER // ATTN_BLOCK // ATTN_BLOCKS_PER_ITER, body, 0)
    if dil > 1:
        for hp in range(ATTN_HEADS // 2):
            o_ref[:, _pair_lanes(hp)] = o_scr[hp].astype(BF16)


def _roll_rows(a, shift):
    rows = a.shape[0]
    if abs(shift) >= V7X_SUBLANES:
        return pltpu.roll(a, shift % rows, 0)
    grouped = a.reshape(rows // V7X_SUBLANES, V7X_SUBLANES, a.shape[1])
    return pltpu.roll(grouped, shift % V7X_SUBLANES, 1).reshape(a.shape)


def _hgrn_kernel(q_ref, f_ref, i_ref, z_ref, g_ref, o_ref, state_ref, *, chunk, n_chunks):
    @pl.when(pl.program_id(1) == 0)
    def _():
        state_ref[...] = jnp.zeros_like(state_ref)

    C = chunk
    n_levels = C.bit_length() - 1
    g_on = g_ref[...]
    nt = (((1,), (1,)), ((), ()))

    def chunk_body(ci, carry):
        row = lax.broadcasted_iota(jnp.int32, (C, HGRN_KEY_DIM), 0)
        odds = [jnp.bitwise_and(row, 1 << lvl) != 0 for lvl in range(n_levels)]
        tt = lax.broadcasted_iota(jnp.int32, (C, C), 0)
        ss = lax.broadcasted_iota(jnp.int32, (C, C), 1)
        owner = jnp.where(tt > ss, 32 - lax.clz(jnp.bitwise_xor(tt, ss)),
                          jnp.where(tt == ss, 0, -1))
        owned = [owner == lvl for lvl in range(n_levels + 1)]

        def chunk_rows(u):
            return pl.ds(pl.multiple_of((ci * HGRN_CHUNKS_PER_ITER + u) * C, C), C)

        def scan(u, h):
            rows = chunk_rows(u)
            hs = slice(h * HGRN_KEY_DIM, (h + 1) * HGRN_KEY_DIM)
            f = f_ref[rows, hs]
            q = q_ref[rows, hs]
            k = (1.0 - f).astype(BF16)
            a = jnp.where(owned[0], lax.dot_general(q, k, nt, preferred_element_type=F32), 0.0)
            x = jnp.where(odds[0], f, 1.0)
            y = jnp.where(odds[0], 1.0, f)
            for lvl in range(n_levels):
                m = 1 << lvl
                e = x.astype(BF16)
                pm = lax.dot_general(q * e, k * e, nt, preferred_element_type=F32)
                a = jnp.where(owned[lvl + 1], pm, a)
                tot = x * y if lvl else f
                if 2 * m == V7X_SUBLANES:
                    partner = _roll_rows(tot, m)
                else:
                    partner = jnp.where(odds[lvl], _roll_rows(tot, m), _roll_rows(tot, -m))
                z = x * partner
                keep = odds[lvl] == odds[lvl + 1] if lvl + 1 < n_levels else jnp.logical_not(odds[lvl])
                x, y = jnp.where(keep, z, y), jnp.where(keep, y, z)
            return a.astype(BF16), q * y.astype(BF16), k * x.astype(BF16), x[0:1, :] * y[0:1, :]

        def finish(u, h, a, q_dec, k_dec, decay):
            rows = chunk_rows(u)
            hs = slice(h * HGRN_KEY_DIM, (h + 1) * HGRN_KEY_DIM)
            v = i_ref[rows, hs]
            st = state_ref[h]
            o = (jnp.dot(a, v, preferred_element_type=F32)
                 + lax.dot_general(q_dec, st.astype(BF16), nt, preferred_element_type=F32))
            upd = lax.dot_general(v, k_dec, (((0,), (0,)), ((), ())), preferred_element_type=F32)
            state_ref[h] = st * decay + upd
            ms = jnp.mean(o * o, axis=-1, keepdims=True)
            y = o * lax.rsqrt(ms + EPS) * g_on
            o_ref[rows, hs] = (y * z_ref[rows, hs].astype(F32)).astype(BF16)

        items = [(u, h) for u in range(HGRN_CHUNKS_PER_ITER) for h in range(HGRN_HEADS)]
        pending = {}
        for step in range(len(items) + HGRN_SKEW):
            if step < len(items):
                pending[step] = scan(*items[step])
            done = step - HGRN_SKEW
            if done >= 0:
                finish(*items[done], *pending.pop(done))
        return carry

    assert n_chunks % HGRN_CHUNKS_PER_ITER == 0
    lax.fori_loop(0, n_chunks // HGRN_CHUNKS_PER_ITER, chunk_body, 0)


def _merge_kernel(o1_ref, o2_ref, o3_ref, l1_ref, l2_ref, l3_ref, za_ref, ob_ref,
                  sga_ref, sgb_ref, x_ref, mod_ref, wa_ref, wb_ref, wo_ref, fg_ref,
                  ex_ref, out_ref):
    ex = ex_ref[...]
    gate = mod_ref[:, 2 * D_MODEL:3 * D_MODEL]

    def expand(w):
        hi = w.astype(BF16)
        lo = (w - hi.astype(F32)).astype(BF16)
        return jnp.dot(jnp.concatenate([hi, lo], axis=1), ex, preferred_element_type=F32)

    def mix(rows):
        l1, l2, l3 = l1_ref[rows, :], l2_ref[rows, :], l3_ref[rows, :]
        mx = jnp.maximum(jnp.maximum(l1, l2), l3)
        e1, e2, e3 = jnp.exp(l1 - mx), jnp.exp(l2 - mx), jnp.exp(l3 - mx)
        inv = 1.0 / (e1 + e2 + e3)
        oa = (expand(e1 * inv) * o1_ref[rows, :].astype(F32)
              + expand(e2 * inv) * o2_ref[rows, :].astype(F32)
              + expand(e3 * inv) * o3_ref[rows, :].astype(F32))
        return (oa * za_ref[rows, :].astype(F32)).astype(BF16)

    def branches(rows, oa):
        ya = jnp.dot(oa, wa_ref[...], preferred_element_type=F32)
        yb = jnp.dot(ob_ref[rows, :], wb_ref[...], preferred_element_type=F32)
        return (sga_ref[rows, :].astype(F32) * ya + sgb_ref[rows, :].astype(F32) * yb).astype(BF16)

    def project(rows, y):
        z = jnp.dot(y, wo_ref[...], preferred_element_type=F32)
        xo = x_ref[rows, :] + gate * z
        ms = jnp.mean(xo * xo, axis=-1, keepdims=True)
        out_ref[rows, :] = xo * lax.rsqrt(ms + EPS) * fg_ref[...]

    pieces = [slice(r, r + MERGE_PIECE) for r in range(0, x_ref.shape[0], MERGE_PIECE)]
    oas = [mix(rows) for rows in pieces]
    ys = [branches(rows, oa) for rows, oa in zip(pieces, oas)]
    for rows, y in zip(pieces, ys):
        project(rows, y)


def _cparams(sem, vmem_mb):
    return pltpu.CompilerParams(dimension_semantics=sem,
                                vmem_limit_bytes=vmem_mb * 1024 * 1024)


def kernel(x, c, w_ada, b_ada, norm_g, w_in, hgrn_onorm_g, w_branch_a, w_branch_b, w_out,
           rel_bias, hgrn_lb, final_g):
    B, S, D = x.shape
    assert D == D_MODEL and w_ada.shape[0] == 1, "single-layer kernel"
    N = B * S
    x2 = x.reshape(N, D)

    c8 = jnp.pad(c, ((0, 8 - B), (0, 0)))
    mod = pl.pallas_call(
        _mod_kernel,
        grid=(3 * D // 512,),
        in_specs=[pl.BlockSpec((8, D), lambda j: (0, 0)),
                  pl.BlockSpec((D, 512), lambda j: (0, j)),
                  pl.BlockSpec((1, 512), lambda j: (0, j))],
        out_specs=pl.BlockSpec((8, 512), lambda j: (0, j)),
        out_shape=jax.ShapeDtypeStruct((8, 3 * D), F32),
        name="adaln_mod",
    )(c8, w_ada[0], b_ada[0].reshape(1, 3 * D))
    mod3 = mod.reshape(8, 1, 3 * D)

    lb = pl.pallas_call(
        _lower_bound_kernel,
        out_shape=jax.ShapeDtypeStruct((1, HGRN_WIDTH), F32),
        name="hgrn_lower_bound",
    )(hgrn_lb)

    n_pat = len(DILATED_PATTERNS)
    bias_tab = pl.pallas_call(
        _bias_table_kernel,
        grid=(n_pat,),
        in_specs=[pl.BlockSpec(memory_space=pltpu.SMEM),
                  pl.BlockSpec((None, ATTN_BLOCK, 2 * ATTN_BLOCK), lambda g: (g, 0, 0))],
        out_specs=pl.BlockSpec((None, ATTN_HEADS, ATTN_BLOCK, 2 * ATTN_BLOCK),
                               lambda g: (g, 0, 0, 0)),
        out_shape=jax.ShapeDtypeStruct((n_pat, ATTN_HEADS, ATTN_BLOCK, 2 * ATTN_BLOCK), F32),
        name="rel_bias_table",
    )(rel_bias, jnp.asarray(_bucket_tables()))

    tm = PERM_TILE
    tiles_per_b = S // tm
    in_width = len(_PROJ_KINDS) * PROJ_TN
    assert PROJ_TN == ATTN_WIDTH and w_in.shape[2] == in_width
    resident = dict(pipeline_mode=pl.Buffered(1))
    main, fgate, qkv_p = pl.pallas_call(
        _inproj_kernel,
        grid=(N // tm,),
        in_specs=[pl.BlockSpec((tm, D), lambda i: (i, 0)),
                  pl.BlockSpec((None, 1, 3 * D), lambda i: (i // tiles_per_b, 0, 0)),
                  pl.BlockSpec((1, D), lambda i: (0, 0)),
                  pl.BlockSpec((1, HGRN_WIDTH), lambda i: (0, 0)),
                  pl.BlockSpec((D, in_width), lambda i: (0, 0), **resident)],
        out_specs=[pl.BlockSpec((tm, MAIN_WIDTH), lambda i: (i, 0)),
                   pl.BlockSpec((tm, HGRN_WIDTH), lambda i: (i, 0)),
                   pl.BlockSpec((tm, _QKV_TILES * ATTN_WIDTH), lambda i: (i, 0))],
        out_shape=[jax.ShapeDtypeStruct((N, MAIN_WIDTH), BF16),
                   jax.ShapeDtypeStruct((N, HGRN_WIDTH), F32),
                   jax.ShapeDtypeStruct((N, _QKV_TILES * ATTN_WIDTH), BF16)],
        scratch_shapes=[pltpu.VMEM((tm, D), BF16),
                        pltpu.VMEM((2, PROJ_PIECE_N // V7X_LANES, PROJ_PIECE_M, V7X_LANES), F32)],
        compiler_params=_cparams(("arbitrary",), 56),
        name="inproj",
    )(x2, mod3, norm_g[0].reshape(1, D), lb, w_in[0].astype(BF16))

    qa_t = _MAIN_COLS["qa"][0] // ATTN_WIDTH
    ka_t = _MAIN_COLS["ka"][0] // ATTN_WIDTH
    va_t = _MAIN_COLS["va"][0] // ATTN_WIDTH
    attn_outs = []
    main_b = main.reshape(B, S, MAIN_WIDTH)
    qkv_pv = qkv_p.reshape(B, S, _QKV_TILES * ATTN_WIDTH)
    n_super = S // ATTN_SUPER
    for g, (window, dil) in enumerate(DILATED_PATTERNS):
        assert window // dil == ATTN_BLOCK
        src, tiles = (main_b, (qa_t, ka_t, va_t)) if dil == 1 else (qkv_pv, (0, 1, 2))
        prev_rows = {1: ATTN_BLOCK, PERM_DIL: ATTN_SUPER}.get(dil, PERM_TILE)
        per_step = ATTN_SUPER // prev_rows

        def cur_spec(t):
            return pl.BlockSpec((None, ATTN_SUPER, ATTN_WIDTH), lambda b, n, t=t: (b, n, t))

        def prev_spec(t, prev_rows=prev_rows, per_step=per_step):
            return pl.BlockSpec((None, prev_rows, ATTN_WIDTH),
                                lambda b, n, t=t: (b, jnp.maximum(n * per_step - 1, 0), t))

        scratch = []
        if dil > 1:
            scratch += [pltpu.VMEM((ATTN_WIDTH // V7X_LANES, ATTN_SUPER, V7X_LANES), F32)]
        if dil != PERM_DIL:
            scratch += [pltpu.VMEM((prev_rows + ATTN_SUPER, ATTN_WIDTH), BF16)] * 2
        o_g, lse_g = pl.pallas_call(
            functools.partial(_attn_kernel, dil=dil),
            grid=(B, n_super),
            in_specs=[cur_spec(tiles[0]), prev_spec(tiles[1]), cur_spec(tiles[1]),
                      prev_spec(tiles[2]), cur_spec(tiles[2]),
                      pl.BlockSpec((None, ATTN_HEADS, ATTN_BLOCK, 2 * ATTN_BLOCK),
                                   lambda b, n, g=g: (g, 0, 0, 0))],
            out_specs=[pl.BlockSpec((None, ATTN_SUPER, ATTN_WIDTH), lambda b, n: (b, n, 0)),
                       pl.BlockSpec((None, ATTN_SUPER, V7X_LANES), lambda b, n: (b, n, 0))],
            out_shape=[jax.ShapeDtypeStruct((B, S, ATTN_WIDTH), BF16),
                       jax.ShapeDtypeStruct((B, S, V7X_LANES), F32)],
            scratch_shapes=scratch,
            compiler_params=_cparams(("arbitrary", "arbitrary"), 48),
            name=f"dilated_attn_d{dil}",
        )(src, src, src, src, src, bias_tab)
        attn_outs.append((o_g.reshape(N, ATTN_WIDTH), lse_g.reshape(N, V7X_LANES)))

    th = 512
    chunk = 64
    hw_t = HGRN_WIDTH
    qb_t = _MAIN_COLS["qb"][0] // hw_t
    ib_t = _MAIN_COLS["ib"][0] // hw_t
    zb_t = _MAIN_COLS["zb"][0] // hw_t
    main_b = main.reshape(B, S, MAIN_WIDTH)
    ob = pl.pallas_call(
        functools.partial(_hgrn_kernel, chunk=chunk, n_chunks=th // chunk),
        grid=(B, S // th),
        in_specs=[pl.BlockSpec((None, th, hw_t), lambda b, s: (b, s, qb_t)),
                  pl.BlockSpec((None, th, hw_t), lambda b, s: (b, s, 0)),
                  pl.BlockSpec((None, th, hw_t), lambda b, s: (b, s, ib_t)),
                  pl.BlockSpec((None, th, hw_t), lambda b, s: (b, s, zb_t)),
                  pl.BlockSpec((1, HGRN_VAL_DIM), lambda b, s: (0, 0))],
        out_specs=pl.BlockSpec((None, th, hw_t), lambda b, s: (b, s, 0)),
        out_shape=jax.ShapeDtypeStruct((B, S, hw_t), BF16),
        scratch_shapes=[pltpu.VMEM((HGRN_HEADS, HGRN_VAL_DIM, HGRN_KEY_DIM), F32)],
        compiler_params=_cparams(("arbitrary", "arbitrary"), 32),
        name="hgrn2",
    )(main_b, fgate.reshape(B, S, hw_t), main_b, main_b, hgrn_onorm_g[0].reshape(1, HGRN_VAL_DIM))
    ob = ob.reshape(N, hw_t)

    tk = 2 * MERGE_PIECE
    tiles_per_b5 = S // tk
    za_t = _MAIN_COLS["za"][0] // ATTN_WIDTH
    ga_t = _MAIN_COLS["ga"][0] // D
    gb_t = _MAIN_COLS["gb"][0] // D
    expand_mat = np.zeros((2 * V7X_LANES, ATTN_WIDTH), np.float32)
    for h in range(ATTN_HEADS):
        expand_mat[h, h * ATTN_HEAD_DIM:(h + 1) * ATTN_HEAD_DIM] = 1.0
        expand_mat[V7X_LANES + h, h * ATTN_HEAD_DIM:(h + 1) * ATTN_HEAD_DIM] = 1.0
    (o1, l1), (o2, l2), (o3, l3) = attn_outs
    row_spec = lambda w, t=0: pl.BlockSpec((tk, w), lambda i, t=t: (i, t))
    full_spec = lambda a, b: pl.BlockSpec((a, b), lambda i: (0, 0))
    out = pl.pallas_call(
        _merge_kernel,
        grid=(N // tk,),
        in_specs=[row_spec(ATTN_WIDTH), row_spec(ATTN_WIDTH), row_spec(ATTN_WIDTH),
                  row_spec(V7X_LANES), row_spec(V7X_LANES), row_spec(V7X_LANES),
                  row_spec(ATTN_WIDTH, za_t), row_spec(HGRN_WIDTH),
                  row_spec(D, ga_t), row_spec(D, gb_t), row_spec(D),
                  pl.BlockSpec((None, 1, 3 * D), lambda i: (i // tiles_per_b5, 0, 0)),
                  full_spec(ATTN_WIDTH, D), full_spec(HGRN_WIDTH, D), full_spec(D, D),
                  full_spec(1, D), full_spec(2 * V7X_LANES, ATTN_WIDTH)],
        out_specs=pl.BlockSpec((tk, D), lambda i: (i, 0)),
        out_shape=jax.ShapeDtypeStruct((N, D), F32),
        compiler_params=_cparams(("arbitrary",), 40),
        name="gated_merge",
    )(o1, o2, o3, l1, l2, l3, main, ob, main, main, x2, mod3,
      w_branch_a[0].astype(BF16), w_branch_b[0].astype(BF16), w_out[0].astype(BF16),
      final_g.reshape(1, D), jnp.asarray(expand_mat, BF16))
    return out.reshape(B, S, D)
```

```python
import functools
import math
import types

import numpy as np
import jax
import jax.numpy as jnp
from jax import lax
from jax.experimental import pallas as pl
from jax.experimental.pallas import tpu as pltpu

D_MODEL = 1024
ATTN_HEADS = 8
ATTN_HEAD_DIM = 64
ATTN_WIDTH = ATTN_HEADS * ATTN_HEAD_DIM
DILATED_PATTERNS = ((128, 1), (512, 4), (2048, 16))
ATTN_BLOCK = 128
N_BUCKETS = 32
MAX_DISTANCE = 2048
NEG_INF = -1e30
HGRN_HEADS = 8
HGRN_KEY_DIM = 128
HGRN_VAL_DIM = 128
HGRN_WIDTH = HGRN_HEADS * HGRN_VAL_DIM
EPS = 1e-6

V7X_LANES = 128
V7X_SUBLANES = 8

F32 = jnp.float32
BF16 = jnp.bfloat16

PROJ_TN = 512
PROJ_PIECE_M = 256
PROJ_PIECE_N = 256
PROJ_SKEW = 2
_PROJ_TILES = ([("qscale", "main", 0), ("id", "main", 1), ("id", "main", 2), ("silu", "main", 3)]
               + [("silu", "q", t) for t in range(2)] + [("forget", "f", t) for t in range(2)]
               + [("id", "i", t) for t in range(2)] + [("silu", "z", t) for t in range(2)]
               + [("sigmoid", "main", 4 + t) for t in range(4)])
MAIN_WIDTH = PROJ_TN * sum(1 for t in _PROJ_TILES if t[1] == "main")
_MAIN_COLS = {"qa": 0, "ka": PROJ_TN, "va": 2 * PROJ_TN, "za": 3 * PROJ_TN,
              "ga": 4 * PROJ_TN, "gb": 6 * PROJ_TN}
_QKV_TILES = 3

PERM_DIL = 16
PERM_TILE = 512
PERM_SLAB = PERM_TILE // PERM_DIL
ATTN_SUPER = PERM_DIL * ATTN_BLOCK
ATTN_BLOCKS_PER_ITER = 2
ATTN_SKEW = 2
MERGE_PIECE = 256
HGRN_CHUNK = 64
HGRN_SKEW = 1


def _sigmoid(x):
    return 1.0 / (1.0 + jnp.exp(-x))


def _any_eq(j, values):
    return functools.reduce(jnp.logical_or, [j == v for v in values])


def _mod_kernel(c_ref, w_ref, b_ref, o_ref):
    c = c_ref[...]
    sc = c * _sigmoid(c)
    o_ref[...] = jnp.dot(sc, w_ref[...], precision=lax.Precision.HIGHEST,
                         preferred_element_type=F32) + b_ref[...]


def _lower_bound_kernel(hl_ref, o_ref):
    hl = hl_ref[...]
    m = jnp.max(hl, axis=0, keepdims=True)
    e = jnp.exp(hl - m)
    o_ref[...] = e[0:1, :] / jnp.sum(e, axis=0, keepdims=True)


def _bias_table_kernel(rb_ref, bucket_ref, o_ref):
    bk = bucket_ref[...]
    for h in range(ATTN_HEADS):
        acc = jnp.full(bk.shape, NEG_INF, F32)
        for u in range(N_BUCKETS):
            acc = jnp.where(bk == u, rb_ref[u, h], acc)
        o_ref[h] = acc


def _bucket_tables():
    qi = np.arange(ATTN_BLOCK)[:, None]
    kj = np.arange(2 * ATTN_BLOCK)[None, :]
    delta = qi + ATTN_BLOCK - kj
    max_exact = N_BUCKETS // 2
    tabs = []
    for window, dilation in DILATED_PATTERNS:
        span = window // dilation
        band = (delta >= 0) & (delta <= span)
        dist = np.clip(delta, 0, None) * dilation
        n = dist.astype(np.float32)
        large = max_exact + (np.log(np.maximum(n, 1.0) / max_exact)
                             / math.log(MAX_DISTANCE / max_exact)
                             * (N_BUCKETS - max_exact)).astype(np.int32)
        large = np.minimum(large, N_BUCKETS - 1)
        bucket = np.where(dist < max_exact, dist, large)
        tab = np.where(band, bucket, -1).astype(np.int32)
        if dilation > 1:
            order = _gather_order(dilation)
            cols = np.concatenate([order, ATTN_BLOCK + order])
            tab = tab[order][:, cols]
        tabs.append(tab)
    return np.stack(tabs, 0)


def _gather_order(dilation):
    per_tile = PERM_DIL // dilation
    slab = np.arange(ATTN_BLOCK) // PERM_SLAB
    m = np.arange(ATTN_BLOCK) % PERM_SLAB
    if per_tile == 1:
        return slab * PERM_SLAB + m
    assert per_tile * PERM_SLAB == ATTN_BLOCK
    return per_tile * m + slab


def _proj_hgrn_kernel(x_ref, mod_ref, g_ref, lb_ref, w_ref, gon_ref, om_ref, op_ref, ob_ref,
                      h_ref, accl_ref, q_scr, f_scr, i_scr, z_scr, state_ref, *, tiles_per_seq):
    g = pl.program_id(0)
    tm = x_ref.shape[0]
    wr = lax.rem(g, 2)
    rd = 1 - wr

    @pl.when(g == 0)
    def _():
        for ref in (q_scr, f_scr, i_scr, z_scr):
            ref[1] = jnp.zeros(ref.shape[1:], ref.dtype)

    @pl.when(jnp.logical_or(g == 0, lax.rem(g + tiles_per_seq - 1, tiles_per_seq) == 0))
    def _():
        state_ref[...] = jnp.zeros_like(state_ref)

    x = x_ref[...]
    ms = jnp.mean(x * x, axis=-1, keepdims=True)
    y = x * lax.rsqrt(ms + EPS) * g_ref[...]
    shift = mod_ref[:, 0:D_MODEL]
    scale = mod_ref[:, D_MODEL:2 * D_MODEL]
    h_ref[...] = (y * (1.0 + scale) + shift).astype(BF16)

    n_sub = PROJ_TN // PROJ_PIECE_N
    slab = PROJ_PIECE_M // PERM_DIL

    def write_perm(mc, col0, acc, slot):
        for c in range(PROJ_PIECE_N // V7X_LANES):
            accl_ref[slot, c] = acc[:, c * V7X_LANES:(c + 1) * V7X_LANES]
            for r in range(PERM_DIL):
                rows = accl_ref[slot, c, pl.ds(r, slab, stride=PERM_DIL), :]
                row0 = r * PERM_SLAB + mc * slab
                op_ref[row0:row0 + slab, col0 + c * V7X_LANES:col0 + (c + 1) * V7X_LANES] = rows.astype(BF16)

    hgrn_dest = {"q": q_scr, "i": i_scr, "z": z_scr}

    def matmul(j, sub, mc):
        col0 = j * PROJ_TN + sub * PROJ_PIECE_N
        return jnp.dot(h_ref[mc * PROJ_PIECE_M:(mc + 1) * PROJ_PIECE_M, :],
                       w_ref[:, col0:col0 + PROJ_PIECE_N], preferred_element_type=F32)

    def epilogue(j, sub, mc, acc):
        kind, dest, tile = _PROJ_TILES[j]
        rows = slice(mc * PROJ_PIECE_M, (mc + 1) * PROJ_PIECE_M)
        col0 = tile * PROJ_TN + sub * PROJ_PIECE_N
        cols = slice(col0, col0 + PROJ_PIECE_N)
        if kind == "forget":
            lb = lb_ref[:, cols]
            f_scr[wr, rows, cols] = lb + (1.0 - lb) * _sigmoid(acc)
            return
        if kind == "qscale":
            acc = acc * (ATTN_HEAD_DIM ** -0.5)
        elif kind == "silu":
            acc = acc * _sigmoid(acc)
        elif kind == "sigmoid":
            acc = _sigmoid(acc)
        if dest != "main":
            hgrn_dest[dest][wr, rows, cols] = acc.astype(BF16)
            return
        om_ref[rows, cols] = acc.astype(BF16)
        if tile < _QKV_TILES:
            write_perm(mc, col0, acc, (sub * (tm // PROJ_PIECE_M) + mc) % accl_ref.shape[0])

    cs = _hgrn_consts(HGRN_CHUNK)
    g_on = gon_ref[...]

    def chunk_head(c, h):
        return (slice(c * HGRN_CHUNK, (c + 1) * HGRN_CHUNK),
                slice(h * HGRN_KEY_DIM, (h + 1) * HGRN_KEY_DIM))

    def scan(c, h):
        rows, hs = chunk_head(c, h)
        return _hgrn_scan(q_scr[rd, rows, hs], f_scr[rd, rows, hs], cs)

    def finish(c, h, a, q_dec, k_dec, decay):
        rows, hs = chunk_head(c, h)
        o, st = _hgrn_finish(a, q_dec, k_dec, decay, i_scr[rd, rows, hs], z_scr[rd, rows, hs],
                             state_ref[h], g_on)
        state_ref[h] = st
        ob_ref[rows, hs] = o

    pieces = [(j, sub, mc) for j in range(len(_PROJ_TILES)) for sub in range(n_sub)
              for mc in range(tm // PROJ_PIECE_M)]
    units = [(c, h) for c in range(tm // HGRN_CHUNK) for h in range(HGRN_HEADS)]
    acc_pending, scan_pending = {}, {}
    for step in range(max(len(pieces) + PROJ_SKEW, len(units) + HGRN_SKEW)):
        if step < len(pieces):
            acc_pending[step] = matmul(*pieces[step])
        if step < len(units):
            scan_pending[step] = scan(*units[step])
        if 0 <= step - PROJ_SKEW < len(pieces):
            epilogue(*pieces[step - PROJ_SKEW], acc_pending.pop(step - PROJ_SKEW))
        if 0 <= step - HGRN_SKEW < len(units):
            finish(*units[step - HGRN_SKEW], *scan_pending.pop(step - HGRN_SKEW))


def _pair_lanes(hp):
    return slice(hp * V7X_LANES, (hp + 1) * V7X_LANES)


def _attn_scores(q, k, bias_ref, hp, no_prev, lane):
    blk = q.shape[0]
    low = lane < ATTN_HEAD_DIM
    zero = jnp.zeros_like(q)
    q2 = jnp.concatenate([jnp.where(low, q, zero), jnp.where(low, zero, q)], axis=0)
    s = lax.dot_general(q2, k, (((1,), (1,)), ((), ())), preferred_element_type=F32)
    bias2 = bias_ref[pl.ds(2 * hp, 2)].reshape(2 * blk, 2 * blk)
    s = s + jnp.where(jnp.concatenate([no_prev, no_prev], axis=0), NEG_INF, bias2)
    return s, jnp.max(s, axis=-1, keepdims=True)


def _attn_values(s, m, v, hp, lane, lse_all):
    blk = s.shape[0] // 2
    p = jnp.exp(s - m).astype(BF16)
    h0, h1 = 2 * hp, 2 * hp + 1
    low = lane < ATTN_HEAD_DIM
    low_v = lax.broadcasted_iota(jnp.int32, v.shape, 1) < ATTN_HEAD_DIM
    one = jnp.ones_like(v)
    o0 = jnp.dot(p[:blk], jnp.where(low_v, v, one), preferred_element_type=F32)
    o1 = jnp.dot(p[blk:], jnp.where(low_v, one, v), preferred_element_type=F32)
    num = jnp.where(low, o0, o1)
    den_swapped = jnp.where(low, o1, o0)
    den = pltpu.roll(den_swapped, ATTN_HEAD_DIM, 1)
    is_h1 = lane == h1
    lse = jnp.where(is_h1, m[blk:], m[:blk]) + jnp.log(jnp.where(is_h1, den_swapped, den))
    lse_all = jnp.where(jnp.logical_or(lane == h0, is_h1), lse, lse_all)
    return num / den, lse_all


def _attn_pipeline(blocks, bias_ref, lane):
    n_pairs = ATTN_HEADS // 2
    items = [(bi, hp) for bi in range(len(blocks)) for hp in range(n_pairs)]
    lse = [jnp.zeros(lane.shape, F32) for _ in blocks]
    pending = {}
    for step in range(len(items) + ATTN_SKEW):
        if step < len(items):
            bi, hp = items[step]
            b = blocks[bi]
            pending[step] = _attn_scores(b.get_q(hp), b.get_k(hp), bias_ref, hp, b.no_prev, lane)
        done = step - ATTN_SKEW
        if done >= 0:
            bi, hp = items[done]
            b = blocks[bi]
            o_pair, lse[bi] = _attn_values(*pending.pop(done), b.get_v(hp), hp, lane, lse[bi])
            b.put_o(hp, o_pair)
            if hp == n_pairs - 1:
                b.put_lse(lse[bi])


def _attn_kernel(q_ref, kp_ref, kc_ref, vp_ref, vc_ref, bias_ref, o_ref, lse_ref, *scratch, dil):
    n = pl.program_id(1)
    blk_rows = ATTN_BLOCK
    lane = lax.broadcasted_iota(jnp.int32, (blk_rows, V7X_LANES), 1)
    col = lax.broadcasted_iota(jnp.int32, (blk_rows, 2 * blk_rows), 1)
    prev_rows = kp_ref.shape[0]
    scratch = list(scratch)
    o_scr = scratch.pop(0) if dil > 1 else None
    if scratch:
        kext, vext = scratch
        kext[0:prev_rows] = kp_ref[...]
        kext[prev_rows:] = kc_ref[...]
        vext[0:prev_rows] = vp_ref[...]
        vext[prev_rows:] = vc_ref[...]
        k_prev, k_cur, v_prev, v_cur, cur_off = kext, kext, vext, vext, prev_rows
    else:
        k_prev, k_cur, v_prev, v_cur, cur_off = kp_ref, kc_ref, vp_ref, vc_ref, 0

    def make_block(blk):
        if dil == 1:
            slab = blk_rows
            offs = [blk * blk_rows]
            first = jnp.logical_and(n == 0, blk == 0)
        elif dil == PERM_DIL:
            slab = PERM_SLAB
            offs = [t * PERM_TILE + blk * PERM_SLAB for t in range(blk_rows // slab)]
            tok0 = [t * PERM_TILE + blk for t in range(blk_rows // slab)]
            first = n == 0
        else:
            slab = PERM_SLAB
            tile = lax.shift_right_logical(blk, 2)
            res = jnp.bitwise_and(blk, dil - 1)
            offs = [tile * PERM_TILE + (res + dil * j) * PERM_SLAB for j in range(blk_rows // slab)]
            tok0 = [tile * PERM_TILE + res + dil * j for j in range(blk_rows // slab)]
            first = jnp.logical_and(n == 0, tile == 0)
        offs = [pl.multiple_of(o, slab) for o in offs]

        def slabs(ref, hp, shift=0, rows=slab):
            return [ref[pl.ds(o + shift, rows), _pair_lanes(hp)] for o in offs]

        def put_o(hp, o_pair):
            if dil == 1:
                o_ref[pl.ds(offs[0], slab), _pair_lanes(hp)] = o_pair.astype(BF16)
                return
            for j, t0 in enumerate(tok0):
                o_scr[hp, pl.ds(t0, slab, stride=PERM_DIL), :] = o_pair[j * slab:(j + 1) * slab]

        def put_lse(lse_all):
            if dil == 1:
                lse_ref[pl.ds(offs[0], slab), :] = lse_all
                return
            for j, t0 in enumerate(tok0):
                lse_ref[pl.ds(t0, slab, stride=PERM_DIL), :] = lse_all[j * slab:(j + 1) * slab]

        if dil == 1:
            get_k = lambda hp: slabs(k_cur, hp, rows=2 * slab)[0]
            get_v = lambda hp: slabs(v_cur, hp, rows=2 * slab)[0]
        else:
            get_k = lambda hp: jnp.concatenate(slabs(k_prev, hp) + slabs(k_cur, hp, cur_off), axis=0)
            get_v = lambda hp: jnp.concatenate(slabs(v_prev, hp) + slabs(v_cur, hp, cur_off), axis=0)
        return types.SimpleNamespace(
            get_q=lambda hp: jnp.concatenate(slabs(q_ref, hp), axis=0), get_k=get_k, get_v=get_v,
            put_o=put_o, put_lse=put_lse, no_prev=jnp.logical_and(col < blk_rows, first))

    def body(it, carry):
        _attn_pipeline([make_block(it * ATTN_BLOCKS_PER_ITER + u) for u in range(ATTN_BLOCKS_PER_ITER)],
                       bias_ref, lane)
        return carry

    lax.fori_loop(0, ATTN_SUPER // ATTN_BLOCK // ATTN_BLOCKS_PER_ITER, body, 0)
    if dil > 1:
        for hp in range(ATTN_HEADS // 2):
            o_ref[:, _pair_lanes(hp)] = o_scr[hp].astype(BF16)


def _roll_rows(a, shift):
    rows = a.shape[0]
    if abs(shift) >= V7X_SUBLANES:
        return pltpu.roll(a, shift % rows, 0)
    grouped = a.reshape(rows // V7X_SUBLANES, V7X_SUBLANES, a.shape[1])
    return pltpu.roll(grouped, shift % V7X_SUBLANES, 1).reshape(a.shape)


def _hgrn_consts(chunk):
    n_levels = chunk.bit_length() - 1
    row = lax.broadcasted_iota(jnp.int32, (chunk, HGRN_KEY_DIM), 0)
    tt = lax.broadcasted_iota(jnp.int32, (chunk, chunk), 0)
    ss = lax.broadcasted_iota(jnp.int32, (chunk, chunk), 1)
    owner = jnp.where(tt > ss, 32 - lax.clz(jnp.bitwise_xor(tt, ss)), jnp.where(tt == ss, 0, -1))
    return types.SimpleNamespace(
        n_levels=n_levels,
        odds=[jnp.bitwise_and(row, 1 << lvl) != 0 for lvl in range(n_levels)],
        owned=[owner == lvl for lvl in range(n_levels + 1)])


_NT = (((1,), (1,)), ((), ()))


def _hgrn_scan(q, f, cs):
    k = (1.0 - f).astype(BF16)
    a = jnp.where(cs.owned[0], lax.dot_general(q, k, _NT, preferred_element_type=F32), 0.0)
    x = jnp.where(cs.odds[0], f, 1.0)
    y = jnp.where(cs.odds[0], 1.0, f)
    for lvl in range(cs.n_levels):
        m = 1 << lvl
        e = x.astype(BF16)
        pm = lax.dot_general(q * e, k * e, _NT, preferred_element_type=F32)
        a = jnp.where(cs.owned[lvl + 1], pm, a)
        tot = x * y if lvl else f
        if 2 * m == V7X_SUBLANES:
            partner = _roll_rows(tot, m)
        else:
            partner = jnp.where(cs.odds[lvl], _roll_rows(tot, m), _roll_rows(tot, -m))
        z = x * partner
        keep = (cs.odds[lvl] == cs.odds[lvl + 1] if lvl + 1 < cs.n_levels
                else jnp.logical_not(cs.odds[lvl]))
        x, y = jnp.where(keep, z, y), jnp.where(keep, y, z)
    return a.astype(BF16), q * y.astype(BF16), k * x.astype(BF16), x[0:1, :] * y[0:1, :]


def _hgrn_finish(a, q_dec, k_dec, decay, v, zgate, st, g_on):
    o = (jnp.dot(a, v, preferred_element_type=F32)
         + lax.dot_general(q_dec, st.astype(BF16), _NT, preferred_element_type=F32))
    upd = lax.dot_general(v, k_dec, (((0,), (0,)), ((), ())), preferred_element_type=F32)
    ms = jnp.mean(o * o, axis=-1, keepdims=True)
    y = o * lax.rsqrt(ms + EPS) * g_on
    return (y * zgate.astype(F32)).astype(BF16), st * decay + upd


def _merge_kernel(o1_ref, o2_ref, o3_ref, l1_ref, l2_ref, l3_ref, za_ref, ob_ref,
                  sga_ref, sgb_ref, x_ref, mod_ref, wa_ref, wb_ref, wo_ref, fg_ref,
                  ex_ref, out_ref):
    ex = ex_ref[...]
    gate = mod_ref[:, 2 * D_MODEL:3 * D_MODEL]

    def expand(w):
        hi = w.astype(BF16)
        lo = (w - hi.astype(F32)).astype(BF16)
        return jnp.dot(jnp.concatenate([hi, lo], axis=1), ex, preferred_element_type=F32)

    def mix(rows):
        l1, l2, l3 = l1_ref[rows, :], l2_ref[rows, :], l3_ref[rows, :]
        mx = jnp.maximum(jnp.maximum(l1, l2), l3)
        e1, e2, e3 = jnp.exp(l1 - mx), jnp.exp(l2 - mx), jnp.exp(l3 - mx)
        inv = 1.0 / (e1 + e2 + e3)
        oa = (expand(e1 * inv) * o1_ref[rows, :].astype(F32)
              + expand(e2 * inv) * o2_ref[rows, :].astype(F32)
              + expand(e3 * inv) * o3_ref[rows, :].astype(F32))
        return (oa * za_ref[rows, :].astype(F32)).astype(BF16)

    def branches(rows, oa):
        ya = jnp.dot(oa, wa_ref[...], preferred_element_type=F32)
        yb = jnp.dot(ob_ref[rows, :], wb_ref[...], preferred_element_type=F32)
        return (sga_ref[rows, :].astype(F32) * ya + sgb_ref[rows, :].astype(F32) * yb).astype(BF16)

    def project(rows, y):
        z = jnp.dot(y, wo_ref[...], preferred_element_type=F32)
        xo = x_ref[rows, :] + gate * z
        ms = jnp.mean(xo * xo, axis=-1, keepdims=True)
        out_ref[rows, :] = xo * lax.rsqrt(ms + EPS) * fg_ref[...]

    pieces = [slice(r, r + MERGE_PIECE) for r in range(0, x_ref.shape[0], MERGE_PIECE)]
    oas = [mix(rows) for rows in pieces]
    ys = [branches(rows, oa) for rows, oa in zip(pieces, oas)]
    for rows, y in zip(pieces, ys):
        project(rows, y)


def _cparams(sem, vmem_mb):
    return pltpu.CompilerParams(dimension_semantics=sem,
                                vmem_limit_bytes=vmem_mb * 1024 * 1024)


def kernel(x, c, w_ada, b_ada, norm_g, w_in, hgrn_onorm_g, w_branch_a, w_branch_b, w_out,
           rel_bias, hgrn_lb, final_g):
    B, S, D = x.shape
    assert D == D_MODEL and w_ada.shape[0] == 1, "single-layer kernel"
    N = B * S
    x2 = x.reshape(N, D)

    c8 = jnp.pad(c, ((0, 8 - B), (0, 0)))
    mod = pl.pallas_call(
        _mod_kernel,
        grid=(3 * D // 512,),
        in_specs=[pl.BlockSpec((8, D), lambda j: (0, 0)),
                  pl.BlockSpec((D, 512), lambda j: (0, j)),
                  pl.BlockSpec((1, 512), lambda j: (0, j))],
        out_specs=pl.BlockSpec((8, 512), lambda j: (0, j)),
        out_shape=jax.ShapeDtypeStruct((8, 3 * D), F32),
        name="adaln_mod",
    )(c8, w_ada[0], b_ada[0].reshape(1, 3 * D))
    mod3 = mod.reshape(8, 1, 3 * D)

    lb = pl.pallas_call(
        _lower_bound_kernel,
        out_shape=jax.ShapeDtypeStruct((1, HGRN_WIDTH), F32),
        name="hgrn_lower_bound",
    )(hgrn_lb)

    n_pat = len(DILATED_PATTERNS)
    bias_tab = pl.pallas_call(
        _bias_table_kernel,
        grid=(n_pat,),
        in_specs=[pl.BlockSpec(memory_space=pltpu.SMEM),
                  pl.BlockSpec((None, ATTN_BLOCK, 2 * ATTN_BLOCK), lambda g: (g, 0, 0))],
        out_specs=pl.BlockSpec((None, ATTN_HEADS, ATTN_BLOCK, 2 * ATTN_BLOCK),
                               lambda g: (g, 0, 0, 0)),
        out_shape=jax.ShapeDtypeStruct((n_pat, ATTN_HEADS, ATTN_BLOCK, 2 * ATTN_BLOCK), F32),
        name="rel_bias_table",
    )(rel_bias, jnp.asarray(_bucket_tables()))

    tm = PERM_TILE
    tiles_per_b = S // tm
    last = N // tm - 1
    in_width = len(_PROJ_TILES) * PROJ_TN
    assert PROJ_TN == ATTN_WIDTH and w_in.shape[2] == in_width
    resident = dict(pipeline_mode=pl.Buffered(1))
    proj_tile = lambda g: (jnp.minimum(g, last), 0)
    main, qkv_p, ob = pl.pallas_call(
        functools.partial(_proj_hgrn_kernel, tiles_per_seq=tiles_per_b),
        grid=(N // tm + 1,),
        in_specs=[pl.BlockSpec((tm, D), proj_tile),
                  pl.BlockSpec((None, 1, 3 * D), lambda g: (jnp.minimum(g, last) // tiles_per_b, 0, 0)),
                  pl.BlockSpec((1, D), lambda g: (0, 0)),
                  pl.BlockSpec((1, HGRN_WIDTH), lambda g: (0, 0)),
                  pl.BlockSpec((D, in_width), lambda g: (0, 0), **resident),
                  pl.BlockSpec((1, HGRN_VAL_DIM), lambda g: (0, 0))],
        out_specs=[pl.BlockSpec((tm, MAIN_WIDTH), proj_tile),
                   pl.BlockSpec((tm, _QKV_TILES * ATTN_WIDTH), proj_tile),
                   pl.BlockSpec((tm, HGRN_WIDTH), lambda g: (jnp.maximum(g - 1, 0), 0))],
        out_shape=[jax.ShapeDtypeStruct((N, MAIN_WIDTH), BF16),
                   jax.ShapeDtypeStruct((N, _QKV_TILES * ATTN_WIDTH), BF16),
                   jax.ShapeDtypeStruct((N, HGRN_WIDTH), BF16)],
        scratch_shapes=[pltpu.VMEM((tm, D), BF16),
                        pltpu.VMEM((2, PROJ_PIECE_N // V7X_LANES, PROJ_PIECE_M, V7X_LANES), F32),
                        pltpu.VMEM((2, tm, HGRN_WIDTH), BF16), pltpu.VMEM((2, tm, HGRN_WIDTH), F32),
                        pltpu.VMEM((2, tm, HGRN_WIDTH), BF16), pltpu.VMEM((2, tm, HGRN_WIDTH), BF16),
                        pltpu.VMEM((HGRN_HEADS, HGRN_VAL_DIM, HGRN_KEY_DIM), F32)],
        compiler_params=_cparams(("arbitrary",), 56),
        name="inproj_hgrn2",
    )(x2, mod3, norm_g[0].reshape(1, D), lb, w_in[0].astype(BF16),
      hgrn_onorm_g[0].reshape(1, HGRN_VAL_DIM))

    qa_t = _MAIN_COLS["qa"] // ATTN_WIDTH
    ka_t = _MAIN_COLS["ka"] // ATTN_WIDTH
    va_t = _MAIN_COLS["va"] // ATTN_WIDTH
    attn_outs = []
    main_b = main.reshape(B, S, MAIN_WIDTH)
    qkv_pv = qkv_p.reshape(B, S, _QKV_TILES * ATTN_WIDTH)
    n_super = S // ATTN_SUPER
    for g, (window, dil) in enumerate(DILATED_PATTERNS):
        assert window // dil == ATTN_BLOCK
        src, tiles = (main_b, (qa_t, ka_t, va_t)) if dil == 1 else (qkv_pv, (0, 1, 2))
        prev_rows = {1: ATTN_BLOCK, PERM_DIL: ATTN_SUPER}.get(dil, PERM_TILE)
        per_step = ATTN_SUPER // prev_rows

        def cur_spec(t):
            return pl.BlockSpec((None, ATTN_SUPER, ATTN_WIDTH), lambda b, n, t=t: (b, n, t))

        def prev_spec(t, prev_rows=prev_rows, per_step=per_step):
            return pl.BlockSpec((None, prev_rows, ATTN_WIDTH),
                                lambda b, n, t=t: (b, jnp.maximum(n * per_step - 1, 0), t))

        scratch = []
        if dil > 1:
            scratch += [pltpu.VMEM((ATTN_WIDTH // V7X_LANES, ATTN_SUPER, V7X_LANES), F32)]
        if dil != PERM_DIL:
            scratch += [pltpu.VMEM((prev_rows + ATTN_SUPER, ATTN_WIDTH), BF16)] * 2
        o_g, lse_g = pl.pallas_call(
            functools.partial(_attn_kernel, dil=dil),
            grid=(B, n_super),
            in_specs=[cur_spec(tiles[0]), prev_spec(tiles[1]), cur_spec(tiles[1]),
                      prev_spec(tiles[2]), cur_spec(tiles[2]),
                      pl.BlockSpec((None, ATTN_HEADS, ATTN_BLOCK, 2 * ATTN_BLOCK),
                                   lambda b, n, g=g: (g, 0, 0, 0))],
            out_specs=[pl.BlockSpec((None, ATTN_SUPER, ATTN_WIDTH), lambda b, n: (b, n, 0)),
                       pl.BlockSpec((None, ATTN_SUPER, V7X_LANES), lambda b, n: (b, n, 0))],
            out_shape=[jax.ShapeDtypeStruct((B, S, ATTN_WIDTH), BF16),
                       jax.ShapeDtypeStruct((B, S, V7X_LANES), F32)],
            scratch_shapes=scratch,
            compiler_params=_cparams(("arbitrary", "arbitrary"), 48),
            name=f"dilated_attn_d{dil}",
        )(src, src, src, src, src, bias_tab)
        attn_outs.append((o_g.reshape(N, ATTN_WIDTH), lse_g.reshape(N, V7X_LANES)))

    tk = 2 * MERGE_PIECE
    tiles_per_b5 = S // tk
    za_t = _MAIN_COLS["za"] // ATTN_WIDTH
    ga_t = _MAIN_COLS["ga"] // D
    gb_t = _MAIN_COLS["gb"] // D
    expand_mat = np.zeros((2 * V7X_LANES, ATTN_WIDTH), np.float32)
    for h in range(ATTN_HEADS):
        expand_mat[h, h * ATTN_HEAD_DIM:(h + 1) * ATTN_HEAD_DIM] = 1.0
        expand_mat[V7X_LANES + h, h * ATTN_HEAD_DIM:(h + 1) * ATTN_HEAD_DIM] = 1.0
    (o1, l1), (o2, l2), (o3, l3) = attn_outs
    row_spec = lambda w, t=0: pl.BlockSpec((tk, w), lambda i, t=t: (i, t))
    full_spec = lambda a, b: pl.BlockSpec((a, b), lambda i: (0, 0))
    out = pl.pallas_call(
        _merge_kernel,
        grid=(N // tk,),
        in_specs=[row_spec(ATTN_WIDTH), row_spec(ATTN_WIDTH), row_spec(ATTN_WIDTH),
                  row_spec(V7X_LANES), row_spec(V7X_LANES), row_spec(V7X_LANES),
                  row_spec(ATTN_WIDTH, za_t), row_spec(HGRN_WIDTH),
                  row_spec(D, ga_t), row_spec(D, gb_t), row_spec(D),
                  pl.BlockSpec((None, 1, 3 * D), lambda i: (i // tiles_per_b5, 0, 0)),
                  full_spec(ATTN_WIDTH, D), full_spec(HGRN_WIDTH, D), full_spec(D, D),
                  full_spec(1, D), full_spec(2 * V7X_LANES, ATTN_WIDTH)],
        out_specs=pl.BlockSpec((tk, D), lambda i: (i, 0)),
        out_shape=jax.ShapeDtypeStruct((N, D), F32),
        compiler_params=_cparams(("arbitrary",), 40),
        name="gated_merge",
    )(o1, o2, o3, l1, l2, l3, main, ob, main, main, x2, mod3,
      w_branch_a[0].astype(BF16), w_branch_b[0].astype(BF16), w_out[0].astype(BF16),
      final_g.reshape(1, D), jnp.asarray(expand_mat, BF16))
    return out.reshape(B, S, D)
```

```python
import functools
import math
import types

import numpy as np
import jax
import jax.numpy as jnp
from jax import lax
from jax.experimental import pallas as pl
from jax.experimental.pallas import tpu as pltpu

D_MODEL = 1024
ATTN_HEADS = 8
ATTN_HEAD_DIM = 64
ATTN_WIDTH = ATTN_HEADS * ATTN_HEAD_DIM
DILATED_PATTERNS = ((128, 1), (512, 4), (2048, 16))
ATTN_BLOCK = 128
N_BUCKETS = 32
MAX_DISTANCE = 2048
NEG_INF = -1e30
HGRN_HEADS = 8
HGRN_KEY_DIM = 128
HGRN_VAL_DIM = 128
HGRN_WIDTH = HGRN_HEADS * HGRN_VAL_DIM
EPS = 1e-6

V7X_LANES = 128
V7X_SUBLANES = 8

F32 = jnp.float32
BF16 = jnp.bfloat16

_MAIN_COLS = {}
_off = 0
for _name, _w in (("qa", ATTN_WIDTH), ("ka", ATTN_WIDTH), ("va", ATTN_WIDTH), ("za", ATTN_WIDTH),
                  ("qb", HGRN_WIDTH), ("ib", HGRN_WIDTH), ("zb", HGRN_WIDTH),
                  ("ga", D_MODEL), ("gb", D_MODEL)):
    _MAIN_COLS[_name] = (_off, _w)
    _off += _w
MAIN_WIDTH = _off
PROJ_TN = 512
PROJ_PIECE_M = 256
PROJ_PIECE_N = 256
PROJ_SKEW = 2
_PROJ_KINDS = (["qscale"] + ["kv"] * 2 + ["silu"] + ["silu"] * 2 + ["forget"] * 2
               + ["id"] * 2 + ["silu"] * 2 + ["sigmoid"] * 4)
_F_TILE0 = _PROJ_KINDS.index("forget")
_F_TILES = _PROJ_KINDS.count("forget")
_QKV_TILES = 3

PERM_DIL = 16
PERM_TILE = 512
PERM_SLAB = PERM_TILE // PERM_DIL
ATTN_SUPER = PERM_DIL * ATTN_BLOCK
ATTN_BLOCKS_PER_ITER = 2
ATTN_SKEW = 2
MERGE_PIECE = 256
HGRN_CHUNKS_PER_ITER = 2
HGRN_SKEW = 1
LOG2E = math.log2(math.e)
LN2 = math.log(2.0)


def _sigmoid(x):
    return 1.0 / (1.0 + jnp.exp(-x))


def _mod_kernel(c_ref, w_ref, b_ref, o_ref):
    c = c_ref[...]
    sc = c * _sigmoid(c)
    o_ref[...] = jnp.dot(sc, w_ref[...], precision=lax.Precision.HIGHEST,
                         preferred_element_type=F32) + b_ref[...]


def _lower_bound_kernel(hl_ref, o_ref):
    hl = hl_ref[...]
    m = jnp.max(hl, axis=0, keepdims=True)
    e = jnp.exp(hl - m)
    o_ref[...] = e[0:1, :] / jnp.sum(e, axis=0, keepdims=True)


def _bias_table_kernel(rb_ref, bucket_ref, o_ref):
    bk = bucket_ref[...]
    no_prev = lax.broadcasted_iota(jnp.int32, bk.shape, 1) < ATTN_BLOCK
    for h in range(ATTN_HEADS):
        acc = jnp.full(bk.shape, NEG_INF, F32)
        for u in range(N_BUCKETS):
            acc = jnp.where(bk == u, rb_ref[u, h] * LOG2E, acc)
        o_ref[0, h] = acc
        o_ref[1, h] = jnp.where(no_prev, NEG_INF, acc)


def _bucket_tables():
    qi = np.arange(ATTN_BLOCK)[:, None]
    kj = np.arange(2 * ATTN_BLOCK)[None, :]
    delta = qi + ATTN_BLOCK - kj
    max_exact = N_BUCKETS // 2
    tabs = []
    for window, dilation in DILATED_PATTERNS:
        span = window // dilation
        band = (delta >= 0) & (delta <= span)
        dist = np.clip(delta, 0, None) * dilation
        n = dist.astype(np.float32)
        large = max_exact + (np.log(np.maximum(n, 1.0) / max_exact)
                             / math.log(MAX_DISTANCE / max_exact)
                             * (N_BUCKETS - max_exact)).astype(np.int32)
        large = np.minimum(large, N_BUCKETS - 1)
        bucket = np.where(dist < max_exact, dist, large)
        tab = np.where(band, bucket, -1).astype(np.int32)
        if dilation > 1:
            order = _gather_order(dilation)
            cols = np.concatenate([order, ATTN_BLOCK + order])
            tab = tab[order][:, cols]
        tabs.append(tab)
    return np.stack(tabs, 0)


def _gather_order(dilation):
    per_tile = PERM_DIL // dilation
    slab = np.arange(ATTN_BLOCK) // PERM_SLAB
    m = np.arange(ATTN_BLOCK) % PERM_SLAB
    if per_tile == 1:
        return slab * PERM_SLAB + m
    assert per_tile * PERM_SLAB == ATTN_BLOCK
    return per_tile * m + slab


def _inproj_kernel(x_ref, mod_ref, g_ref, lb_ref, w_ref, om_ref, of_ref, op_ref, h_ref, accl_ref):
    tm = x_ref.shape[0]
    x = x_ref[...]
    ms = jnp.mean(x * x, axis=-1, keepdims=True)
    y = x * lax.rsqrt(ms + EPS) * g_ref[...]
    shift = mod_ref[:, 0:D_MODEL]
    scale = mod_ref[:, D_MODEL:2 * D_MODEL]
    h_ref[...] = (y * (1.0 + scale) + shift).astype(BF16)

    n_sub = PROJ_TN // PROJ_PIECE_N
    slab = PROJ_PIECE_M // PERM_DIL

    def write_perm(mc, col0, acc, slot):
        for c in range(PROJ_PIECE_N // V7X_LANES):
            accl_ref[slot, c] = acc[:, c * V7X_LANES:(c + 1) * V7X_LANES]
            for r in range(PERM_DIL):
                rows = accl_ref[slot, c, pl.ds(r, slab, stride=PERM_DIL), :]
                row0 = r * PERM_SLAB + mc * slab
                op_ref[row0:row0 + slab, col0 + c * V7X_LANES:col0 + (c + 1) * V7X_LANES] = rows.astype(BF16)

    def epilogue(j, sub, mc, acc):
        kind = _PROJ_KINDS[j]
        rows = slice(mc * PROJ_PIECE_M, (mc + 1) * PROJ_PIECE_M)
        if kind == "forget":
            col0 = (j - _F_TILE0) * PROJ_TN + sub * PROJ_PIECE_N
            lb = lb_ref[:, col0:col0 + PROJ_PIECE_N]
            of_ref[rows, col0:col0 + PROJ_PIECE_N] = lb + (1.0 - lb) * _sigmoid(acc)
            return
        if kind == "qscale":
            acc = acc * (ATTN_HEAD_DIM ** -0.5 * LOG2E)
        elif kind == "silu":
            acc = acc * _sigmoid(acc)
        elif kind == "sigmoid":
            acc = _sigmoid(acc)
        jm = j if j < _F_TILE0 else j - _F_TILES
        col0 = jm * PROJ_TN + sub * PROJ_PIECE_N
        om_ref[rows, col0:col0 + PROJ_PIECE_N] = acc.astype(BF16)
        if j < _QKV_TILES:
            write_perm(mc, col0, acc, (sub * (tm // PROJ_PIECE_M) + mc) % accl_ref.shape[0])

    items = [(j, sub, mc) for j in range(len(_PROJ_KINDS)) for sub in range(n_sub)
             for mc in range(tm // PROJ_PIECE_M)]
    pending = {}
    for step in range(len(items) + PROJ_SKEW):
        if step < len(items):
            j, sub, mc = items[step]
            col0 = j * PROJ_TN + sub * PROJ_PIECE_N
            pending[step] = jnp.dot(h_ref[mc * PROJ_PIECE_M:(mc + 1) * PROJ_PIECE_M, :],
                                    w_ref[:, col0:col0 + PROJ_PIECE_N], preferred_element_type=F32)
        done = step - PROJ_SKEW
        if done >= 0:
            epilogue(*items[done], pending.pop(done))


def _pair_lanes(hp):
    return slice(hp * V7X_LANES, (hp + 1) * V7X_LANES)


def _attn_scores(q, k, bias_ref, hp, first, lane):
    blk = q.shape[0]
    low = lane < ATTN_HEAD_DIM
    zero = jnp.zeros_like(q)
    q2 = jnp.concatenate([jnp.where(low, q, zero), jnp.where(low, zero, q)], axis=0)
    s = lax.dot_general(q2, k, (((1,), (1,)), ((), ())), preferred_element_type=F32)
    s = s + bias_ref[first, pl.ds(2 * hp, 2)].reshape(2 * blk, 2 * blk)
    return s, jnp.max(s, axis=-1, keepdims=True)


def _attn_values(s, m, v, hp, lane, lse_all):
    blk = s.shape[0] // 2
    p = jnp.exp2(s - m).astype(BF16)
    h0, h1 = 2 * hp, 2 * hp + 1
    low = lane < ATTN_HEAD_DIM
    low_v = lax.broadcasted_iota(jnp.int32, v.shape, 1) < ATTN_HEAD_DIM
    one = jnp.ones_like(v)
    o0 = jnp.dot(p[:blk], jnp.where(low_v, v, one), preferred_element_type=F32)
    o1 = jnp.dot(p[blk:], jnp.where(low_v, one, v), preferred_element_type=F32)
    num = jnp.where(low, o0, o1)
    den_swapped = jnp.where(low, o1, o0)
    den = pltpu.roll(den_swapped, ATTN_HEAD_DIM, 1)
    is_h1 = lane == h1
    lse = jnp.where(is_h1, m[blk:], m[:blk]) * LN2 + jnp.log(jnp.where(is_h1, den_swapped, den))
    lse_all = jnp.where(jnp.logical_or(lane == h0, is_h1), lse, lse_all)
    return num / den, lse_all


def _attn_pipeline(blocks, bias_ref, lane):
    n_pairs = ATTN_HEADS // 2
    items = [(bi, hp) for bi in range(len(blocks)) for hp in range(n_pairs)]
    lse = [jnp.zeros(lane.shape, F32) for _ in blocks]
    pending = {}
    for step in range(len(items) + ATTN_SKEW):
        if step < len(items):
            bi, hp = items[step]
            b = blocks[bi]
            pending[step] = _attn_scores(b.get_q(hp), b.get_k(hp), bias_ref, hp, b.first, lane)
        done = step - ATTN_SKEW
        if done >= 0:
            bi, hp = items[done]
            b = blocks[bi]
            o_pair, lse[bi] = _attn_values(*pending.pop(done), b.get_v(hp), hp, lane, lse[bi])
            b.put_o(hp, o_pair)
            if hp == n_pairs - 1:
                b.put_lse(lse[bi])


def _attn_kernel(q_ref, kp_ref, kc_ref, vp_ref, vc_ref, bias_ref, o_ref, lse_ref, *scratch, dil):
    n = pl.program_id(1)
    blk_rows = ATTN_BLOCK
    lane = lax.broadcasted_iota(jnp.int32, (blk_rows, V7X_LANES), 1)
    prev_rows = kp_ref.shape[0]
    scratch = list(scratch)
    o_scr = scratch.pop(0) if dil > 1 else None
    if scratch:
        kext, vext = scratch
        kext[0:prev_rows] = kp_ref[...]
        kext[prev_rows:] = kc_ref[...]
        vext[0:prev_rows] = vp_ref[...]
        vext[prev_rows:] = vc_ref[...]
        k_prev, k_cur, v_prev, v_cur, cur_off = kext, kext, vext, vext, prev_rows
    else:
        k_prev, k_cur, v_prev, v_cur, cur_off = kp_ref, kc_ref, vp_ref, vc_ref, 0

    def make_block(blk):
        if dil == 1:
            slab = blk_rows
            offs = [blk * blk_rows]
            first = jnp.logical_and(n == 0, blk == 0)
        elif dil == PERM_DIL:
            slab = PERM_SLAB
            offs = [t * PERM_TILE + blk * PERM_SLAB for t in range(blk_rows // slab)]
            tok0 = [t * PERM_TILE + blk for t in range(blk_rows // slab)]
            first = n == 0
        else:
            slab = PERM_SLAB
            tile = lax.shift_right_logical(blk, 2)
            res = jnp.bitwise_and(blk, dil - 1)
            offs = [tile * PERM_TILE + (res + dil * j) * PERM_SLAB for j in range(blk_rows // slab)]
            tok0 = [tile * PERM_TILE + res + dil * j for j in range(blk_rows // slab)]
            first = jnp.logical_and(n == 0, tile == 0)
        offs = [pl.multiple_of(o, slab) for o in offs]

        def slabs(ref, hp, shift=0, rows=slab):
            return [ref[pl.ds(o + shift, rows), _pair_lanes(hp)] for o in offs]

        def put_o(hp, o_pair):
            if dil == 1:
                o_ref[pl.ds(offs[0], slab), _pair_lanes(hp)] = o_pair.astype(BF16)
                return
            for j, t0 in enumerate(tok0):
                o_scr[hp, pl.ds(t0, slab, stride=PERM_DIL), :] = o_pair[j * slab:(j + 1) * slab]

        def put_lse(lse_all):
            if dil == 1:
                lse_ref[pl.ds(offs[0], slab), :] = lse_all
                return
            for j, t0 in enumerate(tok0):
                lse_ref[pl.ds(t0, slab, stride=PERM_DIL), :] = lse_all[j * slab:(j + 1) * slab]

        if dil == 1:
            get_k = lambda hp: slabs(k_cur, hp, rows=2 * slab)[0]
            get_v = lambda hp: slabs(v_cur, hp, rows=2 * slab)[0]
        else:
            get_k = lambda hp: jnp.concatenate(slabs(k_prev, hp) + slabs(k_cur, hp, cur_off), axis=0)
            get_v = lambda hp: jnp.concatenate(slabs(v_prev, hp) + slabs(v_cur, hp, cur_off), axis=0)
        return types.SimpleNamespace(
            get_q=lambda hp: jnp.concatenate(slabs(q_ref, hp), axis=0), get_k=get_k, get_v=get_v,
            put_o=put_o, put_lse=put_lse, first=first.astype(jnp.int32))

    def body(it, carry):
        _attn_pipeline([make_block(it * ATTN_BLOCKS_PER_ITER + u) for u in range(ATTN_BLOCKS_PER_ITER)],
                       bias_ref, lane)
        return carry

    lax.fori_loop(0, ATTN_SUPER // ATTN_BLOCK // ATTN_BLOCKS_PER_ITER, body, 0)
    if dil > 1:
        for hp in range(ATTN_HEADS // 2):
            o_ref[:, _pair_lanes(hp)] = o_scr[hp].astype(BF16)


def _roll_rows(a, shift):
    rows = a.shape[0]
    if abs(shift) >= V7X_SUBLANES:
        return pltpu.roll(a, shift % rows, 0)
    grouped = a.reshape(rows // V7X_SUBLANES, V7X_SUBLANES, a.shape[1])
    return pltpu.roll(grouped, shift % V7X_SUBLANES, 1).reshape(a.shape)


def _hgrn_kernel(q_ref, f_ref, i_ref, z_ref, g_ref, o_ref, state_ref, *, chunk, n_chunks):
    @pl.when(pl.program_id(1) == 0)
    def _():
        state_ref[...] = jnp.zeros_like(state_ref)

    C = chunk
    n_levels = C.bit_length() - 1
    g_on = g_ref[...]
    nt = (((1,), (1,)), ((), ()))

    def chunk_body(ci, carry):
        row = lax.broadcasted_iota(jnp.int32, (C, HGRN_KEY_DIM), 0)
        odds = [jnp.bitwise_and(row, 1 << lvl) != 0 for lvl in range(n_levels)]
        tt = lax.broadcasted_iota(jnp.int32, (C, C), 0)
        ss = lax.broadcasted_iota(jnp.int32, (C, C), 1)
        owner = jnp.where(tt > ss, 32 - lax.clz(jnp.bitwise_xor(tt, ss)),
                          jnp.where(tt == ss, 0, -1))
        owned = [owner == lvl for lvl in range(n_levels + 1)]

        def chunk_rows(u):
            return pl.ds(pl.multiple_of((ci * HGRN_CHUNKS_PER_ITER + u) * C, C), C)

        def scan(u, h):
            rows = chunk_rows(u)
            hs = slice(h * HGRN_KEY_DIM, (h + 1) * HGRN_KEY_DIM)
            f = f_ref[rows, hs]
            q = q_ref[rows, hs]
            k = (1.0 - f).astype(BF16)
            a = jnp.where(owned[0], lax.dot_general(q, k, nt, preferred_element_type=F32), 0.0)
            x = jnp.where(odds[0], f, 1.0)
            y = jnp.where(odds[0], 1.0, f)
            for lvl in range(n_levels):
                m = 1 << lvl
                e = x.astype(BF16)
                pm = lax.dot_general(q * e, k * e, nt, preferred_element_type=F32)
                a = jnp.where(owned[lvl + 1], pm, a)
                tot = x * y if lvl else f
                if 2 * m == V7X_SUBLANES:
                    partner = _roll_rows(tot, m)
                else:
                    partner = jnp.where(odds[lvl], _roll_rows(tot, m), _roll_rows(tot, -m))
                z = x * partner
                keep = odds[lvl] == odds[lvl + 1] if lvl + 1 < n_levels else jnp.logical_not(odds[lvl])
                x, y = jnp.where(keep, z, y), jnp.where(keep, y, z)
            return a.astype(BF16), q * y.astype(BF16), k * x.astype(BF16), x[0:1, :] * y[0:1, :]

        def finish(u, h, a, q_dec, k_dec, decay):
            rows = chunk_rows(u)
            hs = slice(h * HGRN_KEY_DIM, (h + 1) * HGRN_KEY_DIM)
            v = i_ref[rows, hs]
            st = state_ref[h]
            o = (jnp.dot(a, v, preferred_element_type=F32)
                 + lax.dot_general(q_dec, st.astype(BF16), nt, preferred_element_type=F32))
            upd = lax.dot_general(v, k_dec, (((0,), (0,)), ((), ())), preferred_element_type=F32)
            state_ref[h] = st * decay + upd
            ms = jnp.mean(o * o, axis=-1, keepdims=True)
            y = o * lax.rsqrt(ms + EPS) * g_on
            o_ref[rows, hs] = (y * z_ref[rows, hs].astype(F32)).astype(BF16)

        items = [(u, h) for u in range(HGRN_CHUNKS_PER_ITER) for h in range(HGRN_HEADS)]
        pending = {}
        for step in range(len(items) + HGRN_SKEW):
            if step < len(items):
                pending[step] = scan(*items[step])
            done = step - HGRN_SKEW
            if done >= 0:
                finish(*items[done], *pending.pop(done))
        return carry

    assert n_chunks % HGRN_CHUNKS_PER_ITER == 0
    lax.fori_loop(0, n_chunks // HGRN_CHUNKS_PER_ITER, chunk_body, 0)


def _merge_kernel(o1_ref, o2_ref, o3_ref, l1_ref, l2_ref, l3_ref, za_ref, ob_ref,
                  sga_ref, sgb_ref, x_ref, mod_ref, wa_ref, wb_ref, wo_ref, fg_ref,
                  ex_ref, out_ref):
    ex = ex_ref[...]
    gate = mod_ref[:, 2 * D_MODEL:3 * D_MODEL]

    def expand(w):
        hi = w.astype(BF16)
        lo = (w - hi.astype(F32)).astype(BF16)
        return jnp.dot(jnp.concatenate([hi, lo], axis=1), ex, preferred_element_type=F32)

    def mix(rows):
        l1, l2, l3 = l1_ref[rows, :], l2_ref[rows, :], l3_ref[rows, :]
        mx = jnp.maximum(jnp.maximum(l1, l2), l3)
        e1, e2, e3 = jnp.exp(l1 - mx), jnp.exp(l2 - mx), jnp.exp(l3 - mx)
        inv = 1.0 / (e1 + e2 + e3)
        oa = (expand(e1 * inv) * o1_ref[rows, :].astype(F32)
              + expand(e2 * inv) * o2_ref[rows, :].astype(F32)
              + expand(e3 * inv) * o3_ref[rows, :].astype(F32))
        return (oa * za_ref[rows, :].astype(F32)).astype(BF16)

    def branches(rows, oa):
        ya = jnp.dot(oa, wa_ref[...], preferred_element_type=F32)
        yb = jnp.dot(ob_ref[rows, :], wb_ref[...], preferred_element_type=F32)
        return (sga_ref[rows, :].astype(F32) * ya + sgb_ref[rows, :].astype(F32) * yb).astype(BF16)

    def project(rows, y):
        z = jnp.dot(y, wo_ref[...], preferred_element_type=F32)
        xo = x_ref[rows, :] + gate * z
        ms = jnp.mean(xo * xo, axis=-1, keepdims=True)
        out_ref[rows, :] = xo * lax.rsqrt(ms + EPS) * fg_ref[...]

    pieces = [slice(r, r + MERGE_PIECE) for r in range(0, x_ref.shape[0], MERGE_PIECE)]
    oas = [mix(rows) for rows in pieces]
    ys = [branches(rows, oa) for rows, oa in zip(pieces, oas)]
    for rows, y in zip(pieces, ys):
        project(rows, y)


def _cparams(sem, vmem_mb):
    return pltpu.CompilerParams(dimension_semantics=sem,
                                vmem_limit_bytes=vmem_mb * 1024 * 1024)


def kernel(x, c, w_ada, b_ada, norm_g, w_in, hgrn_onorm_g, w_branch_a, w_branch_b, w_out,
           rel_bias, hgrn_lb, final_g):
    B, S, D = x.shape
    assert D == D_MODEL and w_ada.shape[0] == 1, "single-layer kernel"
    N = B * S
    x2 = x.reshape(N, D)

    c8 = jnp.pad(c, ((0, 8 - B), (0, 0)))
    mod = pl.pallas_call(
        _mod_kernel,
        grid=(3 * D // 512,),
        in_specs=[pl.BlockSpec((8, D), lambda j: (0, 0)),
                  pl.BlockSpec((D, 512), lambda j: (0, j)),
                  pl.BlockSpec((1, 512), lambda j: (0, j))],
        out_specs=pl.BlockSpec((8, 512), lambda j: (0, j)),
        out_shape=jax.ShapeDtypeStruct((8, 3 * D), F32),
        name="adaln_mod",
    )(c8, w_ada[0], b_ada[0].reshape(1, 3 * D))
    mod3 = mod.reshape(8, 1, 3 * D)

    lb = pl.pallas_call(
        _lower_bound_kernel,
        out_shape=jax.ShapeDtypeStruct((1, HGRN_WIDTH), F32),
        name="hgrn_lower_bound",
    )(hgrn_lb)

    n_pat = len(DILATED_PATTERNS)
    bias_tab = pl.pallas_call(
        _bias_table_kernel,
        grid=(n_pat,),
        in_specs=[pl.BlockSpec(memory_space=pltpu.SMEM),
                  pl.BlockSpec((None, ATTN_BLOCK, 2 * ATTN_BLOCK), lambda g: (g, 0, 0))],
        out_specs=pl.BlockSpec((None, 2, ATTN_HEADS, ATTN_BLOCK, 2 * ATTN_BLOCK),
                               lambda g: (g, 0, 0, 0, 0)),
        out_shape=jax.ShapeDtypeStruct((n_pat, 2, ATTN_HEADS, ATTN_BLOCK, 2 * ATTN_BLOCK), F32),
        name="rel_bias_table",
    )(rel_bias, jnp.asarray(_bucket_tables()))

    tm = PERM_TILE
    tiles_per_b = S // tm
    in_width = len(_PROJ_KINDS) * PROJ_TN
    assert PROJ_TN == ATTN_WIDTH and w_in.shape[2] == in_width
    resident = dict(pipeline_mode=pl.Buffered(1))
    main, fgate, qkv_p = pl.pallas_call(
        _inproj_kernel,
        grid=(N // tm,),
        in_specs=[pl.BlockSpec((tm, D), lambda i: (i, 0)),
                  pl.BlockSpec((None, 1, 3 * D), lambda i: (i // tiles_per_b, 0, 0)),
                  pl.BlockSpec((1, D), lambda i: (0, 0)),
                  pl.BlockSpec((1, HGRN_WIDTH), lambda i: (0, 0)),
                  pl.BlockSpec((D, in_width), lambda i: (0, 0), **resident)],
        out_specs=[pl.BlockSpec((tm, MAIN_WIDTH), lambda i: (i, 0)),
                   pl.BlockSpec((tm, HGRN_WIDTH), lambda i: (i, 0)),
                   pl.BlockSpec((tm, _QKV_TILES * ATTN_WIDTH), lambda i: (i, 0))],
        out_shape=[jax.ShapeDtypeStruct((N, MAIN_WIDTH), BF16),
                   jax.ShapeDtypeStruct((N, HGRN_WIDTH), F32),
                   jax.ShapeDtypeStruct((N, _QKV_TILES * ATTN_WIDTH), BF16)],
        scratch_shapes=[pltpu.VMEM((tm, D), BF16),
                        pltpu.VMEM((2, PROJ_PIECE_N // V7X_LANES, PROJ_PIECE_M, V7X_LANES), F32)],
        compiler_params=_cparams(("arbitrary",), 56),
        name="inproj",
    )(x2, mod3, norm_g[0].reshape(1, D), lb, w_in[0].astype(BF16))

    qa_t = _MAIN_COLS["qa"][0] // ATTN_WIDTH
    ka_t = _MAIN_COLS["ka"][0] // ATTN_WIDTH
    va_t = _MAIN_COLS["va"][0] // ATTN_WIDTH
    attn_outs = []
    main_b = main.reshape(B, S, MAIN_WIDTH)
    qkv_pv = qkv_p.reshape(B, S, _QKV_TILES * ATTN_WIDTH)
    n_super = S // ATTN_SUPER
    for g, (window, dil) in enumerate(DILATED_PATTERNS):
        assert window // dil == ATTN_BLOCK
        src, tiles = (main_b, (qa_t, ka_t, va_t)) if dil == 1 else (qkv_pv, (0, 1, 2))
        prev_rows = {1: ATTN_BLOCK, PERM_DIL: ATTN_SUPER}.get(dil, PERM_TILE)
        per_step = ATTN_SUPER // prev_rows

        def cur_spec(t):
            return pl.BlockSpec((None, ATTN_SUPER, ATTN_WIDTH), lambda b, n, t=t: (b, n, t))

        def prev_spec(t, prev_rows=prev_rows, per_step=per_step):
            return pl.BlockSpec((None, prev_rows, ATTN_WIDTH),
                                lambda b, n, t=t: (b, jnp.maximum(n * per_step - 1, 0), t))

        scratch = []
        if dil > 1:
            scratch += [pltpu.VMEM((ATTN_WIDTH // V7X_LANES, ATTN_SUPER, V7X_LANES), F32)]
        if dil != PERM_DIL:
            scratch += [pltpu.VMEM((prev_rows + ATTN_SUPER, ATTN_WIDTH), BF16)] * 2
        o_g, lse_g = pl.pallas_call(
            functools.partial(_attn_kernel, dil=dil),
            grid=(B, n_super),
            in_specs=[cur_spec(tiles[0]), prev_spec(tiles[1]), cur_spec(tiles[1]),
                      prev_spec(tiles[2]), cur_spec(tiles[2]),
                      pl.BlockSpec((None, 2, ATTN_HEADS, ATTN_BLOCK, 2 * ATTN_BLOCK),
                                   lambda b, n, g=g: (g, 0, 0, 0, 0))],
            out_specs=[pl.BlockSpec((None, ATTN_SUPER, ATTN_WIDTH), lambda b, n: (b, n, 0)),
                       pl.BlockSpec((None, ATTN_SUPER, V7X_LANES), lambda b, n: (b, n, 0))],
            out_shape=[jax.ShapeDtypeStruct((B, S, ATTN_WIDTH), BF16),
                       jax.ShapeDtypeStruct((B, S, V7X_LANES), F32)],
            scratch_shapes=scratch,
            compiler_params=_cparams(("arbitrary", "arbitrary"), 48),
            name=f"dilated_attn_d{dil}",
        )(src, src, src, src, src, bias_tab)
        attn_outs.append((o_g.reshape(N, ATTN_WIDTH), lse_g.reshape(N, V7X_LANES)))

    th = 512
    chunk = 64
    hw_t = HGRN_WIDTH
    qb_t = _MAIN_COLS["qb"][0] // hw_t
    ib_t = _MAIN_COLS["ib"][0] // hw_t
    zb_t = _MAIN_COLS["zb"][0] // hw_t
    ob = pl.pallas_call(
        functools.partial(_hgrn_kernel, chunk=chunk, n_chunks=th // chunk),
        grid=(B, S // th),
        in_specs=[pl.BlockSpec((None, th, hw_t), lambda b, s: (b, s, qb_t)),
                  pl.BlockSpec((None, th, hw_t), lambda b, s: (b, s, 0)),
                  pl.BlockSpec((None, th, hw_t), lambda b, s: (b, s, ib_t)),
                  pl.BlockSpec((None, th, hw_t), lambda b, s: (b, s, zb_t)),
                  pl.BlockSpec((1, HGRN_VAL_DIM), lambda b, s: (0, 0))],
        out_specs=pl.BlockSpec((None, th, hw_t), lambda b, s: (b, s, 0)),
        out_shape=jax.ShapeDtypeStruct((B, S, hw_t), BF16),
        scratch_shapes=[pltpu.VMEM((HGRN_HEADS, HGRN_VAL_DIM, HGRN_KEY_DIM), F32)],
        compiler_params=_cparams(("arbitrary", "arbitrary"), 32),
        name="hgrn2",
    )(main_b, fgate.reshape(B, S, hw_t), main_b, main_b, hgrn_onorm_g[0].reshape(1, HGRN_VAL_DIM))
    ob = ob.reshape(N, hw_t)

    tk = 2 * MERGE_PIECE
    tiles_per_b5 = S // tk
    za_t = _MAIN_COLS["za"][0] // ATTN_WIDTH
    ga_t = _MAIN_COLS["ga"][0] // D
    gb_t = _MAIN_COLS["gb"][0] // D
    expand_mat = np.zeros((2 * V7X_LANES, ATTN_WIDTH), np.float32)
    for h in range(ATTN_HEADS):
        expand_mat[h, h * ATTN_HEAD_DIM:(h + 1) * ATTN_HEAD_DIM] = 1.0
        expand_mat[V7X_LANES + h, h * ATTN_HEAD_DIM:(h + 1) * ATTN_HEAD_DIM] = 1.0
    (o1, l1), (o2, l2), (o3, l3) = attn_outs
    row_spec = lambda w, t=0: pl.BlockSpec((tk, w), lambda i, t=t: (i, t))
    full_spec = lambda a, b: pl.BlockSpec((a, b), lambda i: (0, 0))
    out = pl.pallas_call(
        _merge_kernel,
        grid=(N // tk,),
        in_specs=[row_spec(ATTN_WIDTH), row_spec(ATTN_WIDTH), row_spec(ATTN_WIDTH),
                  row_spec(V7X_LANES), row_spec(V7X_LANES), row_spec(V7X_LANES),
                  row_spec(ATTN_WIDTH, za_t), row_spec(HGRN_WIDTH),
                  row_spec(D, ga_t), row_spec(D, gb_t), row_spec(D),
                  pl.BlockSpec((None, 1, 3 * D), lambda i: (i // tiles_per_b5, 0, 0)),
                  full_spec(ATTN_WIDTH, D), full_spec(HGRN_WIDTH, D), full_spec(D, D),
                  full_spec(1, D), full_spec(2 * V7X_LANES, ATTN_WIDTH)],
        out_specs=pl.BlockSpec((tk, D), lambda i: (i, 0)),
        out_shape=jax.ShapeDtypeStruct((N, D), F32),
        compiler_params=_cparams(("arbitrary",), 40),
        name="gated_merge",
    )(o1, o2, o3, l1, l2, l3, main, ob, main, main, x2, mod3,
      w_branch_a[0].astype(BF16), w_branch_b[0].astype(BF16), w_out[0].astype(BF16),
      final_g.reshape(1, D), jnp.asarray(expand_mat, BF16))
    return out.reshape(B, S, D)
```

```python
import functools
import math
import types

import numpy as np
import jax
import jax.numpy as jnp
from jax import lax
from jax.experimental import pallas as pl
from jax.experimental.pallas import tpu as pltpu

D_MODEL = 1024
ATTN_HEADS = 8
ATTN_HEAD_DIM = 64
ATTN_WIDTH = ATTN_HEADS * ATTN_HEAD_DIM
DILATED_PATTERNS = ((128, 1), (512, 4), (2048, 16))
ATTN_BLOCK = 128
N_BUCKETS = 32
MAX_DISTANCE = 2048
NEG_INF = -1e30
HGRN_HEADS = 8
HGRN_KEY_DIM = 128
HGRN_VAL_DIM = 128
HGRN_WIDTH = HGRN_HEADS * HGRN_VAL_DIM
EPS = 1e-6

V7X_LANES = 128
V7X_SUBLANES = 8

F32 = jnp.float32
BF16 = jnp.bfloat16

_MAIN_COLS = {}
_off = 0
for _name, _w in (("qa", ATTN_WIDTH), ("ka", ATTN_WIDTH), ("va", ATTN_WIDTH), ("za", ATTN_WIDTH),
                  ("qb", HGRN_WIDTH), ("ib", HGRN_WIDTH), ("zb", HGRN_WIDTH),
                  ("ga", D_MODEL), ("gb", D_MODEL)):
    _MAIN_COLS[_name] = (_off, _w)
    _off += _w
MAIN_WIDTH = _off
PROJ_TN = 512
PROJ_PIECE_M = 256
PROJ_PIECE_N = 256
PROJ_SKEW = 2
_PROJ_KINDS = (["qscale"] + ["kv"] * 2 + ["silu"] + ["silu"] * 2 + ["forget"] * 2
               + ["id"] * 2 + ["silu"] * 2 + ["sigmoid"] * 4)
_F_TILE0 = _PROJ_KINDS.index("forget")
_F_TILES = _PROJ_KINDS.count("forget")
_QKV_TILES = 3

PERM_DIL = 16
PERM_TILE = 512
PERM_SLAB = PERM_TILE // PERM_DIL
ATTN_SUPER = PERM_DIL * ATTN_BLOCK
ATTN_BLOCKS_PER_ITER = 2
ATTN_SKEW = 2
MERGE_PIECE = 256
HGRN_CHUNKS_PER_ITER = 8
HGRN_SKEW = 1
LOG2E = math.log2(math.e)
LN2 = math.log(2.0)


def _sigmoid(x):
    return 1.0 / (1.0 + jnp.exp(-x))


def _mod_kernel(c_ref, w_ref, b_ref, o_ref):
    c = c_ref[...]
    sc = c * _sigmoid(c)
    o_ref[...] = jnp.dot(sc, w_ref[...], precision=lax.Precision.HIGHEST,
                         preferred_element_type=F32) + b_ref[...]


def _lower_bound_kernel(hl_ref, o_ref):
    hl = hl_ref[...]
    m = jnp.max(hl, axis=0, keepdims=True)
    e = jnp.exp(hl - m)
    o_ref[...] = e[0:1, :] / jnp.sum(e, axis=0, keepdims=True)


def _bias_table_kernel(rb_ref, bucket_ref, o_ref):
    bk = bucket_ref[...]
    no_prev = lax.broadcasted_iota(jnp.int32, bk.shape, 1) < ATTN_BLOCK
    for h in range(ATTN_HEADS):
        acc = jnp.full(bk.shape, NEG_INF, F32)
        for u in range(N_BUCKETS):
            acc = jnp.where(bk == u, rb_ref[u, h] * LOG2E, acc)
        o_ref[0, h] = acc
        o_ref[1, h] = jnp.where(no_prev, NEG_INF, acc)


def _bucket_tables():
    qi = np.arange(ATTN_BLOCK)[:, None]
    kj = np.arange(2 * ATTN_BLOCK)[None, :]
    delta = qi + ATTN_BLOCK - kj
    max_exact = N_BUCKETS // 2
    tabs = []
    for window, dilation in DILATED_PATTERNS:
        span = window // dilation
        band = (delta >= 0) & (delta <= span)
        dist = np.clip(delta, 0, None) * dilation
        n = dist.astype(np.float32)
        large = max_exact + (np.log(np.maximum(n, 1.0) / max_exact)
                             / math.log(MAX_DISTANCE / max_exact)
                             * (N_BUCKETS - max_exact)).astype(np.int32)
        large = np.minimum(large, N_BUCKETS - 1)
        bucket = np.where(dist < max_exact, dist, large)
        tab = np.where(band, bucket, -1).astype(np.int32)
        if dilation > 1:
            order = _gather_order(dilation)
            cols = np.concatenate([order, ATTN_BLOCK + order])
            tab = tab[order][:, cols]
        tabs.append(tab)
    return np.stack(tabs, 0)


def _gather_order(dilation):
    per_tile = PERM_DIL // dilation
    slab = np.arange(ATTN_BLOCK) // PERM_SLAB
    m = np.arange(ATTN_BLOCK) % PERM_SLAB
    if per_tile == 1:
        return slab * PERM_SLAB + m
    assert per_tile * PERM_SLAB == ATTN_BLOCK
    return per_tile * m + slab


def _inproj_kernel(x_ref, mod_ref, g_ref, lb_ref, w_ref, om_ref, of_ref, op_ref, h_ref, accl_ref):
    tm = x_ref.shape[0]
    x = x_ref[...]
    ms = jnp.mean(x * x, axis=-1, keepdims=True)
    y = x * lax.rsqrt(ms + EPS) * g_ref[...]
    shift = mod_ref[:, 0:D_MODEL]
    scale = mod_ref[:, D_MODEL:2 * D_MODEL]
    h_ref[...] = (y * (1.0 + scale) + shift).astype(BF16)

    n_sub = PROJ_TN // PROJ_PIECE_N
    slab = PROJ_PIECE_M // PERM_DIL

    def write_perm(mc, col0, acc, slot):
        for c in range(PROJ_PIECE_N // V7X_LANES):
            accl_ref[slot, c] = acc[:, c * V7X_LANES:(c + 1) * V7X_LANES]
            for r in range(PERM_DIL):
                rows = accl_ref[slot, c, pl.ds(r, slab, stride=PERM_DIL), :]
                row0 = r * PERM_SLAB + mc * slab
                op_ref[row0:row0 + slab, col0 + c * V7X_LANES:col0 + (c + 1) * V7X_LANES] = rows.astype(BF16)

    def epilogue(j, sub, mc, acc):
        kind = _PROJ_KINDS[j]
        rows = slice(mc * PROJ_PIECE_M, (mc + 1) * PROJ_PIECE_M)
        if kind == "forget":
            col0 = (j - _F_TILE0) * PROJ_TN + sub * PROJ_PIECE_N
            lb = lb_ref[:, col0:col0 + PROJ_PIECE_N]
            of_ref[rows, col0:col0 + PROJ_PIECE_N] = lb + (1.0 - lb) * _sigmoid(acc)
            return
        if kind == "qscale":
            acc = acc * (ATTN_HEAD_DIM ** -0.5 * LOG2E)
        elif kind == "silu":
            acc = acc * _sigmoid(acc)
        elif kind == "sigmoid":
            acc = _sigmoid(acc)
        jm = j if j < _F_TILE0 else j - _F_TILES
        col0 = jm * PROJ_TN + sub * PROJ_PIECE_N
        om_ref[rows, col0:col0 + PROJ_PIECE_N] = acc.astype(BF16)
        if j < _QKV_TILES:
            write_perm(mc, col0, acc, (sub * (tm // PROJ_PIECE_M) + mc) % accl_ref.shape[0])

    items = [(j, sub, mc) for j in range(len(_PROJ_KINDS)) for sub in range(n_sub)
             for mc in range(tm // PROJ_PIECE_M)]
    pending = {}
    for step in range(len(items) + PROJ_SKEW):
        if step < len(items):
            j, sub, mc = items[step]
            col0 = j * PROJ_TN + sub * PROJ_PIECE_N
            pending[step] = jnp.dot(h_ref[mc * PROJ_PIECE_M:(mc + 1) * PROJ_PIECE_M, :],
                                    w_ref[:, col0:col0 + PROJ_PIECE_N], preferred_element_type=F32)
        done = step - PROJ_SKEW
        if done >= 0:
            epilogue(*items[done], pending.pop(done))


def _pair_lanes(hp):
    return slice(hp * V7X_LANES, (hp + 1) * V7X_LANES)


def _attn_scores(q, k, bias_ref, hp, first, lane):
    blk = q.shape[0]
    low = lane < ATTN_HEAD_DIM
    zero = jnp.zeros_like(q)
    q2 = jnp.concatenate([jnp.where(low, q, zero), jnp.where(low, zero, q)], axis=0)
    s = lax.dot_general(q2, k, (((1,), (1,)), ((), ())), preferred_element_type=F32)
    s = s + bias_ref[first, pl.ds(2 * hp, 2)].reshape(2 * blk, 2 * blk)
    return s, jnp.max(s, axis=-1, keepdims=True)


def _attn_values(s, m, v, hp, lane, lse_all):
    blk = s.shape[0] // 2
    p = jnp.exp2(s - m).astype(BF16)
    h0, h1 = 2 * hp, 2 * hp + 1
    low = lane < ATTN_HEAD_DIM
    low_v = lax.broadcasted_iota(jnp.int32, v.shape, 1) < ATTN_HEAD_DIM
    one = jnp.ones_like(v)
    o0 = jnp.dot(p[:blk], jnp.where(low_v, v, one), preferred_element_type=F32)
    o1 = jnp.dot(p[blk:], jnp.where(low_v, one, v), preferred_element_type=F32)
    num = jnp.where(low, o0, o1)
    den_swapped = jnp.where(low, o1, o0)
    den = pltpu.roll(den_swapped, ATTN_HEAD_DIM, 1)
    is_h1 = lane == h1
    lse = jnp.where(is_h1, m[blk:], m[:blk]) * LN2 + jnp.log(jnp.where(is_h1, den_swapped, den))
    lse_all = jnp.where(jnp.logical_or(lane == h0, is_h1), lse, lse_all)
    return num / den, lse_all


def _attn_pipeline(blocks, bias_ref, lane):
    n_pairs = ATTN_HEADS // 2
    items = [(bi, hp) for bi in range(len(blocks)) for hp in range(n_pairs)]
    lse = [jnp.zeros(lane.shape, F32) for _ in blocks]
    pending = {}
    for step in range(len(items) + ATTN_SKEW):
        if step < len(items):
            bi, hp = items[step]
            b = blocks[bi]
            pending[step] = _attn_scores(b.get_q(hp), b.get_k(hp), bias_ref, hp, b.first, lane)
        done = step - ATTN_SKEW
        if done >= 0:
            bi, hp = items[done]
            b = blocks[bi]
            o_pair, lse[bi] = _attn_values(*pending.pop(done), b.get_v(hp), hp, lane, lse[bi])
            b.put_o(hp, o_pair)
            if hp == n_pairs - 1:
                b.put_lse(lse[bi])


def _attn_kernel(q_ref, kp_ref, kc_ref, vp_ref, vc_ref, bias_ref, o_ref, lse_ref, *scratch, dil):
    n = pl.program_id(1)
    blk_rows = ATTN_BLOCK
    lane = lax.broadcasted_iota(jnp.int32, (blk_rows, V7X_LANES), 1)
    prev_rows = kp_ref.shape[0]
    scratch = list(scratch)
    o_scr = scratch.pop(0) if dil > 1 else None
    if scratch:
        kext, vext = scratch
        kext[0:prev_rows] = kp_ref[...]
        kext[prev_rows:] = kc_ref[...]
        vext[0:prev_rows] = vp_ref[...]
        vext[prev_rows:] = vc_ref[...]
        k_prev, k_cur, v_prev, v_cur, cur_off = kext, kext, vext, vext, prev_rows
    else:
        k_prev, k_cur, v_prev, v_cur, cur_off = kp_ref, kc_ref, vp_ref, vc_ref, 0

    def make_block(blk):
        if dil == 1:
            slab = blk_rows
            offs = [blk * blk_rows]
            first = jnp.logical_and(n == 0, blk == 0)
        elif dil == PERM_DIL:
            slab = PERM_SLAB
            offs = [t * PERM_TILE + blk * PERM_SLAB for t in range(blk_rows // slab)]
            tok0 = [t * PERM_TILE + blk for t in range(blk_rows // slab)]
            first = n == 0
        else:
            slab = PERM_SLAB
            tile = lax.shift_right_logical(blk, 2)
            res = jnp.bitwise_and(blk, dil - 1)
            offs = [tile * PERM_TILE + (res + dil * j) * PERM_SLAB for j in range(blk_rows // slab)]
            tok0 = [tile * PERM_TILE + res + dil * j for j in range(blk_rows // slab)]
            first = jnp.logical_and(n == 0, tile == 0)
        offs = [pl.multiple_of(o, slab) for o in offs]

        def slabs(ref, hp, shift=0, rows=slab):
            return [ref[pl.ds(o + shift, rows), _pair_lanes(hp)] for o in offs]

        def put_o(hp, o_pair):
            if dil == 1:
                o_ref[pl.ds(offs[0], slab), _pair_lanes(hp)] = o_pair.astype(BF16)
                return
            for j, t0 in enumerate(tok0):
                o_scr[hp, pl.ds(t0, slab, stride=PERM_DIL), :] = o_pair[j * slab:(j + 1) * slab]

        def put_lse(lse_all):
            if dil == 1:
                lse_ref[pl.ds(offs[0], slab), :] = lse_all
                return
            for j, t0 in enumerate(tok0):
                lse_ref[pl.ds(t0, slab, stride=PERM_DIL), :] = lse_all[j * slab:(j + 1) * slab]

        if dil == 1:
            get_k = lambda hp: slabs(k_cur, hp, rows=2 * slab)[0]
            get_v = lambda hp: slabs(v_cur, hp, rows=2 * slab)[0]
        else:
            get_k = lambda hp: jnp.concatenate(slabs(k_prev, hp) + slabs(k_cur, hp, cur_off), axis=0)
            get_v = lambda hp: jnp.concatenate(slabs(v_prev, hp) + slabs(v_cur, hp, cur_off), axis=0)
        return types.SimpleNamespace(
            get_q=lambda hp: jnp.concatenate(slabs(q_ref, hp), axis=0), get_k=get_k, get_v=get_v,
            put_o=put_o, put_lse=put_lse, first=first.astype(jnp.int32))

    def body(it, carry):
        _attn_pipeline([make_block(it * ATTN_BLOCKS_PER_ITER + u) for u in range(ATTN_BLOCKS_PER_ITER)],
                       bias_ref, lane)
        return carry

    lax.fori_loop(0, ATTN_SUPER // ATTN_BLOCK // ATTN_BLOCKS_PER_ITER, body, 0)
    if dil > 1:
        for hp in range(ATTN_HEADS // 2):
            o_ref[:, _pair_lanes(hp)] = o_scr[hp].astype(BF16)


def _roll_rows(a, shift):
    rows = a.shape[0]
    if abs(shift) >= V7X_SUBLANES:
        return pltpu.roll(a, shift % rows, 0)
    grouped = a.reshape(rows // V7X_SUBLANES, V7X_SUBLANES, a.shape[1])
    return pltpu.roll(grouped, shift % V7X_SUBLANES, 1).reshape(a.shape)


def _hgrn_kernel(q_ref, f_ref, i_ref, z_ref, g_ref, o_ref, state_ref, *, chunk, n_chunks):
    @pl.when(pl.program_id(1) == 0)
    def _():
        state_ref[...] = jnp.zeros_like(state_ref)

    C = chunk
    n_levels = C.bit_length() - 1
    g_on = g_ref[...]
    nt = (((1,), (1,)), ((), ()))

    def chunk_body(ci, carry):
        row = lax.broadcasted_iota(jnp.int32, (C, HGRN_KEY_DIM), 0)
        odds = [jnp.bitwise_and(row, 1 << lvl) != 0 for lvl in range(n_levels)]
        tt = lax.broadcasted_iota(jnp.int32, (C, C), 0)
        ss = lax.broadcasted_iota(jnp.int32, (C, C), 1)
        owner = jnp.where(tt > ss, 32 - lax.clz(jnp.bitwise_xor(tt, ss)),
                          jnp.where(tt == ss, 0, -1))
        owned = [owner == lvl for lvl in range(n_levels + 1)]

        def chunk_rows(u):
            return pl.ds(pl.multiple_of((ci * HGRN_CHUNKS_PER_ITER + u) * C, C), C)

        def scan(u, h):
            rows = chunk_rows(u)
            hs = slice(h * HGRN_KEY_DIM, (h + 1) * HGRN_KEY_DIM)
            f = f_ref[rows, hs]
            q = q_ref[rows, hs]
            k = (1.0 - f).astype(BF16)
            a = jnp.where(owned[0], lax.dot_general(q, k, nt, preferred_element_type=F32), 0.0)
            x = jnp.where(odds[0], f, 1.0)
            y = jnp.where(odds[0], 1.0, f)
            for lvl in range(n_levels):
                m = 1 << lvl
                e = x.astype(BF16)
                pm = lax.dot_general(q * e, k * e, nt, preferred_element_type=F32)
                a = jnp.where(owned[lvl + 1], pm, a)
                tot = x * y if lvl else f
                if 2 * m == V7X_SUBLANES:
                    partner = _roll_rows(tot, m)
                else:
                    partner = jnp.where(odds[lvl], _roll_rows(tot, m), _roll_rows(tot, -m))
                z = x * partner
                keep = odds[lvl] == odds[lvl + 1] if lvl + 1 < n_levels else jnp.logical_not(odds[lvl])
                x, y = jnp.where(keep, z, y), jnp.where(keep, y, z)
            return a.astype(BF16), q * y.astype(BF16), k * x.astype(BF16), x[0:1, :] * y[0:1, :]

        def finish(u, h, a, q_dec, k_dec, decay):
            rows = chunk_rows(u)
            hs = slice(h * HGRN_KEY_DIM, (h + 1) * HGRN_KEY_DIM)
            v = i_ref[rows, hs]
            st = state_ref[h]
            o = (jnp.dot(a, v, preferred_element_type=F32)
                 + lax.dot_general(q_dec, st.astype(BF16), nt, preferred_element_type=F32))
            upd = lax.dot_general(v, k_dec, (((0,), (0,)), ((), ())), preferred_element_type=F32)
            state_ref[h] = st * decay + upd
            ms = jnp.mean(o * o, axis=-1, keepdims=True)
            y = o * lax.rsqrt(ms + EPS) * g_on
            o_ref[rows, hs] = (y * z_ref[rows, hs].astype(F32)).astype(BF16)

        items = [(u, h) for u in range(HGRN_CHUNKS_PER_ITER) for h in range(HGRN_HEADS)]
        pending = {}
        for step in range(len(items) + HGRN_SKEW):
            if step < len(items):
                pending[step] = scan(*items[step])
            done = step - HGRN_SKEW
            if done >= 0:
                finish(*items[done], *pending.pop(done))
        return carry

    assert n_chunks % HGRN_CHUNKS_PER_ITER == 0
    lax.fori_loop(0, n_chunks // HGRN_CHUNKS_PER_ITER, chunk_body, 0)


def _merge_kernel(o1_ref, o2_ref, o3_ref, l1_ref, l2_ref, l3_ref, za_ref, ob_ref,
                  sga_ref, sgb_ref, x_ref, mod_ref, wa_ref, wb_ref, wo_ref, fg_ref,
                  ex_ref, out_ref):
    ex = ex_ref[...]
    gate = mod_ref[:, 2 * D_MODEL:3 * D_MODEL]

    def expand(w):
        hi = w.astype(BF16)
        lo = (w - hi.astype(F32)).astype(BF16)
        return jnp.dot(jnp.concatenate([hi, lo], axis=1), ex, preferred_element_type=F32)

    def mix(rows):
        l1, l2, l3 = l1_ref[rows, :], l2_ref[rows, :], l3_ref[rows, :]
        mx = jnp.maximum(jnp.maximum(l1, l2), l3)
        e1, e2, e3 = jnp.exp(l1 - mx), jnp.exp(l2 - mx), jnp.exp(l3 - mx)
        inv = 1.0 / (e1 + e2 + e3)
        oa = (expand(e1 * inv) * o1_ref[rows, :].astype(F32)
              + expand(e2 * inv) * o2_ref[rows, :].astype(F32)
              + expand(e3 * inv) * o3_ref[rows, :].astype(F32))
        return (oa * za_ref[rows, :].astype(F32)).astype(BF16)

    def branches(rows, oa):
        ya = jnp.dot(oa, wa_ref[...], preferred_element_type=F32)
        yb = jnp.dot(ob_ref[rows, :], wb_ref[...], preferred_element_type=F32)
        return (sga_ref[rows, :].astype(F32) * ya + sgb_ref[rows, :].astype(F32) * yb).astype(BF16)

    def project(rows, y):
        z = jnp.dot(y, wo_ref[...], preferred_element_type=F32)
        xo = x_ref[rows, :] + gate * z
        ms = jnp.mean(xo * xo, axis=-1, keepdims=True)
        out_ref[rows, :] = xo * lax.rsqrt(ms + EPS) * fg_ref[...]

    pieces = [slice(r, r + MERGE_PIECE) for r in range(0, x_ref.shape[0], MERGE_PIECE)]
    oas = [mix(rows) for rows in pieces]
    ys = [branches(rows, oa) for rows, oa in zip(pieces, oas)]
    for rows, y in zip(pieces, ys):
        project(rows, y)


def _cparams(sem, vmem_mb):
    return pltpu.CompilerParams(dimension_semantics=sem,
                                vmem_limit_bytes=vmem_mb * 1024 * 1024)


def kernel(x, c, w_ada, b_ada, norm_g, w_in, hgrn_onorm_g, w_branch_a, w_branch_b, w_out,
           rel_bias, hgrn_lb, final_g):
    B, S, D = x.shape
    assert D == D_MODEL and w_ada.shape[0] == 1, "single-layer kernel"
    N = B * S
    x2 = x.reshape(N, D)

    c8 = jnp.pad(c, ((0, 8 - B), (0, 0)))
    mod = pl.pallas_call(
        _mod_kernel,
        grid=(3 * D // 512,),
        in_specs=[pl.BlockSpec((8, D), lambda j: (0, 0)),
                  pl.BlockSpec((D, 512), lambda j: (0, j)),
                  pl.BlockSpec((1, 512), lambda j: (0, j))],
        out_specs=pl.BlockSpec((8, 512), lambda j: (0, j)),
        out_shape=jax.ShapeDtypeStruct((8, 3 * D), F32),
        name="adaln_mod",
    )(c8, w_ada[0], b_ada[0].reshape(1, 3 * D))
    mod3 = mod.reshape(8, 1, 3 * D)

    lb = pl.pallas_call(
        _lower_bound_kernel,
        out_shape=jax.ShapeDtypeStruct((1, HGRN_WIDTH), F32),
        name="hgrn_lower_bound",
    )(hgrn_lb)

    n_pat = len(DILATED_PATTERNS)
    bias_tab = pl.pallas_call(
        _bias_table_kernel,
        grid=(n_pat,),
        in_specs=[pl.BlockSpec(memory_space=pltpu.SMEM),
                  pl.BlockSpec((None, ATTN_BLOCK, 2 * ATTN_BLOCK), lambda g: (g, 0, 0))],
        out_specs=pl.BlockSpec((None, 2, ATTN_HEADS, ATTN_BLOCK, 2 * ATTN_BLOCK),
                               lambda g: (g, 0, 0, 0, 0)),
        out_shape=jax.ShapeDtypeStruct((n_pat, 2, ATTN_HEADS, ATTN_BLOCK, 2 * ATTN_BLOCK), F32),
        name="rel_bias_table",
    )(rel_bias, jnp.asarray(_bucket_tables()))

    tm = PERM_TILE
    tiles_per_b = S // tm
    in_width = len(_PROJ_KINDS) * PROJ_TN
    assert PROJ_TN == ATTN_WIDTH and w_in.shape[2] == in_width
    resident = dict(pipeline_mode=pl.Buffered(1))
    main, fgate, qkv_p = pl.pallas_call(
        _inproj_kernel,
        grid=(N // tm,),
        in_specs=[pl.BlockSpec((tm, D), lambda i: (i, 0)),
                  pl.BlockSpec((None, 1, 3 * D), lambda i: (i // tiles_per_b, 0, 0)),
                  pl.BlockSpec((1, D), lambda i: (0, 0)),
                  pl.BlockSpec((1, HGRN_WIDTH), lambda i: (0, 0)),
                  pl.BlockSpec((D, in_width), lambda i: (0, 0), **resident)],
        out_specs=[pl.BlockSpec((tm, MAIN_WIDTH), lambda i: (i, 0)),
                   pl.BlockSpec((tm, HGRN_WIDTH), lambda i: (i, 0)),
                   pl.BlockSpec((tm, _QKV_TILES * ATTN_WIDTH), lambda i: (i, 0))],
        out_shape=[jax.ShapeDtypeStruct((N, MAIN_WIDTH), BF16),
                   jax.ShapeDtypeStruct((N, HGRN_WIDTH), F32),
                   jax.ShapeDtypeStruct((N, _QKV_TILES * ATTN_WIDTH), BF16)],
        scratch_shapes=[pltpu.VMEM((tm, D), BF16),
                        pltpu.VMEM((2, PROJ_PIECE_N // V7X_LANES, PROJ_PIECE_M, V7X_LANES), F32)],
        compiler_params=_cparams(("arbitrary",), 56),
        name="inproj",
    )(x2, mod3, norm_g[0].reshape(1, D), lb, w_in[0].astype(BF16))

    qa_t = _MAIN_COLS["qa"][0] // ATTN_WIDTH
    ka_t = _MAIN_COLS["ka"][0] // ATTN_WIDTH
    va_t = _MAIN_COLS["va"][0] // ATTN_WIDTH
    attn_outs = []
    main_b = main.reshape(B, S, MAIN_WIDTH)
    qkv_pv = qkv_p.reshape(B, S, _QKV_TILES * ATTN_WIDTH)
    n_super = S // ATTN_SUPER
    for g, (window, dil) in enumerate(DILATED_PATTERNS):
        assert window // dil == ATTN_BLOCK
        src, tiles = (main_b, (qa_t, ka_t, va_t)) if dil == 1 else (qkv_pv, (0, 1, 2))
        prev_rows = {1: ATTN_BLOCK, PERM_DIL: ATTN_SUPER}.get(dil, PERM_TILE)
        per_step = ATTN_SUPER // prev_rows

        def cur_spec(t):
            return pl.BlockSpec((None, ATTN_SUPER, ATTN_WIDTH), lambda b, n, t=t: (b, n, t))

        def prev_spec(t, prev_rows=prev_rows, per_step=per_step):
            return pl.BlockSpec((None, prev_rows, ATTN_WIDTH),
                                lambda b, n, t=t: (b, jnp.maximum(n * per_step - 1, 0), t))

        scratch = []
        if dil > 1:
            scratch += [pltpu.VMEM((ATTN_WIDTH // V7X_LANES, ATTN_SUPER, V7X_LANES), F32)]
        if dil != PERM_DIL:
            scratch += [pltpu.VMEM((prev_rows + ATTN_SUPER, ATTN_WIDTH), BF16)] * 2
        o_g, lse_g = pl.pallas_call(
            functools.partial(_attn_kernel, dil=dil),
            grid=(B, n_super),
            in_specs=[cur_spec(tiles[0]), prev_spec(tiles[1]), cur_spec(tiles[1]),
                      prev_spec(tiles[2]), cur_spec(tiles[2]),
                      pl.BlockSpec((None, 2, ATTN_HEADS, ATTN_BLOCK, 2 * ATTN_BLOCK),
                                   lambda b, n, g=g: (g, 0, 0, 0, 0))],
            out_specs=[pl.BlockSpec((None, ATTN_SUPER, ATTN_WIDTH), lambda b, n: (b, n, 0)),
                       pl.BlockSpec((None, ATTN_SUPER, V7X_LANES), lambda b, n: (b, n, 0))],
            out_shape=[jax.ShapeDtypeStruct((B, S, ATTN_WIDTH), BF16),
                       jax.ShapeDtypeStruct((B, S, V7X_LANES), F32)],
            scratch_shapes=scratch,
            compiler_params=_cparams(("arbitrary", "arbitrary"), 48),
            name=f"dilated_attn_d{dil}",
        )(src, src, src, src, src, bias_tab)
        attn_outs.append((o_g.reshape(N, ATTN_WIDTH), lse_g.reshape(N, V7X_LANES)))

    th = 512
    chunk = 64
    hw_t = HGRN_WIDTH
    qb_t = _MAIN_COLS["qb"][0] // hw_t
    ib_t = _MAIN_COLS["ib"][0] // hw_t
    zb_t = _MAIN_COLS["zb"][0] // hw_t
    ob = pl.pallas_call(
        functools.partial(_hgrn_kernel, chunk=chunk, n_chunks=th // chunk),
        grid=(B, S // th),
        in_specs=[pl.BlockSpec((None, th, hw_t), lambda b, s: (b, s, qb_t)),
                  pl.BlockSpec((None, th, hw_t), lambda b, s: (b, s, 0)),
                  pl.BlockSpec((None, th, hw_t), lambda b, s: (b, s, ib_t)),
                  pl.BlockSpec((None, th, hw_t), lambda b, s: (b, s, zb_t)),
                  pl.BlockSpec((1, HGRN_VAL_DIM), lambda b, s: (0, 0))],
        out_specs=pl.BlockSpec((None, th, hw_t), lambda b, s: (b, s, 0)),
        out_shape=jax.ShapeDtypeStruct((B, S, hw_t), BF16),
        scratch_shapes=[pltpu.VMEM((HGRN_HEADS, HGRN_VAL_DIM, HGRN_KEY_DIM), F32)],
        compiler_params=_cparams(("arbitrary", "arbitrary"), 32),
        name="hgrn2",
    )(main_b, fgate.reshape(B, S, hw_t), main_b, main_b, hgrn_onorm_g[0].reshape(1, HGRN_VAL_DIM))
    ob = ob.reshape(N, hw_t)

    tk = 2 * MERGE_PIECE
    tiles_per_b5 = S // tk
    za_t = _MAIN_COLS["za"][0] // ATTN_WIDTH
    ga_t = _MAIN_COLS["ga"][0] // D
    gb_t = _MAIN_COLS["gb"][0] // D
    expand_mat = np.zeros((2 * V7X_LANES, ATTN_WIDTH), np.float32)
    for h in range(ATTN_HEADS):
        expand_mat[h, h * ATTN_HEAD_DIM:(h + 1) * ATTN_HEAD_DIM] = 1.0
        expand_mat[V7X_LANES + h, h * ATTN_HEAD_DIM:(h + 1) * ATTN_HEAD_DIM] = 1.0
    (o1, l1), (o2, l2), (o3, l3) = attn_outs
    row_spec = lambda w, t=0: pl.BlockSpec((tk, w), lambda i, t=t: (i, t))
    full_spec = lambda a, b: pl.BlockSpec((a, b), lambda i: (0, 0))
    out = pl.pallas_call(
        _merge_kernel,
        grid=(N // tk,),
        in_specs=[row_spec(ATTN_WIDTH), row_spec(ATTN_WIDTH), row_spec(ATTN_WIDTH),
                  row_spec(V7X_LANES), row_spec(V7X_LANES), row_spec(V7X_LANES),
                  row_spec(ATTN_WIDTH, za_t), row_spec(HGRN_WIDTH),
                  row_spec(D, ga_t), row_spec(D, gb_t), row_spec(D),
                  pl.BlockSpec((None, 1, 3 * D), lambda i: (i // tiles_per_b5, 0, 0)),
                  full_spec(ATTN_WIDTH, D), full_spec(HGRN_WIDTH, D), full_spec(D, D),
                  full_spec(1, D), full_spec(2 * V7X_LANES, ATTN_WIDTH)],
        out_specs=pl.BlockSpec((tk, D), lambda i: (i, 0)),
        out_shape=jax.ShapeDtypeStruct((N, D), F32),
        compiler_params=_cparams(("arbitrary",), 40),
        name="gated_merge",
    )(o1, o2, o3, l1, l2, l3, main, ob, main, main, x2, mod3,
      w_branch_a[0].astype(BF16), w_branch_b[0].astype(BF16), w_out[0].astype(BF16),
      final_g.reshape(1, D), jnp.asarray(expand_mat, BF16))
    return out.reshape(B, S, D)
```

```python
import functools
import math
import types

import numpy as np
import jax
import jax.numpy as jnp
from jax import lax
from jax.experimental import pallas as pl
from jax.experimental.pallas import tpu as pltpu

D_MODEL = 1024
ATTN_HEADS = 8
ATTN_HEAD_DIM = 64
ATTN_WIDTH = ATTN_HEADS * ATTN_HEAD_DIM
DILATED_PATTERNS = ((128, 1), (512, 4), (2048, 16))
ATTN_BLOCK = 128
N_BUCKETS = 32
MAX_DISTANCE = 2048
NEG_INF = -1e30
HGRN_HEADS = 8
HGRN_KEY_DIM = 128
HGRN_VAL_DIM = 128
HGRN_WIDTH = HGRN_HEADS * HGRN_VAL_DIM
EPS = 1e-6

V7X_LANES = 128
V7X_SUBLANES = 8

F32 = jnp.float32
BF16 = jnp.bfloat16

_MAIN_COLS = {}
_off = 0
for _name, _w in (("qa", ATTN_WIDTH), ("ka", ATTN_WIDTH), ("va", ATTN_WIDTH), ("za", ATTN_WIDTH),
                  ("qb", HGRN_WIDTH), ("ib", HGRN_WIDTH), ("zb", HGRN_WIDTH),
                  ("ga", D_MODEL), ("gb", D_MODEL)):
    _MAIN_COLS[_name] = (_off, _w)
    _off += _w
MAIN_WIDTH = _off
PROJ_TN = 512
PROJ_PIECE_M = 256
PROJ_PIECE_N = 512
PROJ_SKEW = 2
_PROJ_KINDS = (["qscale"] + ["kv"] * 2 + ["silu"] + ["silu"] * 2 + ["forget"] * 2
               + ["id"] * 2 + ["silu"] * 2 + ["sigmoid"] * 4)
_F_TILE0 = _PROJ_KINDS.index("forget")
_F_TILES = _PROJ_KINDS.count("forget")
_QKV_TILES = 3

PERM_DIL = 16
PERM_TILE = 512
PERM_SLAB = PERM_TILE // PERM_DIL
ATTN_SUPER = PERM_DIL * ATTN_BLOCK
ATTN_BLOCKS_PER_ITER = 2
ATTN_SKEW = 2
MERGE_PIECE = 256
HGRN_CHUNKS_PER_ITER = 8
HGRN_SKEW = 1
LOG2E = math.log2(math.e)
LN2 = math.log(2.0)


def _sigmoid(x):
    return 1.0 / (1.0 + jnp.exp(-x))


def _mod_kernel(c_ref, w_ref, b_ref, o_ref):
    c = c_ref[...]
    sc = c * _sigmoid(c)
    o_ref[...] = jnp.dot(sc, w_ref[...], precision=lax.Precision.HIGHEST,
                         preferred_element_type=F32) + b_ref[...]


def _lower_bound_kernel(hl_ref, o_ref):
    hl = hl_ref[...]
    m = jnp.max(hl, axis=0, keepdims=True)
    e = jnp.exp(hl - m)
    o_ref[...] = e[0:1, :] / jnp.sum(e, axis=0, keepdims=True)


def _bias_table_kernel(rb_ref, bucket_ref, o_ref):
    bk = bucket_ref[...]
    no_prev = lax.broadcasted_iota(jnp.int32, bk.shape, 1) < ATTN_BLOCK
    for h in range(ATTN_HEADS):
        acc = jnp.full(bk.shape, NEG_INF, F32)
        for u in range(N_BUCKETS):
            acc = jnp.where(bk == u, rb_ref[u, h] * LOG2E, acc)
        o_ref[0, h] = acc
        o_ref[1, h] = jnp.where(no_prev, NEG_INF, acc)


def _bucket_tables():
    qi = np.arange(ATTN_BLOCK)[:, None]
    kj = np.arange(2 * ATTN_BLOCK)[None, :]
    delta = qi + ATTN_BLOCK - kj
    max_exact = N_BUCKETS // 2
    tabs = []
    for window, dilation in DILATED_PATTERNS:
        span = window // dilation
        band = (delta >= 0) & (delta <= span)
        dist = np.clip(delta, 0, None) * dilation
        n = dist.astype(np.float32)
        large = max_exact + (np.log(np.maximum(n, 1.0) / max_exact)
                             / math.log(MAX_DISTANCE / max_exact)
                             * (N_BUCKETS - max_exact)).astype(np.int32)
        large = np.minimum(large, N_BUCKETS - 1)
        bucket = np.where(dist < max_exact, dist, large)
        tab = np.where(band, bucket, -1).astype(np.int32)
        if dilation > 1:
            order = _gather_order(dilation)
            cols = np.concatenate([order, ATTN_BLOCK + order])
            tab = tab[order][:, cols]
        tabs.append(tab)
    return np.stack(tabs, 0)


def _gather_order(dilation):
    per_tile = PERM_DIL // dilation
    slab = np.arange(ATTN_BLOCK) // PERM_SLAB
    m = np.arange(ATTN_BLOCK) % PERM_SLAB
    if per_tile == 1:
        return slab * PERM_SLAB + m
    assert per_tile * PERM_SLAB == ATTN_BLOCK
    return per_tile * m + slab


def _inproj_kernel(x_ref, mod_ref, g_ref, lb_ref, w_ref, om_ref, of_ref, op_ref, h_ref, accl_ref):
    tm = x_ref.shape[0]
    x = x_ref[...]
    ms = jnp.mean(x * x, axis=-1, keepdims=True)
    y = x * lax.rsqrt(ms + EPS) * g_ref[...]
    shift = mod_ref[:, 0:D_MODEL]
    scale = mod_ref[:, D_MODEL:2 * D_MODEL]
    h_ref[...] = (y * (1.0 + scale) + shift).astype(BF16)

    n_sub = PROJ_TN // PROJ_PIECE_N
    slab = PROJ_PIECE_M // PERM_DIL

    def write_perm(mc, col0, acc, slot):
        for c in range(PROJ_PIECE_N // V7X_LANES):
            accl_ref[slot, c] = acc[:, c * V7X_LANES:(c + 1) * V7X_LANES]
            for r in range(PERM_DIL):
                rows = accl_ref[slot, c, pl.ds(r, slab, stride=PERM_DIL), :]
                row0 = r * PERM_SLAB + mc * slab
                op_ref[row0:row0 + slab, col0 + c * V7X_LANES:col0 + (c + 1) * V7X_LANES] = rows.astype(BF16)

    def epilogue(j, sub, mc, acc):
        kind = _PROJ_KINDS[j]
        rows = slice(mc * PROJ_PIECE_M, (mc + 1) * PROJ_PIECE_M)
        if kind == "forget":
            col0 = (j - _F_TILE0) * PROJ_TN + sub * PROJ_PIECE_N
            lb = lb_ref[:, col0:col0 + PROJ_PIECE_N]
            of_ref[rows, col0:col0 + PROJ_PIECE_N] = lb + (1.0 - lb) * _sigmoid(acc)
            return
        if kind == "qscale":
            acc = acc * (ATTN_HEAD_DIM ** -0.5 * LOG2E)
        elif kind == "silu":
            acc = acc * _sigmoid(acc)
        elif kind == "sigmoid":
            acc = _sigmoid(acc)
        jm = j if j < _F_TILE0 else j - _F_TILES
        col0 = jm * PROJ_TN + sub * PROJ_PIECE_N
        om_ref[rows, col0:col0 + PROJ_PIECE_N] = acc.astype(BF16)
        if j < _QKV_TILES:
            write_perm(mc, col0, acc, (sub * (tm // PROJ_PIECE_M) + mc) % accl_ref.shape[0])

    items = [(j, sub, mc) for j in range(len(_PROJ_KINDS)) for sub in range(n_sub)
             for mc in range(tm // PROJ_PIECE_M)]
    pending = {}
    for step in range(len(items) + PROJ_SKEW):
        if step < len(items):
            j, sub, mc = items[step]
            col0 = j * PROJ_TN + sub * PROJ_PIECE_N
            pending[step] = jnp.dot(h_ref[mc * PROJ_PIECE_M:(mc + 1) * PROJ_PIECE_M, :],
                                    w_ref[:, col0:col0 + PROJ_PIECE_N], preferred_element_type=F32)
        done = step - PROJ_SKEW
        if done >= 0:
            epilogue(*items[done], pending.pop(done))


def _pair_lanes(hp):
    return slice(hp * V7X_LANES, (hp + 1) * V7X_LANES)


def _attn_scores(q, k, bias_ref, hp, first, lane):
    blk = q.shape[0]
    low = lane < ATTN_HEAD_DIM
    zero = jnp.zeros_like(q)
    q2 = jnp.concatenate([jnp.where(low, q, zero), jnp.where(low, zero, q)], axis=0)
    s = lax.dot_general(q2, k, (((1,), (1,)), ((), ())), preferred_element_type=F32)
    s = s + bias_ref[first, pl.ds(2 * hp, 2)].reshape(2 * blk, 2 * blk)
    return s, jnp.max(s, axis=-1, keepdims=True)


def _attn_values(s, m, v, hp, lane, lse_all):
    blk = s.shape[0] // 2
    p = jnp.exp2(s - m).astype(BF16)
    h0, h1 = 2 * hp, 2 * hp + 1
    low = lane < ATTN_HEAD_DIM
    low_v = lax.broadcasted_iota(jnp.int32, v.shape, 1) < ATTN_HEAD_DIM
    one = jnp.ones_like(v)
    o0 = jnp.dot(p[:blk], jnp.where(low_v, v, one), preferred_element_type=F32)
    o1 = jnp.dot(p[blk:], jnp.where(low_v, one, v), preferred_element_type=F32)
    num = jnp.where(low, o0, o1)
    den_swapped = jnp.where(low, o1, o0)
    den = pltpu.roll(den_swapped, ATTN_HEAD_DIM, 1)
    is_h1 = lane == h1
    lse = jnp.where(is_h1, m[blk:], m[:blk]) * LN2 + jnp.log(jnp.where(is_h1, den_swapped, den))
    lse_all = jnp.where(jnp.logical_or(lane == h0, is_h1), lse, lse_all)
    return num / den, lse_all


def _attn_pipeline(blocks, bias_ref, lane):
    n_pairs = ATTN_HEADS // 2
    items = [(bi, hp) for bi in range(len(blocks)) for hp in range(n_pairs)]
    lse = [jnp.zeros(lane.shape, F32) for _ in blocks]
    pending = {}
    for step in range(len(items) + ATTN_SKEW):
        if step < len(items):
            bi, hp = items[step]
            b = blocks[bi]
            pending[step] = _attn_scores(b.get_q(hp), b.get_k(hp), bias_ref, hp, b.first, lane)
        done = step - ATTN_SKEW
        if done >= 0:
            bi, hp = items[done]
            b = blocks[bi]
            o_pair, lse[bi] = _attn_values(*pending.pop(done), b.get_v(hp), hp, lane, lse[bi])
            b.put_o(hp, o_pair)
            if hp == n_pairs - 1:
                b.put_lse(lse[bi])


def _attn_kernel(q_ref, kp_ref, kc_ref, vp_ref, vc_ref, bias_ref, o_ref, lse_ref, *scratch, dil):
    n = pl.program_id(1)
    blk_rows = ATTN_BLOCK
    lane = lax.broadcasted_iota(jnp.int32, (blk_rows, V7X_LANES), 1)
    prev_rows = kp_ref.shape[0]
    scratch = list(scratch)
    o_scr = scratch.pop(0) if dil > 1 else None
    if scratch:
        kext, vext = scratch
        kext[0:prev_rows] = kp_ref[...]
        kext[prev_rows:] = kc_ref[...]
        vext[0:prev_rows] = vp_ref[...]
        vext[prev_rows:] = vc_ref[...]
        k_prev, k_cur, v_prev, v_cur, cur_off = kext, kext, vext, vext, prev_rows
    else:
        k_prev, k_cur, v_prev, v_cur, cur_off = kp_ref, kc_ref, vp_ref, vc_ref, 0

    def make_block(blk):
        if dil == 1:
            slab = blk_rows
            offs = [blk * blk_rows]
            first = jnp.logical_and(n == 0, blk == 0)
        elif dil == PERM_DIL:
            slab = PERM_SLAB
            offs = [t * PERM_TILE + blk * PERM_SLAB for t in range(blk_rows // slab)]
            tok0 = [t * PERM_TILE + blk for t in range(blk_rows // slab)]
            first = n == 0
        else:
            slab = PERM_SLAB
            tile = lax.shift_right_logical(blk, 2)
            res = jnp.bitwise_and(blk, dil - 1)
            offs = [tile * PERM_TILE + (res + dil * j) * PERM_SLAB for j in range(blk_rows // slab)]
            tok0 = [tile * PERM_TILE + res + dil * j for j in range(blk_rows // slab)]
            first = jnp.logical_and(n == 0, tile == 0)
        offs = [pl.multiple_of(o, slab) for o in offs]

        def slabs(ref, hp, shift=0, rows=slab):
            return [ref[pl.ds(o + shift, rows), _pair_lanes(hp)] for o in offs]

        def put_o(hp, o_pair):
            if dil == 1:
                o_ref[pl.ds(offs[0], slab), _pair_lanes(hp)] = o_pair.astype(BF16)
                return
            for j, t0 in enumerate(tok0):
                o_scr[hp, pl.ds(t0, slab, stride=PERM_DIL), :] = o_pair[j * slab:(j + 1) * slab]

        def put_lse(lse_all):
            if dil == 1:
                lse_ref[pl.ds(offs[0], slab), :] = lse_all
                return
            for j, t0 in enumerate(tok0):
                lse_ref[pl.ds(t0, slab, stride=PERM_DIL), :] = lse_all[j * slab:(j + 1) * slab]

        if dil == 1:
            get_k = lambda hp: slabs(k_cur, hp, rows=2 * slab)[0]
            get_v = lambda hp: slabs(v_cur, hp, rows=2 * slab)[0]
        else:
            get_k = lambda hp: jnp.concatenate(slabs(k_prev, hp) + slabs(k_cur, hp, cur_off), axis=0)
            get_v = lambda hp: jnp.concatenate(slabs(v_prev, hp) + slabs(v_cur, hp, cur_off), axis=0)
        return types.SimpleNamespace(
            get_q=lambda hp: jnp.concatenate(slabs(q_ref, hp), axis=0), get_k=get_k, get_v=get_v,
            put_o=put_o, put_lse=put_lse, first=first.astype(jnp.int32))

    def body(it, carry):
        _attn_pipeline([make_block(it * ATTN_BLOCKS_PER_ITER + u) for u in range(ATTN_BLOCKS_PER_ITER)],
                       bias_ref, lane)
        return carry

    lax.fori_loop(0, ATTN_SUPER // ATTN_BLOCK // ATTN_BLOCKS_PER_ITER, body, 0)
    if dil > 1:
        for hp in range(ATTN_HEADS // 2):
            o_ref[:, _pair_lanes(hp)] = o_scr[hp].astype(BF16)


def _roll_rows(a, shift):
    rows = a.shape[0]
    if abs(shift) >= V7X_SUBLANES:
        return pltpu.roll(a, shift % rows, 0)
    grouped = a.reshape(rows // V7X_SUBLANES, V7X_SUBLANES, a.shape[1])
    return pltpu.roll(grouped, shift % V7X_SUBLANES, 1).reshape(a.shape)


def _hgrn_kernel(q_ref, f_ref, i_ref, z_ref, g_ref, o_ref, state_ref, *, chunk, n_chunks):
    @pl.when(pl.program_id(1) == 0)
    def _():
        state_ref[...] = jnp.zeros_like(state_ref)

    C = chunk
    n_levels = C.bit_length() - 1
    g_on = g_ref[...]
    nt = (((1,), (1,)), ((), ()))

    def chunk_body(ci, carry):
        row = lax.broadcasted_iota(jnp.int32, (C, HGRN_KEY_DIM), 0)
        odds = [jnp.bitwise_and(row, 1 << lvl) != 0 for lvl in range(n_levels)]
        tt = lax.broadcasted_iota(jnp.int32, (C, C), 0)
        ss = lax.broadcasted_iota(jnp.int32, (C, C), 1)
        owner = jnp.where(tt > ss, 32 - lax.clz(jnp.bitwise_xor(tt, ss)),
                          jnp.where(tt == ss, 0, -1))
        owned = [owner == lvl for lvl in range(n_levels + 1)]

        def chunk_rows(u):
            return pl.ds(pl.multiple_of((ci * HGRN_CHUNKS_PER_ITER + u) * C, C), C)

        def scan(u, h):
            rows = chunk_rows(u)
            hs = slice(h * HGRN_KEY_DIM, (h + 1) * HGRN_KEY_DIM)
            f = f_ref[rows, hs]
            q = q_ref[rows, hs]
            k = (1.0 - f).astype(BF16)
            a = jnp.where(owned[0], lax.dot_general(q, k, nt, preferred_element_type=F32), 0.0)
            x = jnp.where(odds[0], f, 1.0)
            y = jnp.where(odds[0], 1.0, f)
            for lvl in range(n_levels):
                m = 1 << lvl
                e = x.astype(BF16)
                pm = lax.dot_general(q * e, k * e, nt, preferred_element_type=F32)
                a = jnp.where(owned[lvl + 1], pm, a)
                tot = x * y if lvl else f
                if 2 * m == V7X_SUBLANES:
                    partner = _roll_rows(tot, m)
                else:
                    partner = jnp.where(odds[lvl], _roll_rows(tot, m), _roll_rows(tot, -m))
                z = x * partner
                keep = odds[lvl] == odds[lvl + 1] if lvl + 1 < n_levels else jnp.logical_not(odds[lvl])
                x, y = jnp.where(keep, z, y), jnp.where(keep, y, z)
            return a.astype(BF16), q * y.astype(BF16), k * x.astype(BF16), x[0:1, :] * y[0:1, :]

        def finish(u, h, a, q_dec, k_dec, decay):
            rows = chunk_rows(u)
            hs = slice(h * HGRN_KEY_DIM, (h + 1) * HGRN_KEY_DIM)
            v = i_ref[rows, hs]
            st = state_ref[h]
            o = (jnp.dot(a, v, preferred_element_type=F32)
                 + lax.dot_general(q_dec, st.astype(BF16), nt, preferred_element_type=F32))
            upd = lax.dot_general(v, k_dec, (((0,), (0,)), ((), ())), preferred_element_type=F32)
            state_ref[h] = st * decay + upd
            ms = jnp.mean(o * o, axis=-1, keepdims=True)
            y = o * lax.rsqrt(ms + EPS) * g_on
            o_ref[rows, hs] = (y * z_ref[rows, hs].astype(F32)).astype(BF16)

        items = [(u, h) for u in range(HGRN_CHUNKS_PER_ITER) for h in range(HGRN_HEADS)]
        pending = {}
        for step in range(len(items) + HGRN_SKEW):
            if step < len(items):
                pending[step] = scan(*items[step])
            done = step - HGRN_SKEW
            if done >= 0:
                finish(*items[done], *pending.pop(done))
        return carry

    assert n_chunks % HGRN_CHUNKS_PER_ITER == 0
    lax.fori_loop(0, n_chunks // HGRN_CHUNKS_PER_ITER, chunk_body, 0)


def _merge_kernel(o1_ref, o2_ref, o3_ref, l1_ref, l2_ref, l3_ref, za_ref, ob_ref,
                  sga_ref, sgb_ref, x_ref, mod_ref, wa_ref, wb_ref, wo_ref, fg_ref,
                  ex_ref, out_ref):
    ex = ex_ref[...]
    gate = mod_ref[:, 2 * D_MODEL:3 * D_MODEL]

    def expand(w):
        hi = w.astype(BF16)
        lo = (w - hi.astype(F32)).astype(BF16)
        return jnp.dot(jnp.concatenate([hi, lo], axis=1), ex, preferred_element_type=F32)

    def mix(rows):
        l1, l2, l3 = l1_ref[rows, :], l2_ref[rows, :], l3_ref[rows, :]
        mx = jnp.maximum(jnp.maximum(l1, l2), l3)
        e1, e2, e3 = jnp.exp(l1 - mx), jnp.exp(l2 - mx), jnp.exp(l3 - mx)
        inv = 1.0 / (e1 + e2 + e3)
        oa = (expand(e1 * inv) * o1_ref[rows, :].astype(F32)
              + expand(e2 * inv) * o2_ref[rows, :].astype(F32)
              + expand(e3 * inv) * o3_ref[rows, :].astype(F32))
        return (oa * za_ref[rows, :].astype(F32)).astype(BF16)

    def branches(rows, oa):
        ya = jnp.dot(oa, wa_ref[...], preferred_element_type=F32)
        yb = jnp.dot(ob_ref[rows, :], wb_ref[...], preferred_element_type=F32)
        return (sga_ref[rows, :].astype(F32) * ya + sgb_ref[rows, :].astype(F32) * yb).astype(BF16)

    def project(rows, y):
        z = jnp.dot(y, wo_ref[...], preferred_element_type=F32)
        xo = x_ref[rows, :] + gate * z
        ms = jnp.mean(xo * xo, axis=-1, keepdims=True)
        out_ref[rows, :] = xo * lax.rsqrt(ms + EPS) * fg_ref[...]

    pieces = [slice(r, r + MERGE_PIECE) for r in range(0, x_ref.shape[0], MERGE_PIECE)]
    oas = [mix(rows) for rows in pieces]
    ys = [branches(rows, oa) for rows, oa in zip(pieces, oas)]
    for rows, y in zip(pieces, ys):
        project(rows, y)


def _cparams(sem, vmem_mb):
    return pltpu.CompilerParams(dimension_semantics=sem,
                                vmem_limit_bytes=vmem_mb * 1024 * 1024)


def kernel(x, c, w_ada, b_ada, norm_g, w_in, hgrn_onorm_g, w_branch_a, w_branch_b, w_out,
           rel_bias, hgrn_lb, final_g):
    B, S, D = x.shape
    assert D == D_MODEL and w_ada.shape[0] == 1, "single-layer kernel"
    N = B * S
    x2 = x.reshape(N, D)

    c8 = jnp.pad(c, ((0, 8 - B), (0, 0)))
    mod = pl.pallas_call(
        _mod_kernel,
        grid=(3 * D // 512,),
        in_specs=[pl.BlockSpec((8, D), lambda j: (0, 0)),
                  pl.BlockSpec((D, 512), lambda j: (0, j)),
                  pl.BlockSpec((1, 512), lambda j: (0, j))],
        out_specs=pl.BlockSpec((8, 512), lambda j: (0, j)),
        out_shape=jax.ShapeDtypeStruct((8, 3 * D), F32),
        name="adaln_mod",
    )(c8, w_ada[0], b_ada[0].reshape(1, 3 * D))
    mod3 = mod.reshape(8, 1, 3 * D)

    lb = pl.pallas_call(
        _lower_bound_kernel,
        out_shape=jax.ShapeDtypeStruct((1, HGRN_WIDTH), F32),
        name="hgrn_lower_bound",
    )(hgrn_lb)

    n_pat = len(DILATED_PATTERNS)
    bias_tab = pl.pallas_call(
        _bias_table_kernel,
        grid=(n_pat,),
        in_specs=[pl.BlockSpec(memory_space=pltpu.SMEM),
                  pl.BlockSpec((None, ATTN_BLOCK, 2 * ATTN_BLOCK), lambda g: (g, 0, 0))],
        out_specs=pl.BlockSpec((None, 2, ATTN_HEADS, ATTN_BLOCK, 2 * ATTN_BLOCK),
                               lambda g: (g, 0, 0, 0, 0)),
        out_shape=jax.ShapeDtypeStruct((n_pat, 2, ATTN_HEADS, ATTN_BLOCK, 2 * ATTN_BLOCK), F32),
        name="rel_bias_table",
    )(rel_bias, jnp.asarray(_bucket_tables()))

    tm = PERM_TILE
    tiles_per_b = S // tm
    in_width = len(_PROJ_KINDS) * PROJ_TN
    assert PROJ_TN == ATTN_WIDTH and w_in.shape[2] == in_width
    resident = dict(pipeline_mode=pl.Buffered(1))
    main, fgate, qkv_p = pl.pallas_call(
        _inproj_kernel,
        grid=(N // tm,),
        in_specs=[pl.BlockSpec((tm, D), lambda i: (i, 0)),
                  pl.BlockSpec((None, 1, 3 * D), lambda i: (i // tiles_per_b, 0, 0)),
                  pl.BlockSpec((1, D), lambda i: (0, 0)),
                  pl.BlockSpec((1, HGRN_WIDTH), lambda i: (0, 0)),
                  pl.BlockSpec((D, in_width), lambda i: (0, 0), **resident)],
        out_specs=[pl.BlockSpec((tm, MAIN_WIDTH), lambda i: (i, 0)),
                   pl.BlockSpec((tm, HGRN_WIDTH), lambda i: (i, 0)),
                   pl.BlockSpec((tm, _QKV_TILES * ATTN_WIDTH), lambda i: (i, 0))],
        out_shape=[jax.ShapeDtypeStruct((N, MAIN_WIDTH), BF16),
                   jax.ShapeDtypeStruct((N, HGRN_WIDTH), F32),
                   jax.ShapeDtypeStruct((N, _QKV_TILES * ATTN_WIDTH), BF16)],
        scratch_shapes=[pltpu.VMEM((tm, D), BF16),
                        pltpu.VMEM((2, PROJ_PIECE_N // V7X_LANES, PROJ_PIECE_M, V7X_LANES), F32)],
        compiler_params=_cparams(("arbitrary",), 56),
        name="inproj",
    )(x2, mod3, norm_g[0].reshape(1, D), lb, w_in[0].astype(BF16))

    qa_t = _MAIN_COLS["qa"][0] // ATTN_WIDTH
    ka_t = _MAIN_COLS["ka"][0] // ATTN_WIDTH
    va_t = _MAIN_COLS["va"][0] // ATTN_WIDTH
    attn_outs = []
    main_b = main.reshape(B, S, MAIN_WIDTH)
    qkv_pv = qkv_p.reshape(B, S, _QKV_TILES * ATTN_WIDTH)
    n_super = S // ATTN_SUPER
    for g, (window, dil) in enumerate(DILATED_PATTERNS):
        assert window // dil == ATTN_BLOCK
        src, tiles = (main_b, (qa_t, ka_t, va_t)) if dil == 1 else (qkv_pv, (0, 1, 2))
        prev_rows = {1: ATTN_BLOCK, PERM_DIL: ATTN_SUPER}.get(dil, PERM_TILE)
        per_step = ATTN_SUPER // prev_rows

        def cur_spec(t):
            return pl.BlockSpec((None, ATTN_SUPER, ATTN_WIDTH), lambda b, n, t=t: (b, n, t))

        def prev_spec(t, prev_rows=prev_rows, per_step=per_step):
            return pl.BlockSpec((None, prev_rows, ATTN_WIDTH),
                                lambda b, n, t=t: (b, jnp.maximum(n * per_step - 1, 0), t))

        scratch = []
        if dil > 1:
            scratch += [pltpu.VMEM((ATTN_WIDTH // V7X_LANES, ATTN_SUPER, V7X_LANES), F32)]
        if dil != PERM_DIL:
            scratch += [pltpu.VMEM((prev_rows + ATTN_SUPER, ATTN_WIDTH), BF16)] * 2
        o_g, lse_g = pl.pallas_call(
            functools.partial(_attn_kernel, dil=dil),
            grid=(B, n_super),
            in_specs=[cur_spec(tiles[0]), prev_spec(tiles[1]), cur_spec(tiles[1]),
                      prev_spec(tiles[2]), cur_spec(tiles[2]),
                      pl.BlockSpec((None, 2, ATTN_HEADS, ATTN_BLOCK, 2 * ATTN_BLOCK),
                                   lambda b, n, g=g: (g, 0, 0, 0, 0))],
            out_specs=[pl.BlockSpec((None, ATTN_SUPER, ATTN_WIDTH), lambda b, n: (b, n, 0)),
                       pl.BlockSpec((None, ATTN_SUPER, V7X_LANES), lambda b, n: (b, n, 0))],
            out_shape=[jax.ShapeDtypeStruct((B, S, ATTN_WIDTH), BF16),
                       jax.ShapeDtypeStruct((B, S, V7X_LANES), F32)],
            scratch_shapes=scratch,
            compiler_params=_cparams(("arbitrary", "arbitrary"), 48),
            name=f"dilated_attn_d{dil}",
        )(src, src, src, src, src, bias_tab)
        attn_outs.append((o_g.reshape(N, ATTN_WIDTH), lse_g.reshape(N, V7X_LANES)))

    th = 1024
    chunk = 64
    hw_t = HGRN_WIDTH
    qb_t = _MAIN_COLS["qb"][0] // hw_t
    ib_t = _MAIN_COLS["ib"][0] // hw_t
    zb_t = _MAIN_COLS["zb"][0] // hw_t
    ob = pl.pallas_call(
        functools.partial(_hgrn_kernel, chunk=chunk, n_chunks=th // chunk),
        grid=(B, S // th),
        in_specs=[pl.BlockSpec((None, th, hw_t), lambda b, s: (b, s, qb_t)),
                  pl.BlockSpec((None, th, hw_t), lambda b, s: (b, s, 0)),
                  pl.BlockSpec((None, th, hw_t), lambda b, s: (b, s, ib_t)),
                  pl.BlockSpec((None, th, hw_t), lambda b, s: (b, s, zb_t)),
                  pl.BlockSpec((1, HGRN_VAL_DIM), lambda b, s: (0, 0))],
        out_specs=pl.BlockSpec((None, th, hw_t), lambda b, s: (b, s, 0)),
        out_shape=jax.ShapeDtypeStruct((B, S, hw_t), BF16),
        scratch_shapes=[pltpu.VMEM((HGRN_HEADS, HGRN_VAL_DIM, HGRN_KEY_DIM), F32)],
        compiler_params=_cparams(("arbitrary", "arbitrary"), 32),
        name="hgrn2",
    )(main_b, fgate.reshape(B, S, hw_t), main_b, main_b, hgrn_onorm_g[0].reshape(1, HGRN_VAL_DIM))
    ob = ob.reshape(N, hw_t)

    tk = 2 * MERGE_PIECE
    tiles_per_b5 = S // tk
    za_t = _MAIN_COLS["za"][0] // ATTN_WIDTH
    ga_t = _MAIN_COLS["ga"][0] // D
    gb_t = _MAIN_COLS["gb"][0] // D
    expand_mat = np.zeros((2 * V7X_LANES, ATTN_WIDTH), np.float32)
    for h in range(ATTN_HEADS):
        expand_mat[h, h * ATTN_HEAD_DIM:(h + 1) * ATTN_HEAD_DIM] = 1.0
        expand_mat[V7X_LANES + h, h * ATTN_HEAD_DIM:(h + 1) * ATTN_HEAD_DIM] = 1.0
    (o1, l1), (o2, l2), (o3, l3) = attn_outs
    row_spec = lambda w, t=0: pl.BlockSpec((tk, w), lambda i, t=t: (i, t))
    full_spec = lambda a, b: pl.BlockSpec((a, b), lambda i: (0, 0))
    out = pl.pallas_call(
        _merge_kernel,
        grid=(N // tk,),
        in_specs=[row_spec(ATTN_WIDTH), row_spec(ATTN_WIDTH), row_spec(ATTN_WIDTH),
                  row_spec(V7X_LANES), row_spec(V7X_LANES), row_spec(V7X_LANES),
                  row_spec(ATTN_WIDTH, za_t), row_spec(HGRN_WIDTH),
                  row_spec(D, ga_t), row_spec(D, gb_t), row_spec(D),
                  pl.BlockSpec((None, 1, 3 * D), lambda i: (i // tiles_per_b5, 0, 0)),
                  full_spec(ATTN_WIDTH, D), full_spec(HGRN_WIDTH, D), full_spec(D, D),
                  full_spec(1, D), full_spec(2 * V7X_LANES, ATTN_WIDTH)],
        out_specs=pl.BlockSpec((tk, D), lambda i: (i, 0)),
        out_shape=jax.ShapeDtypeStruct((N, D), F32),
        compiler_params=_cparams(("arbitrary",), 40),
        name="gated_merge",
    )(o1, o2, o3, l1, l2, l3, main, ob, main, main, x2, mod3,
      w_branch_a[0].astype(BF16), w_branch_b[0].astype(BF16), w_out[0].astype(BF16),
      final_g.reshape(1, D), jnp.asarray(expand_mat, BF16))
    return out.reshape(B, S, D)
```

```python
import functools
import math
import types

import numpy as np
import jax
import jax.numpy as jnp
from jax import lax
from jax.experimental import pallas as pl
from jax.experimental.pallas import tpu as pltpu

D_MODEL = 1024
ATTN_HEADS = 8
ATTN_HEAD_DIM = 64
ATTN_WIDTH = ATTN_HEADS * ATTN_HEAD_DIM
DILATED_PATTERNS = ((128, 1), (512, 4), (2048, 16))
ATTN_BLOCK = 128
N_BUCKETS = 32
MAX_DISTANCE = 2048
NEG_INF = -1e30
HGRN_HEADS = 8
HGRN_KEY_DIM = 128
HGRN_VAL_DIM = 128
HGRN_WIDTH = HGRN_HEADS * HGRN_VAL_DIM
EPS = 1e-6

V7X_LANES = 128
V7X_SUBLANES = 8

F32 = jnp.float32
BF16 = jnp.bfloat16

_MAIN_COLS = {}
_off = 0
for _name, _w in (("qa", ATTN_WIDTH), ("ka", ATTN_WIDTH), ("va", ATTN_WIDTH), ("za", ATTN_WIDTH),
                  ("qb", HGRN_WIDTH), ("ib", HGRN_WIDTH), ("zb", HGRN_WIDTH),
                  ("ga", D_MODEL), ("gb", D_MODEL)):
    _MAIN_COLS[_name] = (_off, _w)
    _off += _w
MAIN_WIDTH = _off
PROJ_TN = 512
PROJ_PIECE_M = 256
PROJ_PIECE_N = 512
PROJ_SKEW = 2
_PROJ_KINDS = (["qscale"] + ["kv"] * 2 + ["silu"] + ["silu"] * 2 + ["forget"] * 2
               + ["id"] * 2 + ["silu"] * 2 + ["sigmoid"] * 4)
_F_TILE0 = _PROJ_KINDS.index("forget")
_F_TILES = _PROJ_KINDS.count("forget")
_QKV_TILES = 3

PERM_DIL = 16
PERM_TILE = 512
PERM_SLAB = PERM_TILE // PERM_DIL
ATTN_SUPER = PERM_DIL * ATTN_BLOCK
ATTN_BLOCKS_PER_ITER = 2
ATTN_SKEW = 2
MERGE_PIECE = 256
MERGE_TILE = 2 * MERGE_PIECE
HGRN_TILE = 512
HGRN_CHUNK = 64
HGRN_CHUNKS_PER_ITER = 8
HGRN_SKEW = 1
LOG2E = math.log2(math.e)
LN2 = math.log(2.0)


def _sigmoid(x):
    return 1.0 / (1.0 + jnp.exp(-x))


def _mod_kernel(c_ref, w_ref, b_ref, o_ref):
    c = c_ref[...]
    sc = c * _sigmoid(c)
    o_ref[...] = jnp.dot(sc, w_ref[...], precision=lax.Precision.HIGHEST,
                         preferred_element_type=F32) + b_ref[...]


def _lower_bound_kernel(hl_ref, o_ref):
    hl = hl_ref[...]
    m = jnp.max(hl, axis=0, keepdims=True)
    e = jnp.exp(hl - m)
    o_ref[...] = e[0:1, :] / jnp.sum(e, axis=0, keepdims=True)


def _bias_table_kernel(rb_ref, bucket_ref, o_ref):
    bk = bucket_ref[...]
    no_prev = lax.broadcasted_iota(jnp.int32, bk.shape, 1) < ATTN_BLOCK
    for h in range(ATTN_HEADS):
        acc = jnp.full(bk.shape, NEG_INF, F32)
        for u in range(N_BUCKETS):
            acc = jnp.where(bk == u, rb_ref[u, h] * LOG2E, acc)
        o_ref[0, h] = acc
        o_ref[1, h] = jnp.where(no_prev, NEG_INF, acc)


def _bucket_tables():
    qi = np.arange(ATTN_BLOCK)[:, None]
    kj = np.arange(2 * ATTN_BLOCK)[None, :]
    delta = qi + ATTN_BLOCK - kj
    max_exact = N_BUCKETS // 2
    tabs = []
    for window, dilation in DILATED_PATTERNS:
        span = window // dilation
        band = (delta >= 0) & (delta <= span)
        dist = np.clip(delta, 0, None) * dilation
        n = dist.astype(np.float32)
        large = max_exact + (np.log(np.maximum(n, 1.0) / max_exact)
                             / math.log(MAX_DISTANCE / max_exact)
                             * (N_BUCKETS - max_exact)).astype(np.int32)
        large = np.minimum(large, N_BUCKETS - 1)
        bucket = np.where(dist < max_exact, dist, large)
        tab = np.where(band, bucket, -1).astype(np.int32)
        if dilation > 1:
            order = _gather_order(dilation)
            cols = np.concatenate([order, ATTN_BLOCK + order])
            tab = tab[order][:, cols]
        tabs.append(tab)
    return np.stack(tabs, 0)


def _gather_order(dilation):
    per_tile = PERM_DIL // dilation
    slab = np.arange(ATTN_BLOCK) // PERM_SLAB
    m = np.arange(ATTN_BLOCK) % PERM_SLAB
    if per_tile == 1:
        return slab * PERM_SLAB + m
    assert per_tile * PERM_SLAB == ATTN_BLOCK
    return per_tile * m + slab


def _inproj_kernel(x_ref, mod_ref, g_ref, lb_ref, w_ref, om_ref, of_ref, op_ref, h_ref, accl_ref):
    tm = x_ref.shape[0]
    x = x_ref[...]
    ms = jnp.mean(x * x, axis=-1, keepdims=True)
    y = x * lax.rsqrt(ms + EPS) * g_ref[...]
    shift = mod_ref[:, 0:D_MODEL]
    scale = mod_ref[:, D_MODEL:2 * D_MODEL]
    h_ref[...] = (y * (1.0 + scale) + shift).astype(BF16)

    n_sub = PROJ_TN // PROJ_PIECE_N
    slab = PROJ_PIECE_M // PERM_DIL

    def write_perm(mc, col0, acc, slot):
        for c in range(PROJ_PIECE_N // V7X_LANES):
            accl_ref[slot, c] = acc[:, c * V7X_LANES:(c + 1) * V7X_LANES]
            for r in range(PERM_DIL):
                rows = accl_ref[slot, c, pl.ds(r, slab, stride=PERM_DIL), :]
                row0 = r * PERM_SLAB + mc * slab
                op_ref[row0:row0 + slab, col0 + c * V7X_LANES:col0 + (c + 1) * V7X_LANES] = rows.astype(BF16)

    def epilogue(j, sub, mc, acc):
        kind = _PROJ_KINDS[j]
        rows = slice(mc * PROJ_PIECE_M, (mc + 1) * PROJ_PIECE_M)
        if kind == "forget":
            col0 = (j - _F_TILE0) * PROJ_TN + sub * PROJ_PIECE_N
            lb = lb_ref[:, col0:col0 + PROJ_PIECE_N]
            of_ref[rows, col0:col0 + PROJ_PIECE_N] = lb + (1.0 - lb) * _sigmoid(acc)
            return
        if kind == "qscale":
            acc = acc * (ATTN_HEAD_DIM ** -0.5 * LOG2E)
        elif kind == "silu":
            acc = acc * _sigmoid(acc)
        elif kind == "sigmoid":
            acc = _sigmoid(acc)
        jm = j if j < _F_TILE0 else j - _F_TILES
        col0 = jm * PROJ_TN + sub * PROJ_PIECE_N
        om_ref[rows, col0:col0 + PROJ_PIECE_N] = acc.astype(BF16)
        if j < _QKV_TILES:
            write_perm(mc, col0, acc, (sub * (tm // PROJ_PIECE_M) + mc) % accl_ref.shape[0])

    items = [(j, sub, mc) for j in range(len(_PROJ_KINDS)) for sub in range(n_sub)
             for mc in range(tm // PROJ_PIECE_M)]
    pending = {}
    for step in range(len(items) + PROJ_SKEW):
        if step < len(items):
            j, sub, mc = items[step]
            col0 = j * PROJ_TN + sub * PROJ_PIECE_N
            pending[step] = jnp.dot(h_ref[mc * PROJ_PIECE_M:(mc + 1) * PROJ_PIECE_M, :],
                                    w_ref[:, col0:col0 + PROJ_PIECE_N], preferred_element_type=F32)
        done = step - PROJ_SKEW
        if done >= 0:
            epilogue(*items[done], pending.pop(done))


def _pair_lanes(hp):
    return slice(hp * V7X_LANES, (hp + 1) * V7X_LANES)


def _attn_scores(q, k, bias_ref, hp, first, lane):
    blk = q.shape[0]
    low = lane < ATTN_HEAD_DIM
    zero = jnp.zeros_like(q)
    q2 = jnp.concatenate([jnp.where(low, q, zero), jnp.where(low, zero, q)], axis=0)
    s = lax.dot_general(q2, k, (((1,), (1,)), ((), ())), preferred_element_type=F32)
    s = s + bias_ref[first, pl.ds(2 * hp, 2)].reshape(2 * blk, 2 * blk)
    return s, jnp.max(s, axis=-1, keepdims=True)


def _attn_values(s, m, v, hp, lane, lse_all):
    blk = s.shape[0] // 2
    p = jnp.exp2(s - m).astype(BF16)
    h0, h1 = 2 * hp, 2 * hp + 1
    low = lane < ATTN_HEAD_DIM
    low_v = lax.broadcasted_iota(jnp.int32, v.shape, 1) < ATTN_HEAD_DIM
    one = jnp.ones_like(v)
    o0 = jnp.dot(p[:blk], jnp.where(low_v, v, one), preferred_element_type=F32)
    o1 = jnp.dot(p[blk:], jnp.where(low_v, one, v), preferred_element_type=F32)
    num = jnp.where(low, o0, o1)
    den_swapped = jnp.where(low, o1, o0)
    den = pltpu.roll(den_swapped, ATTN_HEAD_DIM, 1)
    is_h1 = lane == h1
    lse = jnp.where(is_h1, m[blk:], m[:blk]) * LN2 + jnp.log(jnp.where(is_h1, den_swapped, den))
    lse_all = jnp.where(jnp.logical_or(lane == h0, is_h1), lse, lse_all)
    return num / den, lse_all


def _attn_pipeline(blocks, bias_ref, lane):
    n_pairs = ATTN_HEADS // 2
    items = [(bi, hp) for bi in range(len(blocks)) for hp in range(n_pairs)]
    lse = [jnp.zeros(lane.shape, F32) for _ in blocks]
    pending = {}
    for step in range(len(items) + ATTN_SKEW):
        if step < len(items):
            bi, hp = items[step]
            b = blocks[bi]
            pending[step] = _attn_scores(b.get_q(hp), b.get_k(hp), bias_ref, hp, b.first, lane)
        done = step - ATTN_SKEW
        if done >= 0:
            bi, hp = items[done]
            b = blocks[bi]
            o_pair, lse[bi] = _attn_values(*pending.pop(done), b.get_v(hp), hp, lane, lse[bi])
            b.put_o(hp, o_pair)
            if hp == n_pairs - 1:
                b.put_lse(lse[bi])


def _attn_kernel(q_ref, kp_ref, kc_ref, vp_ref, vc_ref, bias_ref, o_ref, lse_ref, *scratch, dil):
    n = pl.program_id(1)
    blk_rows = ATTN_BLOCK
    lane = lax.broadcasted_iota(jnp.int32, (blk_rows, V7X_LANES), 1)
    prev_rows = kp_ref.shape[0]
    scratch = list(scratch)
    o_scr = scratch.pop(0) if dil > 1 else None
    if scratch:
        kext, vext = scratch
        kext[0:prev_rows] = kp_ref[...]
        kext[prev_rows:] = kc_ref[...]
        vext[0:prev_rows] = vp_ref[...]
        vext[prev_rows:] = vc_ref[...]
        k_prev, k_cur, v_prev, v_cur, cur_off = kext, kext, vext, vext, prev_rows
    else:
        k_prev, k_cur, v_prev, v_cur, cur_off = kp_ref, kc_ref, vp_ref, vc_ref, 0

    def make_block(blk):
        if dil == 1:
            slab = blk_rows
            offs = [blk * blk_rows]
            first = jnp.logical_and(n == 0, blk == 0)
        elif dil == PERM_DIL:
            slab = PERM_SLAB
            offs = [t * PERM_TILE + blk * PERM_SLAB for t in range(blk_rows // slab)]
            tok0 = [t * PERM_TILE + blk for t in range(blk_rows // slab)]
            first = n == 0
        else:
            slab = PERM_SLAB
            tile = lax.shift_right_logical(blk, 2)
            res = jnp.bitwise_and(blk, dil - 1)
            offs = [tile * PERM_TILE + (res + dil * j) * PERM_SLAB for j in range(blk_rows // slab)]
            tok0 = [tile * PERM_TILE + res + dil * j for j in range(blk_rows // slab)]
            first = jnp.logical_and(n == 0, tile == 0)
        offs = [pl.multiple_of(o, slab) for o in offs]

        def slabs(ref, hp, shift=0, rows=slab):
            return [ref[pl.ds(o + shift, rows), _pair_lanes(hp)] for o in offs]

        def put_o(hp, o_pair):
            if dil == 1:
                o_ref[pl.ds(offs[0], slab), _pair_lanes(hp)] = o_pair.astype(BF16)
                return
            for j, t0 in enumerate(tok0):
                o_scr[hp, pl.ds(t0, slab, stride=PERM_DIL), :] = o_pair[j * slab:(j + 1) * slab]

        def put_lse(lse_all):
            if dil == 1:
                lse_ref[pl.ds(offs[0], slab), :] = lse_all
                return
            for j, t0 in enumerate(tok0):
                lse_ref[pl.ds(t0, slab, stride=PERM_DIL), :] = lse_all[j * slab:(j + 1) * slab]

        if dil == 1:
            get_k = lambda hp: slabs(k_cur, hp, rows=2 * slab)[0]
            get_v = lambda hp: slabs(v_cur, hp, rows=2 * slab)[0]
        else:
            get_k = lambda hp: jnp.concatenate(slabs(k_prev, hp) + slabs(k_cur, hp, cur_off), axis=0)
            get_v = lambda hp: jnp.concatenate(slabs(v_prev, hp) + slabs(v_cur, hp, cur_off), axis=0)
        return types.SimpleNamespace(
            get_q=lambda hp: jnp.concatenate(slabs(q_ref, hp), axis=0), get_k=get_k, get_v=get_v,
            put_o=put_o, put_lse=put_lse, first=first.astype(jnp.int32))

    def body(it, carry):
        _attn_pipeline([make_block(it * ATTN_BLOCKS_PER_ITER + u) for u in range(ATTN_BLOCKS_PER_ITER)],
                       bias_ref, lane)
        return carry

    lax.fori_loop(0, ATTN_SUPER // ATTN_BLOCK // ATTN_BLOCKS_PER_ITER, body, 0)
    if dil > 1:
        for hp in range(ATTN_HEADS // 2):
            o_ref[:, _pair_lanes(hp)] = o_scr[hp].astype(BF16)


def _roll_rows(a, shift):
    rows = a.shape[0]
    if abs(shift) >= V7X_SUBLANES:
        return pltpu.roll(a, shift % rows, 0)
    grouped = a.reshape(rows // V7X_SUBLANES, V7X_SUBLANES, a.shape[1])
    return pltpu.roll(grouped, shift % V7X_SUBLANES, 1).reshape(a.shape)


def _hgrn_kernel(q_ref, f_ref, i_ref, z_ref, g_ref, o_ref, state_ref, *, chunk, n_chunks):
    @pl.when(pl.program_id(1) == 0)
    def _():
        state_ref[...] = jnp.zeros_like(state_ref)

    C = chunk
    n_levels = C.bit_length() - 1
    g_on = g_ref[...]
    nt = (((1,), (1,)), ((), ()))

    def chunk_body(ci, carry):
        row = lax.broadcasted_iota(jnp.int32, (C, HGRN_KEY_DIM), 0)
        odds = [jnp.bitwise_and(row, 1 << lvl) != 0 for lvl in range(n_levels)]
        tt = lax.broadcasted_iota(jnp.int32, (C, C), 0)
        ss = lax.broadcasted_iota(jnp.int32, (C, C), 1)
        owner = jnp.where(tt > ss, 32 - lax.clz(jnp.bitwise_xor(tt, ss)),
                          jnp.where(tt == ss, 0, -1))
        owned = [owner == lvl for lvl in range(n_levels + 1)]

        def chunk_rows(u):
            return pl.ds(pl.multiple_of((ci * HGRN_CHUNKS_PER_ITER + u) * C, C), C)

        def scan(u, h):
            rows = chunk_rows(u)
            hs = slice(h * HGRN_KEY_DIM, (h + 1) * HGRN_KEY_DIM)
            f = f_ref[rows, hs]
            q = q_ref[rows, hs]
            k = (1.0 - f).astype(BF16)
            a = jnp.where(owned[0], lax.dot_general(q, k, nt, preferred_element_type=F32), 0.0)
            x = jnp.where(odds[0], f, 1.0)
            y = jnp.where(odds[0], 1.0, f)
            for lvl in range(n_levels):
                m = 1 << lvl
                e = x.astype(BF16)
                pm = lax.dot_general(q * e, k * e, nt, preferred_element_type=F32)
                a = jnp.where(owned[lvl + 1], pm, a)
                tot = x * y if lvl else f
                if 2 * m == V7X_SUBLANES:
                    partner = _roll_rows(tot, m)
                else:
                    partner = jnp.where(odds[lvl], _roll_rows(tot, m), _roll_rows(tot, -m))
                z = x * partner
                keep = odds[lvl] == odds[lvl + 1] if lvl + 1 < n_levels else jnp.logical_not(odds[lvl])
                x, y = jnp.where(keep, z, y), jnp.where(keep, y, z)
            return a.astype(BF16), q * y.astype(BF16), k * x.astype(BF16), x[0:1, :] * y[0:1, :]

        def finish(u, h, a, q_dec, k_dec, decay):
            rows = chunk_rows(u)
            hs = slice(h * HGRN_KEY_DIM, (h + 1) * HGRN_KEY_DIM)
            v = i_ref[rows, hs]
            st = state_ref[h]
            o = (jnp.dot(a, v, preferred_element_type=F32)
                 + lax.dot_general(q_dec, st.astype(BF16), nt, preferred_element_type=F32))
            upd = lax.dot_general(v, k_dec, (((0,), (0,)), ((), ())), preferred_element_type=F32)
            state_ref[h] = st * decay + upd
            ms = jnp.mean(o * o, axis=-1, keepdims=True)
            y = o * lax.rsqrt(ms + EPS) * g_on
            o_ref[rows, hs] = (y * z_ref[rows, hs].astype(F32)).astype(BF16)

        items = [(u, h) for u in range(HGRN_CHUNKS_PER_ITER) for h in range(HGRN_HEADS)]
        pending = {}
        for step in range(len(items) + HGRN_SKEW):
            if step < len(items):
                pending[step] = scan(*items[step])
            done = step - HGRN_SKEW
            if done >= 0:
                finish(*items[done], *pending.pop(done))
        return carry

    assert n_chunks % HGRN_CHUNKS_PER_ITER == 0
    lax.fori_loop(0, n_chunks // HGRN_CHUNKS_PER_ITER, chunk_body, 0)


def _merge_kernel(o1_ref, o2_ref, o3_ref, l1_ref, l2_ref, l3_ref, za_ref, ob_ref,
                  sga_ref, sgb_ref, x_ref, mod_ref, wa_ref, wb_ref, wo_ref, fg_ref,
                  ex_ref, out_ref):
    ex = ex_ref[...]
    gate = mod_ref[:, 2 * D_MODEL:3 * D_MODEL]

    def expand(w):
        hi = w.astype(BF16)
        lo = (w - hi.astype(F32)).astype(BF16)
        return jnp.dot(jnp.concatenate([hi, lo], axis=1), ex, preferred_element_type=F32)

    def mix(rows):
        l1, l2, l3 = l1_ref[rows, :], l2_ref[rows, :], l3_ref[rows, :]
        mx = jnp.maximum(jnp.maximum(l1, l2), l3)
        e1, e2, e3 = jnp.exp(l1 - mx), jnp.exp(l2 - mx), jnp.exp(l3 - mx)
        inv = 1.0 / (e1 + e2 + e3)
        oa = (expand(e1 * inv) * o1_ref[rows, :].astype(F32)
              + expand(e2 * inv) * o2_ref[rows, :].astype(F32)
              + expand(e3 * inv) * o3_ref[rows, :].astype(F32))
        return (oa * za_ref[rows, :].astype(F32)).astype(BF16)

    def branches(rows, oa):
        ya = jnp.dot(oa, wa_ref[...], preferred_element_type=F32)
        yb = jnp.dot(ob_ref[rows, :], wb_ref[...], preferred_element_type=F32)
        return (sga_ref[rows, :].astype(F32) * ya + sgb_ref[rows, :].astype(F32) * yb).astype(BF16)

    def project(rows, y):
        z = jnp.dot(y, wo_ref[...], preferred_element_type=F32)
        xo = x_ref[rows, :] + gate * z
        ms = jnp.mean(xo * xo, axis=-1, keepdims=True)
        out_ref[rows, :] = xo * lax.rsqrt(ms + EPS) * fg_ref[...]

    pieces = [slice(r, r + MERGE_PIECE) for r in range(0, x_ref.shape[0], MERGE_PIECE)]
    oas = [mix(rows) for rows in pieces]
    ys = [branches(rows, oa) for rows, oa in zip(pieces, oas)]
    for rows, y in zip(pieces, ys):
        project(rows, y)


_VMEM_MIB = {"inproj": 56, "attn": 48, "hgrn": 32, "merge": 40}


def _cparams(sem, vmem_mib):
    return pltpu.CompilerParams(dimension_semantics=sem,
                                vmem_limit_bytes=vmem_mib * 1024 * 1024)


def kernel(x, c, w_ada, b_ada, norm_g, w_in, hgrn_onorm_g, w_branch_a, w_branch_b, w_out,
           rel_bias, hgrn_lb, final_g):
    B, S, D = x.shape
    assert D == D_MODEL and w_ada.shape[0] == 1, "single-layer kernel"
    N = B * S
    x2 = x.reshape(N, D)

    c8 = jnp.pad(c, ((0, 8 - B), (0, 0)))
    mod = pl.pallas_call(
        _mod_kernel,
        grid=(3 * D // 512,),
        in_specs=[pl.BlockSpec((8, D), lambda j: (0, 0)),
                  pl.BlockSpec((D, 512), lambda j: (0, j)),
                  pl.BlockSpec((1, 512), lambda j: (0, j))],
        out_specs=pl.BlockSpec((8, 512), lambda j: (0, j)),
        out_shape=jax.ShapeDtypeStruct((8, 3 * D), F32),
        name="adaln_mod",
    )(c8, w_ada[0], b_ada[0].reshape(1, 3 * D))
    mod3 = mod.reshape(8, 1, 3 * D)

    lb = pl.pallas_call(
        _lower_bound_kernel,
        out_shape=jax.ShapeDtypeStruct((1, HGRN_WIDTH), F32),
        name="hgrn_lower_bound",
    )(hgrn_lb)

    n_pat = len(DILATED_PATTERNS)
    bias_tab = pl.pallas_call(
        _bias_table_kernel,
        grid=(n_pat,),
        in_specs=[pl.BlockSpec(memory_space=pltpu.SMEM),
                  pl.BlockSpec((None, ATTN_BLOCK, 2 * ATTN_BLOCK), lambda g: (g, 0, 0))],
        out_specs=pl.BlockSpec((None, 2, ATTN_HEADS, ATTN_BLOCK, 2 * ATTN_BLOCK),
                               lambda g: (g, 0, 0, 0, 0)),
        out_shape=jax.ShapeDtypeStruct((n_pat, 2, ATTN_HEADS, ATTN_BLOCK, 2 * ATTN_BLOCK), F32),
        name="rel_bias_table",
    )(rel_bias, jnp.asarray(_bucket_tables()))

    tm = PERM_TILE
    tiles_per_b = S // tm
    in_width = len(_PROJ_KINDS) * PROJ_TN
    assert PROJ_TN == ATTN_WIDTH and w_in.shape[2] == in_width
    resident = dict(pipeline_mode=pl.Buffered(1))
    main, fgate, qkv_p = pl.pallas_call(
        _inproj_kernel,
        grid=(N // tm,),
        in_specs=[pl.BlockSpec((tm, D), lambda i: (i, 0)),
                  pl.BlockSpec((None, 1, 3 * D), lambda i: (i // tiles_per_b, 0, 0)),
                  pl.BlockSpec((1, D), lambda i: (0, 0)),
                  pl.BlockSpec((1, HGRN_WIDTH), lambda i: (0, 0)),
                  pl.BlockSpec((D, in_width), lambda i: (0, 0), **resident)],
        out_specs=[pl.BlockSpec((tm, MAIN_WIDTH), lambda i: (i, 0)),
                   pl.BlockSpec((tm, HGRN_WIDTH), lambda i: (i, 0)),
                   pl.BlockSpec((tm, _QKV_TILES * ATTN_WIDTH), lambda i: (i, 0))],
        out_shape=[jax.ShapeDtypeStruct((N, MAIN_WIDTH), BF16),
                   jax.ShapeDtypeStruct((N, HGRN_WIDTH), F32),
                   jax.ShapeDtypeStruct((N, _QKV_TILES * ATTN_WIDTH), BF16)],
        scratch_shapes=[pltpu.VMEM((tm, D), BF16),
                        pltpu.VMEM((2, PROJ_PIECE_N // V7X_LANES, PROJ_PIECE_M, V7X_LANES), F32)],
        compiler_params=_cparams(("arbitrary",), _VMEM_MIB["inproj"]),
        name="inproj",
    )(x2, mod3, norm_g[0].reshape(1, D), lb, w_in[0].astype(BF16))

    qa_t = _MAIN_COLS["qa"][0] // ATTN_WIDTH
    ka_t = _MAIN_COLS["ka"][0] // ATTN_WIDTH
    va_t = _MAIN_COLS["va"][0] // ATTN_WIDTH
    attn_outs = []
    main_b = main.reshape(B, S, MAIN_WIDTH)
    qkv_pv = qkv_p.reshape(B, S, _QKV_TILES * ATTN_WIDTH)
    n_super = S // ATTN_SUPER
    for g, (window, dil) in enumerate(DILATED_PATTERNS):
        assert window // dil == ATTN_BLOCK
        src, tiles = (main_b, (qa_t, ka_t, va_t)) if dil == 1 else (qkv_pv, (0, 1, 2))
        prev_rows = {1: ATTN_BLOCK, PERM_DIL: ATTN_SUPER}.get(dil, PERM_TILE)
        per_step = ATTN_SUPER // prev_rows

        def cur_spec(t):
            return pl.BlockSpec((None, ATTN_SUPER, ATTN_WIDTH), lambda b, n, t=t: (b, n, t))

        def prev_spec(t, prev_rows=prev_rows, per_step=per_step):
            return pl.BlockSpec((None, prev_rows, ATTN_WIDTH),
                                lambda b, n, t=t: (b, jnp.maximum(n * per_step - 1, 0), t))

        scratch = []
        if dil > 1:
            scratch += [pltpu.VMEM((ATTN_WIDTH // V7X_LANES, ATTN_SUPER, V7X_LANES), F32)]
        if dil != PERM_DIL:
            scratch += [pltpu.VMEM((prev_rows + ATTN_SUPER, ATTN_WIDTH), BF16)] * 2
        o_g, lse_g = pl.pallas_call(
            functools.partial(_attn_kernel, dil=dil),
            grid=(B, n_super),
            in_specs=[cur_spec(tiles[0]), prev_spec(tiles[1]), cur_spec(tiles[1]),
                      prev_spec(tiles[2]), cur_spec(tiles[2]),
                      pl.BlockSpec((None, 2, ATTN_HEADS, ATTN_BLOCK, 2 * ATTN_BLOCK),
                                   lambda b, n, g=g: (g, 0, 0, 0, 0))],
            out_specs=[pl.BlockSpec((None, ATTN_SUPER, ATTN_WIDTH), lambda b, n: (b, n, 0)),
                       pl.BlockSpec((None, ATTN_SUPER, V7X_LANES), lambda b, n: (b, n, 0))],
            out_shape=[jax.ShapeDtypeStruct((B, S, ATTN_WIDTH), BF16),
                       jax.ShapeDtypeStruct((B, S, V7X_LANES), F32)],
            scratch_shapes=scratch,
            compiler_params=_cparams(("arbitrary", "arbitrary"), _VMEM_MIB["attn"]),
            name=f"dilated_attn_d{dil}",
        )(src, src, src, src, src, bias_tab)
        attn_outs.append((o_g.reshape(N, ATTN_WIDTH), lse_g.reshape(N, V7X_LANES)))

    th = HGRN_TILE
    chunk = HGRN_CHUNK
    hw_t = HGRN_WIDTH
    qb_t = _MAIN_COLS["qb"][0] // hw_t
    ib_t = _MAIN_COLS["ib"][0] // hw_t
    zb_t = _MAIN_COLS["zb"][0] // hw_t
    ob = pl.pallas_call(
        functools.partial(_hgrn_kernel, chunk=chunk, n_chunks=th // chunk),
        grid=(B, S // th),
        in_specs=[pl.BlockSpec((None, th, hw_t), lambda b, s: (b, s, qb_t)),
                  pl.BlockSpec((None, th, hw_t), lambda b, s: (b, s, 0)),
                  pl.BlockSpec((None, th, hw_t), lambda b, s: (b, s, ib_t)),
                  pl.BlockSpec((None, th, hw_t), lambda b, s: (b, s, zb_t)),
                  pl.BlockSpec((1, HGRN_VAL_DIM), lambda b, s: (0, 0))],
        out_specs=pl.BlockSpec((None, th, hw_t), lambda b, s: (b, s, 0)),
        out_shape=jax.ShapeDtypeStruct((B, S, hw_t), BF16),
        scratch_shapes=[pltpu.VMEM((HGRN_HEADS, HGRN_VAL_DIM, HGRN_KEY_DIM), F32)],
        compiler_params=_cparams(("arbitrary", "arbitrary"), _VMEM_MIB["hgrn"]),
        name="hgrn2",
    )(main_b, fgate.reshape(B, S, hw_t), main_b, main_b, hgrn_onorm_g[0].reshape(1, HGRN_VAL_DIM))
    ob = ob.reshape(N, hw_t)

    tk = MERGE_TILE
    tiles_per_b5 = S // tk
    za_t = _MAIN_COLS["za"][0] // ATTN_WIDTH
    ga_t = _MAIN_COLS["ga"][0] // D
    gb_t = _MAIN_COLS["gb"][0] // D
    expand_mat = np.zeros((2 * V7X_LANES, ATTN_WIDTH), np.float32)
    for h in range(ATTN_HEADS):
        expand_mat[h, h * ATTN_HEAD_DIM:(h + 1) * ATTN_HEAD_DIM] = 1.0
        expand_mat[V7X_LANES + h, h * ATTN_HEAD_DIM:(h + 1) * ATTN_HEAD_DIM] = 1.0
    (o1, l1), (o2, l2), (o3, l3) = attn_outs
    row_spec = lambda w, t=0: pl.BlockSpec((tk, w), lambda i, t=t: (i, t))
    full_spec = lambda a, b: pl.BlockSpec((a, b), lambda i: (0, 0))
    out = pl.pallas_call(
        _merge_kernel,
        grid=(N // tk,),
        in_specs=[row_spec(ATTN_WIDTH), row_spec(ATTN_WIDTH), row_spec(ATTN_WIDTH),
                  row_spec(V7X_LANES), row_spec(V7X_LANES), row_spec(V7X_LANES),
                  row_spec(ATTN_WIDTH, za_t), row_spec(HGRN_WIDTH),
                  row_spec(D, ga_t), row_spec(D, gb_t), row_spec(D),
                  pl.BlockSpec((None, 1, 3 * D), lambda i: (i // tiles_per_b5, 0, 0)),
                  full_spec(ATTN_WIDTH, D), full_spec(HGRN_WIDTH, D), full_spec(D, D),
                  full_spec(1, D), full_spec(2 * V7X_LANES, ATTN_WIDTH)],
        out_specs=pl.BlockSpec((tk, D), lambda i: (i, 0)),
        out_shape=jax.ShapeDtypeStruct((N, D), F32),
        compiler_params=_cparams(("arbitrary",), _VMEM_MIB["merge"]),
        name="gated_merge",
    )(o1, o2, o3, l1, l2, l3, main, ob, main, main, x2, mod3,
      w_branch_a[0].astype(BF16), w_branch_b[0].astype(BF16), w_out[0].astype(BF16),
      final_g.reshape(1, D), jnp.asarray(expand_mat, BF16))
    return out.reshape(B, S, D)
```

```python
import functools
import math
import types

import numpy as np
import jax
import jax.numpy as jnp
from jax import lax
from jax.experimental import pallas as pl
from jax.experimental.pallas import tpu as pltpu

D_MODEL = 1024
ATTN_HEADS = 8
ATTN_HEAD_DIM = 64
ATTN_WIDTH = ATTN_HEADS * ATTN_HEAD_DIM
DILATED_PATTERNS = ((128, 1), (512, 4), (2048, 16))
ATTN_BLOCK = 128
N_BUCKETS = 32
MAX_DISTANCE = 2048
NEG_INF = -1e30
HGRN_HEADS = 8
HGRN_KEY_DIM = 128
HGRN_VAL_DIM = 128
HGRN_WIDTH = HGRN_HEADS * HGRN_VAL_DIM
EPS = 1e-6

V7X_LANES = 128
V7X_SUBLANES = 8
BF16_ROWS_PER_VREG = 16

F32 = jnp.float32
BF16 = jnp.bfloat16

_MAIN_COLS = {}
_off = 0
for _name, _w in (("qa", ATTN_WIDTH), ("ka", ATTN_WIDTH), ("va", ATTN_WIDTH), ("za", ATTN_WIDTH),
                  ("qb", HGRN_WIDTH), ("ib", HGRN_WIDTH), ("zb", HGRN_WIDTH),
                  ("ga", D_MODEL), ("gb", D_MODEL)):
    _MAIN_COLS[_name] = (_off, _w)
    _off += _w
MAIN_WIDTH = _off
PROJ_TN = 512
PROJ_PIECE_M = 256
PROJ_PIECE_N = 512
PROJ_SKEW = 2
_PROJ_KINDS = (["qscale"] + ["kv"] * 2 + ["silu"] + ["silu"] * 2 + ["forget"] * 2
               + ["id"] * 2 + ["silu"] * 2 + ["sigmoid"] * 4)
_F_TILE0 = _PROJ_KINDS.index("forget")
_F_TILES = _PROJ_KINDS.count("forget")
_QKV_TILES = 3

PERM_DIL = 16
PERM_TILE = 512
PERM_SLAB = PERM_TILE // PERM_DIL
ATTN_SUPER = PERM_DIL * ATTN_BLOCK
ATTN_BLOCKS_PER_ITER = 2
ATTN_SKEW = 2
MERGE_PIECE = 256
MERGE_TILE = 2 * MERGE_PIECE
HGRN_TILE = 512
HGRN_CHUNK = 64
HGRN_CHUNKS_PER_ITER = 8
HGRN_SKEW = 1
LOG2E = math.log2(math.e)
LN2 = math.log(2.0)


def _sigmoid(x):
    return 1.0 / (1.0 + jnp.exp(-x))


def _mod_kernel(c_ref, w_ref, b_ref, o_ref):
    c = c_ref[...]
    sc = c * _sigmoid(c)
    o_ref[...] = jnp.dot(sc, w_ref[...], precision=lax.Precision.HIGHEST,
                         preferred_element_type=F32) + b_ref[...]


def _lower_bound_kernel(hl_ref, o_ref):
    hl = hl_ref[...]
    m = jnp.max(hl, axis=0, keepdims=True)
    e = jnp.exp(hl - m)
    o_ref[...] = e[0:1, :] / jnp.sum(e, axis=0, keepdims=True)


def _bias_table_kernel(rb_ref, bucket_ref, o_ref):
    bk = bucket_ref[...]
    no_prev = lax.broadcasted_iota(jnp.int32, bk.shape, 1) < ATTN_BLOCK
    for h in range(ATTN_HEADS):
        acc = jnp.full(bk.shape, NEG_INF, F32)
        for u in range(N_BUCKETS):
            acc = jnp.where(bk == u, rb_ref[u, h] * LOG2E, acc)
        o_ref[0, h] = acc
        o_ref[1, h] = jnp.where(no_prev, NEG_INF, acc)


def _bucket_tables():
    qi = np.arange(ATTN_BLOCK)[:, None]
    kj = np.arange(2 * ATTN_BLOCK)[None, :]
    delta = qi + ATTN_BLOCK - kj
    max_exact = N_BUCKETS // 2
    tabs = []
    for window, dilation in DILATED_PATTERNS:
        span = window // dilation
        band = (delta >= 0) & (delta <= span)
        dist = np.clip(delta, 0, None) * dilation
        n = dist.astype(np.float32)
        large = max_exact + (np.log(np.maximum(n, 1.0) / max_exact)
                             / math.log(MAX_DISTANCE / max_exact)
                             * (N_BUCKETS - max_exact)).astype(np.int32)
        large = np.minimum(large, N_BUCKETS - 1)
        bucket = np.where(dist < max_exact, dist, large)
        tab = np.where(band, bucket, -1).astype(np.int32)
        if dilation > 1:
            order = _gather_order(dilation)
            cols = np.concatenate([order, ATTN_BLOCK + order])
            tab = tab[order][:, cols]
        tabs.append(tab)
    return np.stack(tabs, 0)


def _gather_order(dilation):
    per_tile = PERM_DIL // dilation
    slab = np.arange(ATTN_BLOCK) // PERM_SLAB
    m = np.arange(ATTN_BLOCK) % PERM_SLAB
    if per_tile == 1:
        return slab * PERM_SLAB + m
    assert per_tile * PERM_SLAB == ATTN_BLOCK
    return per_tile * m + slab


def _inproj_kernel(x_ref, mod_ref, g_ref, lb_ref, w_ref, om_ref, of_ref, op_ref, h_ref, accl_ref):
    tm = x_ref.shape[0]
    x = x_ref[...]
    ms = jnp.mean(x * x, axis=-1, keepdims=True)
    y = x * lax.rsqrt(ms + EPS) * g_ref[...]
    shift = mod_ref[:, 0:D_MODEL]
    scale = mod_ref[:, D_MODEL:2 * D_MODEL]
    h_ref[...] = (y * (1.0 + scale) + shift).astype(BF16)

    n_sub = PROJ_TN // PROJ_PIECE_N
    slab = PROJ_PIECE_M // PERM_DIL

    def write_perm(mc, col0, acc, slot):
        for c in range(PROJ_PIECE_N // V7X_LANES):
            accl_ref[slot, c] = acc[:, c * V7X_LANES:(c + 1) * V7X_LANES]
            for r in range(PERM_DIL):
                rows = accl_ref[slot, c, pl.ds(r, slab, stride=PERM_DIL), :]
                row0 = r * PERM_SLAB + mc * slab
                op_ref[row0:row0 + slab, col0 + c * V7X_LANES:col0 + (c + 1) * V7X_LANES] = rows.astype(BF16)

    def epilogue(j, sub, mc, acc):
        kind = _PROJ_KINDS[j]
        rows = slice(mc * PROJ_PIECE_M, (mc + 1) * PROJ_PIECE_M)
        if kind == "forget":
            col0 = (j - _F_TILE0) * PROJ_TN + sub * PROJ_PIECE_N
            lb = lb_ref[:, col0:col0 + PROJ_PIECE_N]
            of_ref[rows, col0:col0 + PROJ_PIECE_N] = lb + (1.0 - lb) * _sigmoid(acc)
            return
        if kind == "qscale":
            acc = acc * (ATTN_HEAD_DIM ** -0.5 * LOG2E)
        elif kind == "silu":
            acc = acc * _sigmoid(acc)
        elif kind == "sigmoid":
            acc = _sigmoid(acc)
        jm = j if j < _F_TILE0 else j - _F_TILES
        col0 = jm * PROJ_TN + sub * PROJ_PIECE_N
        om_ref[rows, col0:col0 + PROJ_PIECE_N] = acc.astype(BF16)
        if j < _QKV_TILES:
            write_perm(mc, col0, acc, (sub * (tm // PROJ_PIECE_M) + mc) % accl_ref.shape[0])

    items = [(j, sub, mc) for j in range(len(_PROJ_KINDS)) for sub in range(n_sub)
             for mc in range(tm // PROJ_PIECE_M)]
    pending = {}
    for step in range(len(items) + PROJ_SKEW):
        if step < len(items):
            j, sub, mc = items[step]
            col0 = j * PROJ_TN + sub * PROJ_PIECE_N
            pending[step] = jnp.dot(h_ref[mc * PROJ_PIECE_M:(mc + 1) * PROJ_PIECE_M, :],
                                    w_ref[:, col0:col0 + PROJ_PIECE_N], preferred_element_type=F32)
        done = step - PROJ_SKEW
        if done >= 0:
            epilogue(*items[done], pending.pop(done))


def _pair_lanes(hp):
    return slice(hp * V7X_LANES, (hp + 1) * V7X_LANES)


def _attn_scores(q, k, bias_ref, hp, first, lane):
    blk = q.shape[0]
    low = lane < ATTN_HEAD_DIM
    zero = jnp.zeros_like(q)
    q2 = jnp.concatenate([jnp.where(low, q, zero), jnp.where(low, zero, q)], axis=0)
    s = lax.dot_general(q2, k, (((1,), (1,)), ((), ())), preferred_element_type=F32)
    s = s + bias_ref[first, pl.ds(2 * hp, 2)].reshape(2 * blk, 2 * blk)
    return s, jnp.max(s, axis=-1, keepdims=True)


def _attn_values(s, m, v, hp, lane, lse_all):
    blk = s.shape[0] // 2
    p = jnp.exp2(s - m).astype(BF16)
    h0, h1 = 2 * hp, 2 * hp + 1
    low = lane < ATTN_HEAD_DIM
    low_v = lax.broadcasted_iota(jnp.int32, v.shape, 1) < ATTN_HEAD_DIM
    one = jnp.ones_like(v)
    o0 = jnp.dot(p[:blk], jnp.where(low_v, v, one), preferred_element_type=F32)
    o1 = jnp.dot(p[blk:], jnp.where(low_v, one, v), preferred_element_type=F32)
    num = jnp.where(low, o0, o1)
    den_swapped = jnp.where(low, o1, o0)
    den = pltpu.roll(den_swapped, ATTN_HEAD_DIM, 1)
    is_h1 = lane == h1
    lse = jnp.where(is_h1, m[blk:], m[:blk]) * LN2 + jnp.log(jnp.where(is_h1, den_swapped, den))
    lse_all = jnp.where(jnp.logical_or(lane == h0, is_h1), lse, lse_all)
    return num / den, lse_all


def _attn_pipeline(blocks, bias_ref, lane):
    n_pairs = ATTN_HEADS // 2
    items = [(bi, hp) for bi in range(len(blocks)) for hp in range(n_pairs)]
    lse = [jnp.zeros(lane.shape, F32) for _ in blocks]
    pending = {}
    for step in range(len(items) + ATTN_SKEW):
        if step < len(items):
            bi, hp = items[step]
            b = blocks[bi]
            pending[step] = _attn_scores(b.get_q(hp), b.get_k(hp), bias_ref, hp, b.first, lane)
        done = step - ATTN_SKEW
        if done >= 0:
            bi, hp = items[done]
            b = blocks[bi]
            o_pair, lse[bi] = _attn_values(*pending.pop(done), b.get_v(hp), hp, lane, lse[bi])
            b.put_o(hp, o_pair)
            if hp == n_pairs - 1:
                b.put_lse(lse[bi])


def _attn_kernel(q_ref, kp_ref, kc_ref, vp_ref, vc_ref, bias_ref, o_ref, lse_ref, *scratch, dil):
    n = pl.program_id(1)
    blk_rows = ATTN_BLOCK
    lane = lax.broadcasted_iota(jnp.int32, (blk_rows, V7X_LANES), 1)
    prev_rows = kp_ref.shape[0]
    scratch = list(scratch)
    o_scr = scratch.pop(0) if dil > 1 else None
    if scratch:
        kext, vext = scratch
        kext[0:prev_rows] = kp_ref[...]
        kext[prev_rows:] = kc_ref[...]
        vext[0:prev_rows] = vp_ref[...]
        vext[prev_rows:] = vc_ref[...]
        k_prev, k_cur, v_prev, v_cur, cur_off = kext, kext, vext, vext, prev_rows
    else:
        k_prev, k_cur, v_prev, v_cur, cur_off = kp_ref, kc_ref, vp_ref, vc_ref, 0

    def make_block(blk):
        if dil == 1:
            slab = blk_rows
            offs = [blk * blk_rows]
            first = jnp.logical_and(n == 0, blk == 0)
        elif dil == PERM_DIL:
            slab = PERM_SLAB
            offs = [t * PERM_TILE + blk * PERM_SLAB for t in range(blk_rows // slab)]
            tok0 = [t * PERM_TILE + blk for t in range(blk_rows // slab)]
            first = n == 0
        else:
            slab = PERM_SLAB
            tile = lax.shift_right_logical(blk, 2)
            res = jnp.bitwise_and(blk, dil - 1)
            offs = [tile * PERM_TILE + (res + dil * j) * PERM_SLAB for j in range(blk_rows // slab)]
            tok0 = [tile * PERM_TILE + res + dil * j for j in range(blk_rows // slab)]
            first = jnp.logical_and(n == 0, tile == 0)
        offs = [pl.multiple_of(o, slab) for o in offs]

        def slabs(ref, hp, shift=0, rows=slab):
            return [ref[pl.ds(o + shift, rows), _pair_lanes(hp)] for o in offs]

        def put_o(hp, o_pair):
            if dil == 1:
                o_ref[pl.ds(offs[0], slab), _pair_lanes(hp)] = o_pair.astype(BF16)
                return
            for j, t0 in enumerate(tok0):
                o_scr[hp, pl.ds(t0, slab, stride=PERM_DIL), :] = o_pair[j * slab:(j + 1) * slab]

        def put_lse(lse_all):
            if dil == 1:
                lse_ref[pl.ds(offs[0], slab), :] = lse_all
                return
            for j, t0 in enumerate(tok0):
                lse_ref[pl.ds(t0, slab, stride=PERM_DIL), :] = lse_all[j * slab:(j + 1) * slab]

        if dil == 1:
            get_k = lambda hp: slabs(k_cur, hp, rows=2 * slab)[0]
            get_v = lambda hp: slabs(v_cur, hp, rows=2 * slab)[0]
        else:
            get_k = lambda hp: jnp.concatenate(slabs(k_prev, hp) + slabs(k_cur, hp, cur_off), axis=0)
            get_v = lambda hp: jnp.concatenate(slabs(v_prev, hp) + slabs(v_cur, hp, cur_off), axis=0)
        return types.SimpleNamespace(
            get_q=lambda hp: jnp.concatenate(slabs(q_ref, hp), axis=0), get_k=get_k, get_v=get_v,
            put_o=put_o, put_lse=put_lse, first=first.astype(jnp.int32))

    def body(it, carry):
        _attn_pipeline([make_block(it * ATTN_BLOCKS_PER_ITER + u) for u in range(ATTN_BLOCKS_PER_ITER)],
                       bias_ref, lane)
        return carry

    lax.fori_loop(0, ATTN_SUPER // ATTN_BLOCK // ATTN_BLOCKS_PER_ITER, body, 0)
    if dil > 1:
        for hp in range(ATTN_HEADS // 2):
            o_ref[:, _pair_lanes(hp)] = o_scr[hp].astype(BF16)


def _roll_rows(a, shift):
    rows = a.shape[0]
    if abs(shift) >= V7X_SUBLANES:
        return pltpu.roll(a, shift % rows, 0)
    grouped = a.reshape(rows // V7X_SUBLANES, V7X_SUBLANES, a.shape[1])
    return pltpu.roll(grouped, shift % V7X_SUBLANES, 1).reshape(a.shape)


def _hgrn_kernel(q_ref, f_ref, i_ref, z_ref, g_ref, o_ref, state_ref, *, chunk, n_chunks):
    @pl.when(pl.program_id(1) == 0)
    def _():
        state_ref[...] = jnp.zeros_like(state_ref)

    C = chunk
    n_levels = C.bit_length() - 1
    g_on = g_ref[...]
    nt = (((1,), (1,)), ((), ()))

    def chunk_body(ci, carry):
        row = lax.broadcasted_iota(jnp.int32, (C, HGRN_KEY_DIM), 0)
        odds = [jnp.bitwise_and(row, 1 << lvl) != 0 for lvl in range(n_levels)]
        tt = lax.broadcasted_iota(jnp.int32, (C, C), 0)
        ss = lax.broadcasted_iota(jnp.int32, (C, C), 1)
        owner = jnp.where(tt > ss, 32 - lax.clz(jnp.bitwise_xor(tt, ss)),
                          jnp.where(tt == ss, 0, -1))
        owned = [owner == lvl for lvl in range(n_levels + 1)]

        def chunk_rows(u):
            return pl.ds(pl.multiple_of((ci * HGRN_CHUNKS_PER_ITER + u) * C, C), C)

        def scan(u, h):
            rows = chunk_rows(u)
            hs = slice(h * HGRN_KEY_DIM, (h + 1) * HGRN_KEY_DIM)
            f = f_ref[rows, hs]
            q = q_ref[rows, hs]
            k = (1.0 - f).astype(BF16)
            a = jnp.where(owned[0], lax.dot_general(q, k, nt, preferred_element_type=F32), 0.0)
            x = jnp.where(odds[0], f, 1.0)
            y = jnp.where(odds[0], 1.0, f)
            for lvl in range(n_levels):
                m = 1 << lvl
                e = x.astype(BF16)
                if m % BF16_ROWS_PER_VREG:
                    pm = lax.dot_general(q * e, k * e, nt, preferred_element_type=F32)
                    a = jnp.where(owned[lvl + 1], pm, a)
                else:
                    odd_rows = [slice(b * m, (b + 1) * m) for b in range(1, C // m, 2)]
                    qe = jnp.concatenate([q[r] * e[r] for r in odd_rows], axis=0)
                    pm = lax.dot_general(qe, k * e, nt, preferred_element_type=F32)
                    slabs = [a[b * m:(b + 1) * m] for b in range(C // m)]
                    for i, r in enumerate(odd_rows):
                        slabs[2 * i + 1] = jnp.where(owned[lvl + 1][r], pm[i * m:(i + 1) * m], a[r])
                    a = jnp.concatenate(slabs, axis=0)
                tot = x * y if lvl else f
                if 2 * m == V7X_SUBLANES:
                    partner = _roll_rows(tot, m)
                else:
                    partner = jnp.where(odds[lvl], _roll_rows(tot, m), _roll_rows(tot, -m))
                z = x * partner
                keep = odds[lvl] == odds[lvl + 1] if lvl + 1 < n_levels else jnp.logical_not(odds[lvl])
                x, y = jnp.where(keep, z, y), jnp.where(keep, y, z)
            return a.astype(BF16), q * y.astype(BF16), k * x.astype(BF16), x[0:1, :] * y[0:1, :]

        def finish(u, h, a, q_dec, k_dec, decay):
            rows = chunk_rows(u)
            hs = slice(h * HGRN_KEY_DIM, (h + 1) * HGRN_KEY_DIM)
            v = i_ref[rows, hs]
            st = state_ref[h]
            o = (jnp.dot(a, v, preferred_element_type=F32)
                 + lax.dot_general(q_dec, st.astype(BF16), nt, preferred_element_type=F32))
            upd = lax.dot_general(v, k_dec, (((0,), (0,)), ((), ())), preferred_element_type=F32)
            state_ref[h] = st * decay + upd
            ms = jnp.mean(o * o, axis=-1, keepdims=True)
            y = o * lax.rsqrt(ms + EPS) * g_on
            o_ref[rows, hs] = (y * z_ref[rows, hs].astype(F32)).astype(BF16)

        items = [(u, h) for u in range(HGRN_CHUNKS_PER_ITER) for h in range(HGRN_HEADS)]
        pending = {}
        for step in range(len(items) + HGRN_SKEW):
            if step < len(items):
                pending[step] = scan(*items[step])
            done = step - HGRN_SKEW
            if done >= 0:
                finish(*items[done], *pending.pop(done))
        return carry

    assert n_chunks % HGRN_CHUNKS_PER_ITER == 0
    lax.fori_loop(0, n_chunks // HGRN_CHUNKS_PER_ITER, chunk_body, 0)


def _merge_kernel(o1_ref, o2_ref, o3_ref, l1_ref, l2_ref, l3_ref, za_ref, ob_ref,
                  sga_ref, sgb_ref, x_ref, mod_ref, wa_ref, wb_ref, wo_ref, fg_ref,
                  ex_ref, out_ref):
    ex = ex_ref[...]
    gate = mod_ref[:, 2 * D_MODEL:3 * D_MODEL]

    def expand(w):
        hi = w.astype(BF16)
        lo = (w - hi.astype(F32)).astype(BF16)
        return jnp.dot(jnp.concatenate([hi, lo], axis=1), ex, preferred_element_type=F32)

    def mix(rows):
        l1, l2, l3 = l1_ref[rows, :], l2_ref[rows, :], l3_ref[rows, :]
        mx = jnp.maximum(jnp.maximum(l1, l2), l3)
        e1, e2, e3 = jnp.exp(l1 - mx), jnp.exp(l2 - mx), jnp.exp(l3 - mx)
        inv = 1.0 / (e1 + e2 + e3)
        oa = (expand(e1 * inv) * o1_ref[rows, :].astype(F32)
              + expand(e2 * inv) * o2_ref[rows, :].astype(F32)
              + expand(e3 * inv) * o3_ref[rows, :].astype(F32))
        return (oa * za_ref[rows, :].astype(F32)).astype(BF16)

    def branches(rows, oa):
        ya = jnp.dot(oa, wa_ref[...], preferred_element_type=F32)
        yb = jnp.dot(ob_ref[rows, :], wb_ref[...], preferred_element_type=F32)
        return (sga_ref[rows, :].astype(F32) * ya + sgb_ref[rows, :].astype(F32) * yb).astype(BF16)

    def project(rows, y):
        z = jnp.dot(y, wo_ref[...], preferred_element_type=F32)
        xo = x_ref[rows, :] + gate * z
        ms = jnp.mean(xo * xo, axis=-1, keepdims=True)
        out_ref[rows, :] = xo * lax.rsqrt(ms + EPS) * fg_ref[...]

    pieces = [slice(r, r + MERGE_PIECE) for r in range(0, x_ref.shape[0], MERGE_PIECE)]
    oas = [mix(rows) for rows in pieces]
    ys = [branches(rows, oa) for rows, oa in zip(pieces, oas)]
    for rows, y in zip(pieces, ys):
        project(rows, y)


_VMEM_MIB = {"inproj": 56, "attn": 48, "hgrn": 32, "merge": 40}


def _cparams(sem, vmem_mib):
    return pltpu.CompilerParams(dimension_semantics=sem,
                                vmem_limit_bytes=vmem_mib * 1024 * 1024)


def kernel(x, c, w_ada, b_ada, norm_g, w_in, hgrn_onorm_g, w_branch_a, w_branch_b, w_out,
           rel_bias, hgrn_lb, final_g):
    B, S, D = x.shape
    assert D == D_MODEL and w_ada.shape[0] == 1, "single-layer kernel"
    N = B * S
    x2 = x.reshape(N, D)

    c8 = jnp.pad(c, ((0, 8 - B), (0, 0)))
    mod = pl.pallas_call(
        _mod_kernel,
        grid=(3 * D // 512,),
        in_specs=[pl.BlockSpec((8, D), lambda j: (0, 0)),
                  pl.BlockSpec((D, 512), lambda j: (0, j)),
                  pl.BlockSpec((1, 512), lambda j: (0, j))],
        out_specs=pl.BlockSpec((8, 512), lambda j: (0, j)),
        out_shape=jax.ShapeDtypeStruct((8, 3 * D), F32),
        name="adaln_mod",
    )(c8, w_ada[0], b_ada[0].reshape(1, 3 * D))
    mod3 = mod.reshape(8, 1, 3 * D)

    lb = pl.pallas_call(
        _lower_bound_kernel,
        out_shape=jax.ShapeDtypeStruct((1, HGRN_WIDTH), F32),
        name="hgrn_lower_bound",
    )(hgrn_lb)

    n_pat = len(DILATED_PATTERNS)
    bias_tab = pl.pallas_call(
        _bias_table_kernel,
        grid=(n_pat,),
        in_specs=[pl.BlockSpec(memory_space=pltpu.SMEM),
                  pl.BlockSpec((None, ATTN_BLOCK, 2 * ATTN_BLOCK), lambda g: (g, 0, 0))],
        out_specs=pl.BlockSpec((None, 2, ATTN_HEADS, ATTN_BLOCK, 2 * ATTN_BLOCK),
                               lambda g: (g, 0, 0, 0, 0)),
        out_shape=jax.ShapeDtypeStruct((n_pat, 2, ATTN_HEADS, ATTN_BLOCK, 2 * ATTN_BLOCK), F32),
        name="rel_bias_table",
    )(rel_bias, jnp.asarray(_bucket_tables()))

    tm = PERM_TILE
    tiles_per_b = S // tm
    in_width = len(_PROJ_KINDS) * PROJ_TN
    assert PROJ_TN == ATTN_WIDTH and w_in.shape[2] == in_width
    resident = dict(pipeline_mode=pl.Buffered(1))
    main, fgate, qkv_p = pl.pallas_call(
        _inproj_kernel,
        grid=(N // tm,),
        in_specs=[pl.BlockSpec((tm, D), lambda i: (i, 0)),
                  pl.BlockSpec((None, 1, 3 * D), lambda i: (i // tiles_per_b, 0, 0)),
                  pl.BlockSpec((1, D), lambda i: (0, 0)),
                  pl.BlockSpec((1, HGRN_WIDTH), lambda i: (0, 0)),
                  pl.BlockSpec((D, in_width), lambda i: (0, 0), **resident)],
        out_specs=[pl.BlockSpec((tm, MAIN_WIDTH), lambda i: (i, 0)),
                   pl.BlockSpec((tm, HGRN_WIDTH), lambda i: (i, 0)),
                   pl.BlockSpec((tm, _QKV_TILES * ATTN_WIDTH), lambda i: (i, 0))],
        out_shape=[jax.ShapeDtypeStruct((N, MAIN_WIDTH), BF16),
                   jax.ShapeDtypeStruct((N, HGRN_WIDTH), F32),
                   jax.ShapeDtypeStruct((N, _QKV_TILES * ATTN_WIDTH), BF16)],
        scratch_shapes=[pltpu.VMEM((tm, D), BF16),
                        pltpu.VMEM((2, PROJ_PIECE_N // V7X_LANES, PROJ_PIECE_M, V7X_LANES), F32)],
        compiler_params=_cparams(("arbitrary",), _VMEM_MIB["inproj"]),
        name="inproj",
    )(x2, mod3, norm_g[0].reshape(1, D), lb, w_in[0].astype(BF16))

    qa_t = _MAIN_COLS["qa"][0] // ATTN_WIDTH
    ka_t = _MAIN_COLS["ka"][0] // ATTN_WIDTH
    va_t = _MAIN_COLS["va"][0] // ATTN_WIDTH
    attn_outs = []
    main_b = main.reshape(B, S, MAIN_WIDTH)
    qkv_pv = qkv_p.reshape(B, S, _QKV_TILES * ATTN_WIDTH)
    n_super = S // ATTN_SUPER
    for g, (window, dil) in enumerate(DILATED_PATTERNS):
        assert window // dil == ATTN_BLOCK
        src, tiles = (main_b, (qa_t, ka_t, va_t)) if dil == 1 else (qkv_pv, (0, 1, 2))
        prev_rows = {1: ATTN_BLOCK, PERM_DIL: ATTN_SUPER}.get(dil, PERM_TILE)
        per_step = ATTN_SUPER // prev_rows

        def cur_spec(t):
            return pl.BlockSpec((None, ATTN_SUPER, ATTN_WIDTH), lambda b, n, t=t: (b, n, t))

        def prev_spec(t, prev_rows=prev_rows, per_step=per_step):
            return pl.BlockSpec((None, prev_rows, ATTN_WIDTH),
                                lambda b, n, t=t: (b, jnp.maximum(n * per_step - 1, 0), t))

        scratch = []
        if dil > 1:
            scratch += [pltpu.VMEM((ATTN_WIDTH // V7X_LANES, ATTN_SUPER, V7X_LANES), F32)]
        if dil != PERM_DIL:
            scratch += [pltpu.VMEM((prev_rows + ATTN_SUPER, ATTN_WIDTH), BF16)] * 2
        o_g, lse_g = pl.pallas_call(
            functools.partial(_attn_kernel, dil=dil),
            grid=(B, n_super),
            in_specs=[cur_spec(tiles[0]), prev_spec(tiles[1]), cur_spec(tiles[1]),
                      prev_spec(tiles[2]), cur_spec(tiles[2]),
                      pl.BlockSpec((None, 2, ATTN_HEADS, ATTN_BLOCK, 2 * ATTN_BLOCK),
                                   lambda b, n, g=g: (g, 0, 0, 0, 0))],
            out_specs=[pl.BlockSpec((None, ATTN_SUPER, ATTN_WIDTH), lambda b, n: (b, n, 0)),
                       pl.BlockSpec((None, ATTN_SUPER, V7X_LANES), lambda b, n: (b, n, 0))],
            out_shape=[jax.ShapeDtypeStruct((B, S, ATTN_WIDTH), BF16),
                       jax.ShapeDtypeStruct((B, S, V7X_LANES), F32)],
            scratch_shapes=scratch,
            compiler_params=_cparams(("arbitrary", "arbitrary"), _VMEM_MIB["attn"]),
            name=f"dilated_attn_d{dil}",
        )(src, src, src, src, src, bias_tab)
        attn_outs.append((o_g.reshape(N, ATTN_WIDTH), lse_g.reshape(N, V7X_LANES)))

    th = HGRN_TILE
    chunk = HGRN_CHUNK
    hw_t = HGRN_WIDTH
    qb_t = _MAIN_COLS["qb"][0] // hw_t
    ib_t = _MAIN_COLS["ib"][0] // hw_t
    zb_t = _MAIN_COLS["zb"][0] // hw_t
    ob = pl.pallas_call(
        functools.partial(_hgrn_kernel, chunk=chunk, n_chunks=th // chunk),
        grid=(B, S // th),
        in_specs=[pl.BlockSpec((None, th, hw_t), lambda b, s: (b, s, qb_t)),
                  pl.BlockSpec((None, th, hw_t), lambda b, s: (b, s, 0)),
                  pl.BlockSpec((None, th, hw_t), lambda b, s: (b, s, ib_t)),
                  pl.BlockSpec((None, th, hw_t), lambda b, s: (b, s, zb_t)),
                  pl.BlockSpec((1, HGRN_VAL_DIM), lambda b, s: (0, 0))],
        out_specs=pl.BlockSpec((None, th, hw_t), lambda b, s: (b, s, 0)),
        out_shape=jax.ShapeDtypeStruct((B, S, hw_t), BF16),
        scratch_shapes=[pltpu.VMEM((HGRN_HEADS, HGRN_VAL_DIM, HGRN_KEY_DIM), F32)],
        compiler_params=_cparams(("arbitrary", "arbitrary"), _VMEM_MIB["hgrn"]),
        name="hgrn2",
    )(main_b, fgate.reshape(B, S, hw_t), main_b, main_b, hgrn_onorm_g[0].reshape(1, HGRN_VAL_DIM))
    ob = ob.reshape(N, hw_t)

    tk = MERGE_TILE
    tiles_per_b5 = S // tk
    za_t = _MAIN_COLS["za"][0] // ATTN_WIDTH
    ga_t = _MAIN_COLS["ga"][0] // D
    gb_t = _MAIN_COLS["gb"][0] // D
    expand_mat = np.zeros((2 * V7X_LANES, ATTN_WIDTH), np.float32)
    for h in range(ATTN_HEADS):
        expand_mat[h, h * ATTN_HEAD_DIM:(h + 1) * ATTN_HEAD_DIM] = 1.0
        expand_mat[V7X_LANES + h, h * ATTN_HEAD_DIM:(h + 1) * ATTN_HEAD_DIM] = 1.0
    (o1, l1), (o2, l2), (o3, l3) = attn_outs
    row_spec = lambda w, t=0: pl.BlockSpec((tk, w), lambda i, t=t: (i, t))
    full_spec = lambda a, b: pl.BlockSpec((a, b), lambda i: (0, 0))
    out = pl.pallas_call(
        _merge_kernel,
        grid=(N // tk,),
        in_specs=[row_spec(ATTN_WIDTH), row_spec(ATTN_WIDTH), row_spec(ATTN_WIDTH),
                  row_spec(V7X_LANES), row_spec(V7X_LANES), row_spec(V7X_LANES),
                  row_spec(ATTN_WIDTH, za_t), row_spec(HGRN_WIDTH),
                  row_spec(D, ga_t), row_spec(D, gb_t), row_spec(D),
                  pl.BlockSpec((None, 1, 3 * D), lambda i: (i // tiles_per_b5, 0, 0)),
                  full_spec(ATTN_WIDTH, D), full_spec(HGRN_WIDTH, D), full_spec(D, D),
                  full_spec(1, D), full_spec(2 * V7X_LANES, ATTN_WIDTH)],
        out_specs=pl.BlockSpec((tk, D), lambda i: (i, 0)),
        out_shape=jax.ShapeDtypeStruct((N, D), F32),
        compiler_params=_cparams(("arbitrary",), _VMEM_MIB["merge"]),
        name="gated_merge",
    )(o1, o2, o3, l1, l2, l3, main, ob, main, main, x2, mod3,
      w_branch_a[0].astype(BF16), w_branch_b[0].astype(BF16), w_out[0].astype(BF16),
      final_g.reshape(1, D), jnp.asarray(expand_mat, BF16))
    return out.reshape(B, S, D)
```

```python
import functools
import math
import types

import numpy as np
import jax
import jax.numpy as jnp
from jax import lax
from jax.experimental import pallas as pl
from jax.experimental.pallas import tpu as pltpu

D_MODEL = 1024
ATTN_HEADS = 8
ATTN_HEAD_DIM = 64
ATTN_WIDTH = ATTN_HEADS * ATTN_HEAD_DIM
DILATED_PATTERNS = ((128, 1), (512, 4), (2048, 16))
ATTN_BLOCK = 128
N_BUCKETS = 32
MAX_DISTANCE = 2048
NEG_INF = -1e30
HGRN_HEADS = 8
HGRN_KEY_DIM = 128
HGRN_VAL_DIM = 128
HGRN_WIDTH = HGRN_HEADS * HGRN_VAL_DIM
EPS = 1e-6

V7X_LANES = 128
V7X_SUBLANES = 8
BF16_ROWS_PER_VREG = 16

F32 = jnp.float32
BF16 = jnp.bfloat16

_MAIN_COLS = {}
_off = 0
for _name, _w in (("qa", ATTN_WIDTH), ("ka", ATTN_WIDTH), ("va", ATTN_WIDTH), ("za", ATTN_WIDTH),
                  ("qb", HGRN_WIDTH), ("ib", HGRN_WIDTH), ("zb", HGRN_WIDTH),
                  ("ga", D_MODEL), ("gb", D_MODEL)):
    _MAIN_COLS[_name] = (_off, _w)
    _off += _w
MAIN_WIDTH = _off
PROJ_TN = 512
PROJ_PIECE_M = 256
PROJ_PIECE_N = 512
PROJ_SKEW = 2
_PROJ_KINDS = (["qscale"] + ["kv"] * 2 + ["silu"] + ["silu"] * 2 + ["forget"] * 2
               + ["id"] * 2 + ["silu"] * 2 + ["sigmoid"] * 4)
_F_TILE0 = _PROJ_KINDS.index("forget")
_F_TILES = _PROJ_KINDS.count("forget")
_QKV_TILES = 3

PERM_DIL = 16
PERM_TILE = 512
PERM_SLAB = PERM_TILE // PERM_DIL
ATTN_SUPER = PERM_DIL * ATTN_BLOCK
ATTN_BLOCKS_PER_ITER = 2
ATTN_SKEW = 2
MERGE_PIECE = 256
MERGE_TILE = 2 * MERGE_PIECE
HGRN_TILE = 512
HGRN_CHUNK = 64
HGRN_CHUNKS_PER_ITER = 8
HGRN_SKEW = 1
LOG2E = math.log2(math.e)
LN2 = math.log(2.0)


def _sigmoid(x):
    return 1.0 / (1.0 + jnp.exp(-x))


def _mod_kernel(c_ref, w_ref, b_ref, o_ref):
    c = c_ref[...]
    sc = c * _sigmoid(c)
    o_ref[...] = jnp.dot(sc, w_ref[...], precision=lax.Precision.HIGHEST,
                         preferred_element_type=F32) + b_ref[...]


def _lower_bound_kernel(hl_ref, o_ref):
    hl = hl_ref[...]
    m = jnp.max(hl, axis=0, keepdims=True)
    e = jnp.exp(hl - m)
    o_ref[...] = e[0:1, :] / jnp.sum(e, axis=0, keepdims=True)


def _bias_table_kernel(rb_ref, bucket_ref, o_ref):
    bk = bucket_ref[...]
    no_prev = lax.broadcasted_iota(jnp.int32, bk.shape, 1) < ATTN_BLOCK
    for h in range(ATTN_HEADS):
        acc = jnp.full(bk.shape, NEG_INF, F32)
        for u in range(N_BUCKETS):
            acc = jnp.where(bk == u, rb_ref[u, h] * LOG2E, acc)
        o_ref[0, h] = acc
        o_ref[1, h] = jnp.where(no_prev, NEG_INF, acc)


def _bucket_tables():
    qi = np.arange(ATTN_BLOCK)[:, None]
    kj = np.arange(2 * ATTN_BLOCK)[None, :]
    delta = qi + ATTN_BLOCK - kj
    max_exact = N_BUCKETS // 2
    tabs = []
    for window, dilation in DILATED_PATTERNS:
        span = window // dilation
        band = (delta >= 0) & (delta <= span)
        dist = np.clip(delta, 0, None) * dilation
        n = dist.astype(np.float32)
        large = max_exact + (np.log(np.maximum(n, 1.0) / max_exact)
                             / math.log(MAX_DISTANCE / max_exact)
                             * (N_BUCKETS - max_exact)).astype(np.int32)
        large = np.minimum(large, N_BUCKETS - 1)
        bucket = np.where(dist < max_exact, dist, large)
        tab = np.where(band, bucket, -1).astype(np.int32)
        if dilation > 1:
            order = _gather_order(dilation)
            cols = np.concatenate([order, ATTN_BLOCK + order])
            tab = tab[order][:, cols]
        tabs.append(tab)
    return np.stack(tabs, 0)


def _gather_order(dilation):
    per_tile = PERM_DIL // dilation
    slab = np.arange(ATTN_BLOCK) // PERM_SLAB
    m = np.arange(ATTN_BLOCK) % PERM_SLAB
    if per_tile == 1:
        return slab * PERM_SLAB + m
    assert per_tile * PERM_SLAB == ATTN_BLOCK
    return per_tile * m + slab


def _inproj_kernel(x_ref, mod_ref, g_ref, lb_ref, w_ref, om_ref, of_ref, op_ref, h_ref, accl_ref):
    tm = x_ref.shape[0]
    x = x_ref[...]
    ms = jnp.mean(x * x, axis=-1, keepdims=True)
    y = x * lax.rsqrt(ms + EPS) * g_ref[...]
    shift = mod_ref[:, 0:D_MODEL]
    scale = mod_ref[:, D_MODEL:2 * D_MODEL]
    h_ref[...] = (y * (1.0 + scale) + shift).astype(BF16)

    n_sub = PROJ_TN // PROJ_PIECE_N
    slab = PROJ_PIECE_M // PERM_DIL

    def write_perm(mc, col0, acc, slot):
        for c in range(PROJ_PIECE_N // V7X_LANES):
            accl_ref[slot, c] = acc[:, c * V7X_LANES:(c + 1) * V7X_LANES]
            for r in range(PERM_DIL):
                rows = accl_ref[slot, c, pl.ds(r, slab, stride=PERM_DIL), :]
                row0 = r * PERM_SLAB + mc * slab
                op_ref[row0:row0 + slab, col0 + c * V7X_LANES:col0 + (c + 1) * V7X_LANES] = rows.astype(BF16)

    def epilogue(j, sub, mc, acc):
        kind = _PROJ_KINDS[j]
        rows = slice(mc * PROJ_PIECE_M, (mc + 1) * PROJ_PIECE_M)
        if kind == "forget":
            col0 = (j - _F_TILE0) * PROJ_TN + sub * PROJ_PIECE_N
            lb = lb_ref[:, col0:col0 + PROJ_PIECE_N]
            of_ref[rows, col0:col0 + PROJ_PIECE_N] = lb + (1.0 - lb) * _sigmoid(acc)
            return
        if kind == "qscale":
            acc = acc * (ATTN_HEAD_DIM ** -0.5 * LOG2E)
        elif kind == "silu":
            acc = acc * _sigmoid(acc)
        elif kind == "sigmoid":
            acc = _sigmoid(acc)
        jm = j if j < _F_TILE0 else j - _F_TILES
        col0 = jm * PROJ_TN + sub * PROJ_PIECE_N
        om_ref[rows, col0:col0 + PROJ_PIECE_N] = acc.astype(BF16)
        if j < _QKV_TILES:
            write_perm(mc, col0, acc, (sub * (tm // PROJ_PIECE_M) + mc) % accl_ref.shape[0])

    items = [(j, sub, mc) for j in range(len(_PROJ_KINDS)) for sub in range(n_sub)
             for mc in range(tm // PROJ_PIECE_M)]
    pending = {}
    for step in range(len(items) + PROJ_SKEW):
        if step < len(items):
            j, sub, mc = items[step]
            col0 = j * PROJ_TN + sub * PROJ_PIECE_N
            pending[step] = jnp.dot(h_ref[mc * PROJ_PIECE_M:(mc + 1) * PROJ_PIECE_M, :],
                                    w_ref[:, col0:col0 + PROJ_PIECE_N], preferred_element_type=F32)
        done = step - PROJ_SKEW
        if done >= 0:
            epilogue(*items[done], pending.pop(done))


def _pair_lanes(hp):
    return slice(hp * V7X_LANES, (hp + 1) * V7X_LANES)


def _attn_scores(q, k, bias_ref, hp, first, lane):
    blk = q.shape[0]
    low = lane < ATTN_HEAD_DIM
    zero = jnp.zeros_like(q)
    q2 = jnp.concatenate([jnp.where(low, q, zero), jnp.where(low, zero, q)], axis=0)
    s = lax.dot_general(q2, k, (((1,), (1,)), ((), ())), preferred_element_type=F32)
    s = s + bias_ref[first, pl.ds(2 * hp, 2)].reshape(2 * blk, 2 * blk)
    return s, jnp.max(s, axis=-1, keepdims=True)


def _attn_values(s, m, v, hp, lane, lse_all):
    blk = s.shape[0] // 2
    p = jnp.exp2(s - m).astype(BF16)
    h0, h1 = 2 * hp, 2 * hp + 1
    low = lane < ATTN_HEAD_DIM
    low_v = lax.broadcasted_iota(jnp.int32, v.shape, 1) < ATTN_HEAD_DIM
    one = jnp.ones_like(v)
    o0 = jnp.dot(p[:blk], jnp.where(low_v, v, one), preferred_element_type=F32)
    o1 = jnp.dot(p[blk:], jnp.where(low_v, one, v), preferred_element_type=F32)
    num = jnp.where(low, o0, o1)
    den_swapped = jnp.where(low, o1, o0)
    den = pltpu.roll(den_swapped, ATTN_HEAD_DIM, 1)
    is_h1 = lane == h1
    lse = jnp.where(is_h1, m[blk:], m[:blk]) * LN2 + jnp.log(jnp.where(is_h1, den_swapped, den))
    lse_all = jnp.where(jnp.logical_or(lane == h0, is_h1), lse, lse_all)
    return num / den, lse_all


def _attn_pipeline(blocks, bias_ref, lane):
    n_pairs = ATTN_HEADS // 2
    items = [(bi, hp) for bi in range(len(blocks)) for hp in range(n_pairs)]
    lse = [jnp.zeros(lane.shape, F32) for _ in blocks]
    pending = {}
    for step in range(len(items) + ATTN_SKEW):
        if step < len(items):
            bi, hp = items[step]
            b = blocks[bi]
            pending[step] = _attn_scores(b.get_q(hp), b.get_k(hp), bias_ref, hp, b.first, lane)
        done = step - ATTN_SKEW
        if done >= 0:
            bi, hp = items[done]
            b = blocks[bi]
            o_pair, lse[bi] = _attn_values(*pending.pop(done), b.get_v(hp), hp, lane, lse[bi])
            b.put_o(hp, o_pair)
            if hp == n_pairs - 1:
                b.put_lse(lse[bi])


def _attn_kernel(q_ref, kp_ref, kc_ref, vp_ref, vc_ref, bias_ref, o_ref, lse_ref, *scratch, dil):
    n = pl.program_id(1)
    blk_rows = ATTN_BLOCK
    lane = lax.broadcasted_iota(jnp.int32, (blk_rows, V7X_LANES), 1)
    prev_rows = kp_ref.shape[0]
    scratch = list(scratch)
    o_scr = scratch.pop(0) if dil > 1 else None
    if scratch:
        kext, vext = scratch
        kext[0:prev_rows] = kp_ref[...]
        kext[prev_rows:] = kc_ref[...]
        vext[0:prev_rows] = vp_ref[...]
        vext[prev_rows:] = vc_ref[...]
        k_prev, k_cur, v_prev, v_cur, cur_off = kext, kext, vext, vext, prev_rows
    else:
        k_prev, k_cur, v_prev, v_cur, cur_off = kp_ref, kc_ref, vp_ref, vc_ref, 0

    def make_block(blk):
        if dil == 1:
            slab = blk_rows
            offs = [blk * blk_rows]
            first = jnp.logical_and(n == 0, blk == 0)
        elif dil == PERM_DIL:
            slab = PERM_SLAB
            offs = [t * PERM_TILE + blk * PERM_SLAB for t in range(blk_rows // slab)]
            tok0 = [t * PERM_TILE + blk for t in range(blk_rows // slab)]
            first = n == 0
        else:
            slab = PERM_SLAB
            tile = lax.shift_right_logical(blk, 2)
            res = jnp.bitwise_and(blk, dil - 1)
            offs = [tile * PERM_TILE + (res + dil * j) * PERM_SLAB for j in range(blk_rows // slab)]
            tok0 = [tile * PERM_TILE + res + dil * j for j in range(blk_rows // slab)]
            first = jnp.logical_and(n == 0, tile == 0)
        offs = [pl.multiple_of(o, slab) for o in offs]

        def slabs(ref, hp, shift=0, rows=slab):
            return [ref[pl.ds(o + shift, rows), _pair_lanes(hp)] for o in offs]

        def put_o(hp, o_pair):
            if dil == 1:
                o_ref[pl.ds(offs[0], slab), _pair_lanes(hp)] = o_pair.astype(BF16)
                return
            for j, t0 in enumerate(tok0):
                o_scr[hp, pl.ds(t0, slab, stride=PERM_DIL), :] = o_pair[j * slab:(j + 1) * slab]

        def put_lse(lse_all):
            if dil == 1:
                lse_ref[pl.ds(offs[0], slab), :] = lse_all
                return
            for j, t0 in enumerate(tok0):
                lse_ref[pl.ds(t0, slab, stride=PERM_DIL), :] = lse_all[j * slab:(j + 1) * slab]

        if dil == 1:
            get_k = lambda hp: slabs(k_cur, hp, rows=2 * slab)[0]
            get_v = lambda hp: slabs(v_cur, hp, rows=2 * slab)[0]
        else:
            get_k = lambda hp: jnp.concatenate(slabs(k_prev, hp) + slabs(k_cur, hp, cur_off), axis=0)
            get_v = lambda hp: jnp.concatenate(slabs(v_prev, hp) + slabs(v_cur, hp, cur_off), axis=0)
        return types.SimpleNamespace(
            get_q=lambda hp: jnp.concatenate(slabs(q_ref, hp), axis=0), get_k=get_k, get_v=get_v,
            put_o=put_o, put_lse=put_lse, first=first.astype(jnp.int32))

    def body(it, carry):
        _attn_pipeline([make_block(it * ATTN_BLOCKS_PER_ITER + u) for u in range(ATTN_BLOCKS_PER_ITER)],
                       bias_ref, lane)
        return carry

    lax.fori_loop(0, ATTN_SUPER // ATTN_BLOCK // ATTN_BLOCKS_PER_ITER, body, 0)
    if dil > 1:
        for hp in range(ATTN_HEADS // 2):
            o_ref[:, _pair_lanes(hp)] = o_scr[hp].astype(BF16)


def _roll_rows(a, shift):
    rows = a.shape[0]
    if abs(shift) >= V7X_SUBLANES:
        return pltpu.roll(a, shift % rows, 0)
    grouped = a.reshape(rows // V7X_SUBLANES, V7X_SUBLANES, a.shape[1])
    return pltpu.roll(grouped, shift % V7X_SUBLANES, 1).reshape(a.shape)


def _hgrn_kernel(q_ref, f_ref, i_ref, z_ref, g_ref, o_ref, state_ref, *, chunk, n_chunks):
    @pl.when(pl.program_id(1) == 0)
    def _():
        state_ref[...] = jnp.zeros_like(state_ref)

    C = chunk
    n_levels = C.bit_length() - 1
    g_on = g_ref[...]
    nt = (((1,), (1,)), ((), ()))

    def chunk_body(ci, carry):
        row = lax.broadcasted_iota(jnp.int32, (C, HGRN_KEY_DIM), 0)
        odds = [jnp.bitwise_and(row, 1 << lvl) != 0 for lvl in range(n_levels)]
        tt = lax.broadcasted_iota(jnp.int32, (C, C), 0)
        ss = lax.broadcasted_iota(jnp.int32, (C, C), 1)
        owner = jnp.where(tt > ss, 32 - lax.clz(jnp.bitwise_xor(tt, ss)),
                          jnp.where(tt == ss, 0, -1))
        owned = [owner == lvl for lvl in range(n_levels + 1)]

        def chunk_rows(u):
            return pl.ds(pl.multiple_of((ci * HGRN_CHUNKS_PER_ITER + u) * C, C), C)

        def scan(u, h):
            rows = chunk_rows(u)
            hs = slice(h * HGRN_KEY_DIM, (h + 1) * HGRN_KEY_DIM)
            f = f_ref[rows, hs]
            q = q_ref[rows, hs]
            k = (1.0 - f).astype(BF16)
            a = jnp.where(owned[0], lax.dot_general(q, k, nt, preferred_element_type=F32), 0.0)
            x = jnp.where(odds[0], f, 1.0)
            y = jnp.where(odds[0], 1.0, f)
            for lvl in range(n_levels):
                m = 1 << lvl
                e = x.astype(BF16)
                if m % BF16_ROWS_PER_VREG:
                    pm = lax.dot_general(q * e, k * e, nt, preferred_element_type=F32)
                    a = jnp.where(owned[lvl + 1], pm, a)
                else:
                    odd_rows = [slice(b * m, (b + 1) * m) for b in range(1, C // m, 2)]
                    qe = jnp.concatenate([q[r] * e[r] for r in odd_rows], axis=0)
                    pm = lax.dot_general(qe, k * e, nt, preferred_element_type=F32)
                    slabs = [a[b * m:(b + 1) * m] for b in range(C // m)]
                    for i, r in enumerate(odd_rows):
                        slabs[2 * i + 1] = jnp.where(owned[lvl + 1][r], pm[i * m:(i + 1) * m], a[r])
                    a = jnp.concatenate(slabs, axis=0)
                tot = x * y if lvl else f
                if 2 * m == V7X_SUBLANES:
                    partner = _roll_rows(tot, m)
                else:
                    partner = jnp.where(odds[lvl], _roll_rows(tot, m), _roll_rows(tot, -m))
                z = x * partner
                keep = odds[lvl] == odds[lvl + 1] if lvl + 1 < n_levels else jnp.logical_not(odds[lvl])
                x, y = jnp.where(keep, z, y), jnp.where(keep, y, z)
            return a.astype(BF16), q * y.astype(BF16), k * x.astype(BF16), x[0:1, :] * y[0:1, :]

        def finish(u, h, a, q_dec, k_dec, decay):
            rows = chunk_rows(u)
            hs = slice(h * HGRN_KEY_DIM, (h + 1) * HGRN_KEY_DIM)
            v = i_ref[rows, hs]
            st = state_ref[h]
            o = (jnp.dot(a, v, preferred_element_type=F32)
                 + lax.dot_general(q_dec, st.astype(BF16), nt, preferred_element_type=F32))
            upd = lax.dot_general(v, k_dec, (((0,), (0,)), ((), ())), preferred_element_type=F32)
            state_ref[h] = st * decay + upd
            ms = jnp.mean(o * o, axis=-1, keepdims=True)
            y = o * lax.rsqrt(ms + EPS) * g_on
            o_ref[rows, hs] = (y * z_ref[rows, hs].astype(F32)).astype(BF16)

        items = [(u, h) for u in range(HGRN_CHUNKS_PER_ITER) for h in range(HGRN_HEADS)]
        pending = {}
        for step in range(len(items) + HGRN_SKEW):
            if step < len(items):
                pending[step] = scan(*items[step])
            done = step - HGRN_SKEW
            if done >= 0:
                finish(*items[done], *pending.pop(done))
        return carry

    assert n_chunks % HGRN_CHUNKS_PER_ITER == 0
    lax.fori_loop(0, n_chunks // HGRN_CHUNKS_PER_ITER, chunk_body, 0)


def _merge_kernel(o1_ref, o2_ref, o3_ref, l1_ref, l2_ref, l3_ref, za_ref, ob_ref,
                  sga_ref, sgb_ref, x_ref, mod_ref, wa_ref, wb_ref, wo_ref, fg_ref,
                  ex_ref, out_ref):
    ex = ex_ref[...]
    gate = mod_ref[:, 2 * D_MODEL:3 * D_MODEL]

    def expand(w):
        hi = w.astype(BF16)
        lo = (w - hi.astype(F32)).astype(BF16)
        return jnp.dot(jnp.concatenate([hi, lo], axis=1), ex, preferred_element_type=F32)

    def mix(rows):
        l1, l2, l3 = l1_ref[rows, :], l2_ref[rows, :], l3_ref[rows, :]
        mx = jnp.maximum(jnp.maximum(l1, l2), l3)
        e1, e2, e3 = jnp.exp(l1 - mx), jnp.exp(l2 - mx), jnp.exp(l3 - mx)
        inv = 1.0 / (e1 + e2 + e3)
        oa = (expand(e1 * inv) * o1_ref[rows, :].astype(F32)
              + expand(e2 * inv) * o2_ref[rows, :].astype(F32)
              + expand(e3 * inv) * o3_ref[rows, :].astype(F32))
        return (oa * za_ref[rows, :].astype(F32)).astype(BF16)

    def branches(rows, oa):
        ya = jnp.dot(oa, wa_ref[...], preferred_element_type=F32)
        yb = jnp.dot(ob_ref[rows, :], wb_ref[...], preferred_element_type=F32)
        return (sga_ref[rows, :].astype(F32) * ya + sgb_ref[rows, :].astype(F32) * yb).astype(BF16)

    def project(rows, y):
        z = jnp.dot(y, wo_ref[...], preferred_element_type=F32)
        xo = x_ref[rows, :] + gate * z
        ms = jnp.mean(xo * xo, axis=-1, keepdims=True)
        out_ref[rows, :] = xo * lax.rsqrt(ms + EPS) * fg_ref[...]

    pieces = [slice(r, r + MERGE_PIECE) for r in range(0, x_ref.shape[0], MERGE_PIECE)]
    oas = [mix(rows) for rows in pieces]
    ys = [branches(rows, oa) for rows, oa in zip(pieces, oas)]
    for rows, y in zip(pieces, ys):
        project(rows, y)


_VMEM_MIB = {"inproj": 56, "attn": 48, "hgrn": 32, "merge": 40}


def _cparams(sem, vmem_mib):
    return pltpu.CompilerParams(dimension_semantics=sem,
                                vmem_limit_bytes=vmem_mib * 1024 * 1024)


def kernel(x, c, w_ada, b_ada, norm_g, w_in, hgrn_onorm_g, w_branch_a, w_branch_b, w_out,
           rel_bias, hgrn_lb, final_g):
    B, S, D = x.shape
    assert D == D_MODEL and w_ada.shape[0] == 1, "single-layer kernel"
    N = B * S
    x2 = x.reshape(N, D)

    c8 = jnp.pad(c, ((0, 8 - B), (0, 0)))
    mod = pl.pallas_call(
        _mod_kernel,
        grid=(3 * D // 512,),
        in_specs=[pl.BlockSpec((8, D), lambda j: (0, 0)),
                  pl.BlockSpec((D, 512), lambda j: (0, j)),
                  pl.BlockSpec((1, 512), lambda j: (0, j))],
        out_specs=pl.BlockSpec((8, 512), lambda j: (0, j)),
        out_shape=jax.ShapeDtypeStruct((8, 3 * D), F32),
        name="adaln_mod",
    )(c8, w_ada[0], b_ada[0].reshape(1, 3 * D))
    mod3 = mod.reshape(8, 1, 3 * D)

    lb = pl.pallas_call(
        _lower_bound_kernel,
        out_shape=jax.ShapeDtypeStruct((1, HGRN_WIDTH), F32),
        name="hgrn_lower_bound",
    )(hgrn_lb)

    n_pat = len(DILATED_PATTERNS)
    bias_tab = pl.pallas_call(
        _bias_table_kernel,
        grid=(n_pat,),
        in_specs=[pl.BlockSpec(memory_space=pltpu.SMEM),
                  pl.BlockSpec((None, ATTN_BLOCK, 2 * ATTN_BLOCK), lambda g: (g, 0, 0))],
        out_specs=pl.BlockSpec((None, 2, ATTN_HEADS, ATTN_BLOCK, 2 * ATTN_BLOCK),
                               lambda g: (g, 0, 0, 0, 0)),
        out_shape=jax.ShapeDtypeStruct((n_pat, 2, ATTN_HEADS, ATTN_BLOCK, 2 * ATTN_BLOCK), F32),
        name="rel_bias_table",
    )(rel_bias, jnp.asarray(_bucket_tables()))

    tm = PERM_TILE
    tiles_per_b = S // tm
    in_width = len(_PROJ_KINDS) * PROJ_TN
    assert PROJ_TN == ATTN_WIDTH and w_in.shape[2] == in_width
    resident = dict(pipeline_mode=pl.Buffered(1))
    main, fgate, qkv_p = pl.pallas_call(
        _inproj_kernel,
        grid=(N // tm,),
        in_specs=[pl.BlockSpec((tm, D), lambda i: (i, 0)),
                  pl.BlockSpec((None, 1, 3 * D), lambda i: (i // tiles_per_b, 0, 0)),
                  pl.BlockSpec((1, D), lambda i: (0, 0)),
                  pl.BlockSpec((1, HGRN_WIDTH), lambda i: (0, 0)),
                  pl.BlockSpec((D, in_width), lambda i: (0, 0), **resident)],
        out_specs=[pl.BlockSpec((tm, MAIN_WIDTH), lambda i: (i, 0)),
                   pl.BlockSpec((tm, HGRN_WIDTH), lambda i: (i, 0)),
                   pl.BlockSpec((tm, _QKV_TILES * ATTN_WIDTH), lambda i: (i, 0))],
        out_shape=[jax.ShapeDtypeStruct((N, MAIN_WIDTH), BF16),
                   jax.ShapeDtypeStruct((N, HGRN_WIDTH), F32),
                   jax.ShapeDtypeStruct((N, _QKV_TILES * ATTN_WIDTH), BF16)],
        scratch_shapes=[pltpu.VMEM((tm, D), BF16),
                        pltpu.VMEM((2, PROJ_PIECE_N // V7X_LANES, PROJ_PIECE_M, V7X_LANES), F32)],
        compiler_params=_cparams(("arbitrary",), _VMEM_MIB["inproj"]),
        name="inproj",
    )(x2, mod3, norm_g[0].reshape(1, D), lb, w_in[0].astype(BF16))

    qa_t = _MAIN_COLS["qa"][0] // ATTN_WIDTH
    ka_t = _MAIN_COLS["ka"][0] // ATTN_WIDTH
    va_t = _MAIN_COLS["va"][0] // ATTN_WIDTH
    attn_outs = []
    main_b = main.reshape(B, S, MAIN_WIDTH)
    qkv_pv = qkv_p.reshape(B, S, _QKV_TILES * ATTN_WIDTH)
    n_super = S // ATTN_SUPER
    for g, (window, dil) in enumerate(DILATED_PATTERNS):
        assert window // dil == ATTN_BLOCK
        src, tiles = (main_b, (qa_t, ka_t, va_t)) if dil == 1 else (qkv_pv, (0, 1, 2))
        prev_rows = {1: ATTN_BLOCK, PERM_DIL: ATTN_SUPER}.get(dil, PERM_TILE)
        per_step = ATTN_SUPER // prev_rows

        def cur_spec(t):
            return pl.BlockSpec((None, ATTN_SUPER, ATTN_WIDTH), lambda b, n, t=t: (b, n, t))

        def prev_spec(t, prev_rows=prev_rows, per_step=per_step):
            return pl.BlockSpec((None, prev_rows, ATTN_WIDTH),
                                lambda b, n, t=t: (b, jnp.maximum(n * per_step - 1, 0), t))

        scratch = []
        if dil > 1:
            scratch += [pltpu.VMEM((ATTN_WIDTH // V7X_LANES, ATTN_SUPER, V7X_LANES), F32)]
        if dil != PERM_DIL:
            scratch += [pltpu.VMEM((prev_rows + ATTN_SUPER, ATTN_WIDTH), BF16)] * 2
        o_g, lse_g = pl.pallas_call(
            functools.partial(_attn_kernel, dil=dil),
            grid=(B, n_super),
            in_specs=[cur_spec(tiles[0]), prev_spec(tiles[1]), cur_spec(tiles[1]),
                      prev_spec(tiles[2]), cur_spec(tiles[2]),
                      pl.BlockSpec((None, 2, ATTN_HEADS, ATTN_BLOCK, 2 * ATTN_BLOCK),
                                   lambda b, n, g=g: (g, 0, 0, 0, 0))],
            out_specs=[pl.BlockSpec((None, ATTN_SUPER, ATTN_WIDTH), lambda b, n: (b, n, 0)),
                       pl.BlockSpec((None, ATTN_SUPER, V7X_LANES), lambda b, n: (b, n, 0))],
            out_shape=[jax.ShapeDtypeStruct((B, S, ATTN_WIDTH), BF16),
                       jax.ShapeDtypeStruct((B, S, V7X_LANES), F32)],
            scratch_shapes=scratch,
            compiler_params=_cparams(("arbitrary", "arbitrary"), _VMEM_MIB["attn"]),
            name=f"dilated_attn_d{dil}",
        )(src, src, src, src, src, bias_tab)
        attn_outs.append((o_g.reshape(N, ATTN_WIDTH), lse_g.reshape(N, V7X_LANES)))

    th = HGRN_TILE
    chunk = HGRN_CHUNK
    hw_t = HGRN_WIDTH
    qb_t = _MAIN_COLS["qb"][0] // hw_t
    ib_t = _MAIN_COLS["ib"][0] // hw_t
    zb_t = _MAIN_COLS["zb"][0] // hw_t
    ob = pl.pallas_call(
        functools.partial(_hgrn_kernel, chunk=chunk, n_chunks=th // chunk),
        grid=(B, S // th),
        in_specs=[pl.BlockSpec((None, th, hw_t), lambda b, s: (b, s, qb_t)),
                  pl.BlockSpec((None, th, hw_t), lambda b, s: (b, s, 0)),
                  pl.BlockSpec((None, th, hw_t), lambda b, s: (b, s, ib_t)),
                  pl.BlockSpec((None, th, hw_t), lambda b, s: (b, s, zb_t)),
                  pl.BlockSpec((1, HGRN_VAL_DIM), lambda b, s: (0, 0))],
        out_specs=pl.BlockSpec((None, th, hw_t), lambda b, s: (b, s, 0)),
        out_shape=jax.ShapeDtypeStruct((B, S, hw_t), BF16),
        scratch_shapes=[pltpu.VMEM((HGRN_HEADS, HGRN_VAL_DIM, HGRN_KEY_DIM), F32)],
        compiler_params=_cparams(("arbitrary", "arbitrary"), _VMEM_MIB["hgrn"]),
        name="hgrn2",
    )(main_b, fgate.reshape(B, S, hw_t), main_b, main_b, hgrn_onorm_g[0].reshape(1, HGRN_VAL_DIM))
    ob = ob.reshape(N, hw_t)

    tk = MERGE_TILE
    tiles_per_b5 = S // tk
    za_t = _MAIN_COLS["za"][0] // ATTN_WIDTH
    ga_t = _MAIN_COLS["ga"][0] // D
    gb_t = _MAIN_COLS["gb"][0] // D
    expand_mat = np.zeros((2 * V7X_LANES, ATTN_WIDTH), np.float32)
    for h in range(ATTN_HEADS):
        expand_mat[h, h * ATTN_HEAD_DIM:(h + 1) * ATTN_HEAD_DIM] = 1.0
        expand_mat[V7X_LANES + h, h * ATTN_HEAD_DIM:(h + 1) * ATTN_HEAD_DIM] = 1.0
    (o1, l1), (o2, l2), (o3, l3) = attn_outs
    def row_spec(w, t=0, deep=False):
        mode = dict(pipeline_mode=pl.Buffered(3)) if deep else {}
        return pl.BlockSpec((tk, w), lambda i, t=t: (i, t), **mode)

    stream_specs = [row_spec(ATTN_WIDTH), row_spec(ATTN_WIDTH), row_spec(ATTN_WIDTH),
                    row_spec(V7X_LANES), row_spec(V7X_LANES), row_spec(V7X_LANES),
                    row_spec(ATTN_WIDTH, za_t), row_spec(HGRN_WIDTH, deep=True),
                    row_spec(D, ga_t, deep=True), row_spec(D, gb_t, deep=True), row_spec(D, deep=True),
                    pl.BlockSpec((None, 1, 3 * D), lambda i: (i // tiles_per_b5, 0, 0))]
    n_stream = len(stream_specs)

    def merge_call(*refs):
        streams, (wa, wb, wo, fg, ex, out_hbm) = refs[:n_stream], refs[n_stream:]

        def step(*blocks):
            _merge_kernel(*blocks[:n_stream], wa, wb, wo, fg, ex, blocks[n_stream])

        pltpu.emit_pipeline(step, grid=(N // tk,), in_specs=stream_specs,
                            out_specs=[pl.BlockSpec((tk, D), lambda i: (i, 0))])(*streams, out_hbm)

    hbm = pl.BlockSpec(memory_space=pl.ANY)
    vmem = pl.BlockSpec(memory_space=pltpu.VMEM)
    out = pl.pallas_call(
        merge_call,
        in_specs=[hbm] * n_stream + [vmem] * 5,
        out_specs=hbm,
        out_shape=jax.ShapeDtypeStruct((N, D), F32),
        compiler_params=pltpu.CompilerParams(vmem_limit_bytes=_VMEM_MIB["merge"] * 1024 * 1024),
        name="gated_merge",
    )(o1, o2, o3, l1, l2, l3, main, ob, main, main, x2, mod3,
      w_branch_a[0].astype(BF16), w_branch_b[0].astype(BF16), w_out[0].astype(BF16),
      final_g.reshape(1, D), jnp.asarray(expand_mat, BF16))
    return out.reshape(B, S, D)
```

```python
import functools
import math
import types

import numpy as np
import jax
import jax.numpy as jnp
from jax import lax
from jax.experimental import pallas as pl
from jax.experimental.pallas import tpu as pltpu

D_MODEL = 1024
ATTN_HEADS = 8
ATTN_HEAD_DIM = 64
ATTN_WIDTH = ATTN_HEADS * ATTN_HEAD_DIM
DILATED_PATTERNS = ((128, 1), (512, 4), (2048, 16))
ATTN_BLOCK = 128
N_BUCKETS = 32
MAX_DISTANCE = 2048
NEG_INF = -1e30
HGRN_HEADS = 8
HGRN_KEY_DIM = 128
HGRN_VAL_DIM = 128
HGRN_WIDTH = HGRN_HEADS * HGRN_VAL_DIM
EPS = 1e-6

V7X_LANES = 128
V7X_SUBLANES = 8
BF16_ROWS_PER_VREG = 16

F32 = jnp.float32
BF16 = jnp.bfloat16

_MAIN_COLS = {}
_off = 0
for _name, _w in (("qa", ATTN_WIDTH), ("ka", ATTN_WIDTH), ("va", ATTN_WIDTH), ("za", ATTN_WIDTH),
                  ("qb", HGRN_WIDTH), ("ib", HGRN_WIDTH), ("zb", HGRN_WIDTH),
                  ("ga", D_MODEL), ("gb", D_MODEL)):
    _MAIN_COLS[_name] = (_off, _w)
    _off += _w
MAIN_WIDTH = _off
PROJ_TN = 512
PROJ_PIECE_M = 256
PROJ_PIECE_N = 512
PROJ_SKEW = 2
_PROJ_KINDS = (["qscale"] + ["kv"] * 2 + ["silu"] + ["silu"] * 2 + ["forget"] * 2
               + ["id"] * 2 + ["silu"] * 2 + ["sigmoid"] * 4)
_F_TILE0 = _PROJ_KINDS.index("forget")
_F_TILES = _PROJ_KINDS.count("forget")
_QKV_TILES = 3

PERM_DIL = 16
PERM_TILE = 512
PERM_SLAB = PERM_TILE // PERM_DIL
ATTN_SUPER = PERM_DIL * ATTN_BLOCK
ATTN_BLOCKS_PER_ITER = 2
ATTN_SKEW = 2
MERGE_PIECE = 256
MERGE_TILE = 2 * MERGE_PIECE
HGRN_TILE = 512
HGRN_CHUNK = 64
HGRN_CHUNKS_PER_ITER = 8
HGRN_SKEW = 1
LOG2E = math.log2(math.e)
LN2 = math.log(2.0)


def _sigmoid(x):
    return 1.0 / (1.0 + jnp.exp(-x))


def _mod_kernel(c_ref, w_ref, b_ref, o_ref):
    c = c_ref[...]
    sc = c * _sigmoid(c)
    o_ref[...] = jnp.dot(sc, w_ref[...], precision=lax.Precision.HIGHEST,
                         preferred_element_type=F32) + b_ref[...]


def _lower_bound_kernel(hl_ref, o_ref):
    hl = hl_ref[...]
    m = jnp.max(hl, axis=0, keepdims=True)
    e = jnp.exp(hl - m)
    o_ref[...] = e[0:1, :] / jnp.sum(e, axis=0, keepdims=True)


def _bias_table_kernel(rb_ref, bucket_ref, o_ref):
    bk = bucket_ref[...]
    no_prev = lax.broadcasted_iota(jnp.int32, bk.shape, 1) < ATTN_BLOCK
    for h in range(ATTN_HEADS):
        acc = jnp.full(bk.shape, NEG_INF, F32)
        for u in range(N_BUCKETS):
            acc = jnp.where(bk == u, rb_ref[u, h] * LOG2E, acc)
        o_ref[0, h] = acc
        o_ref[1, h] = jnp.where(no_prev, NEG_INF, acc)


def _bucket_tables():
    qi = np.arange(ATTN_BLOCK)[:, None]
    kj = np.arange(2 * ATTN_BLOCK)[None, :]
    delta = qi + ATTN_BLOCK - kj
    max_exact = N_BUCKETS // 2
    tabs = []
    for window, dilation in DILATED_PATTERNS:
        span = window // dilation
        band = (delta >= 0) & (delta <= span)
        dist = np.clip(delta, 0, None) * dilation
        n = dist.astype(np.float32)
        large = max_exact + (np.log(np.maximum(n, 1.0) / max_exact)
                             / math.log(MAX_DISTANCE / max_exact)
                             * (N_BUCKETS - max_exact)).astype(np.int32)
        large = np.minimum(large, N_BUCKETS - 1)
        bucket = np.where(dist < max_exact, dist, large)
        tab = np.where(band, bucket, -1).astype(np.int32)
        if dilation > 1:
            order = _gather_order(dilation)
            cols = np.concatenate([order, ATTN_BLOCK + order])
            tab = tab[order][:, cols]
        tabs.append(tab)
    return np.stack(tabs, 0)


def _gather_order(dilation):
    per_tile = PERM_DIL // dilation
    slab = np.arange(ATTN_BLOCK) // PERM_SLAB
    m = np.arange(ATTN_BLOCK) % PERM_SLAB
    if per_tile == 1:
        return slab * PERM_SLAB + m
    assert per_tile * PERM_SLAB == ATTN_BLOCK
    return per_tile * m + slab


def _inproj_kernel(x_ref, mod_ref, g_ref, lb_ref, w_ref, om_ref, of_ref, op_ref, h_ref, accl_ref):
    tm = x_ref.shape[0]
    x = x_ref[...]
    ms = jnp.mean(x * x, axis=-1, keepdims=True)
    y = x * lax.rsqrt(ms + EPS) * g_ref[...]
    shift = mod_ref[:, 0:D_MODEL]
    scale = mod_ref[:, D_MODEL:2 * D_MODEL]
    h_ref[...] = (y * (1.0 + scale) + shift).astype(BF16)

    n_sub = PROJ_TN // PROJ_PIECE_N
    slab = PROJ_PIECE_M // PERM_DIL

    def write_perm(mc, col0, acc, slot):
        for c in range(PROJ_PIECE_N // V7X_LANES):
            accl_ref[slot, c] = acc[:, c * V7X_LANES:(c + 1) * V7X_LANES]
            for r in range(PERM_DIL):
                rows = accl_ref[slot, c, pl.ds(r, slab, stride=PERM_DIL), :]
                row0 = r * PERM_SLAB + mc * slab
                op_ref[row0:row0 + slab, col0 + c * V7X_LANES:col0 + (c + 1) * V7X_LANES] = rows.astype(BF16)

    def epilogue(j, sub, mc, acc):
        kind = _PROJ_KINDS[j]
        rows = slice(mc * PROJ_PIECE_M, (mc + 1) * PROJ_PIECE_M)
        if kind == "forget":
            col0 = (j - _F_TILE0) * PROJ_TN + sub * PROJ_PIECE_N
            lb = lb_ref[:, col0:col0 + PROJ_PIECE_N]
            of_ref[rows, col0:col0 + PROJ_PIECE_N] = lb + (1.0 - lb) * _sigmoid(acc)
            return
        if kind == "qscale":
            acc = acc * (ATTN_HEAD_DIM ** -0.5 * LOG2E)
        elif kind == "silu":
            acc = acc * _sigmoid(acc)
        elif kind == "sigmoid":
            acc = _sigmoid(acc)
        jm = j if j < _F_TILE0 else j - _F_TILES
        col0 = jm * PROJ_TN + sub * PROJ_PIECE_N
        om_ref[rows, col0:col0 + PROJ_PIECE_N] = acc.astype(BF16)
        if j < _QKV_TILES:
            write_perm(mc, col0, acc, (sub * (tm // PROJ_PIECE_M) + mc) % accl_ref.shape[0])

    items = [(j, sub, mc) for j in range(len(_PROJ_KINDS)) for sub in range(n_sub)
             for mc in range(tm // PROJ_PIECE_M)]
    pending = {}
    for step in range(len(items) + PROJ_SKEW):
        if step < len(items):
            j, sub, mc = items[step]
            col0 = j * PROJ_TN + sub * PROJ_PIECE_N
            pending[step] = jnp.dot(h_ref[mc * PROJ_PIECE_M:(mc + 1) * PROJ_PIECE_M, :],
                                    w_ref[:, col0:col0 + PROJ_PIECE_N], preferred_element_type=F32)
        done = step - PROJ_SKEW
        if done >= 0:
            epilogue(*items[done], pending.pop(done))


def _pair_lanes(hp):
    return slice(hp * V7X_LANES, (hp + 1) * V7X_LANES)


def _attn_scores(q, k, bias_ref, hp, first, lane):
    blk = q.shape[0]
    low = lane < ATTN_HEAD_DIM
    zero = jnp.zeros_like(q)
    q2 = jnp.concatenate([jnp.where(low, q, zero), jnp.where(low, zero, q)], axis=0)
    s = lax.dot_general(q2, k, (((1,), (1,)), ((), ())), preferred_element_type=F32)
    s = s + bias_ref[first, pl.ds(2 * hp, 2)].reshape(2 * blk, 2 * blk)
    return s, jnp.max(s, axis=-1, keepdims=True)


def _attn_values(s, m, v, hp, lane, lse_all):
    blk = s.shape[0] // 2
    p = jnp.exp2(s - m).astype(BF16)
    h0, h1 = 2 * hp, 2 * hp + 1
    low = lane < ATTN_HEAD_DIM
    low_v = lax.broadcasted_iota(jnp.int32, v.shape, 1) < ATTN_HEAD_DIM
    one = jnp.ones_like(v)
    o0 = jnp.dot(p[:blk], jnp.where(low_v, v, one), preferred_element_type=F32)
    o1 = jnp.dot(p[blk:], jnp.where(low_v, one, v), preferred_element_type=F32)
    num = jnp.where(low, o0, o1)
    den_swapped = jnp.where(low, o1, o0)
    den = pltpu.roll(den_swapped, ATTN_HEAD_DIM, 1)
    is_h1 = lane == h1
    lse = jnp.where(is_h1, m[blk:], m[:blk]) * LN2 + jnp.log(jnp.where(is_h1, den_swapped, den))
    lse_all = jnp.where(jnp.logical_or(lane == h0, is_h1), lse, lse_all)
    return num / den, lse_all


def _attn_pipeline(blocks, bias_ref, lane):
    n_pairs = ATTN_HEADS // 2
    items = [(bi, hp) for bi in range(len(blocks)) for hp in range(n_pairs)]
    lse = [jnp.zeros(lane.shape, F32) for _ in blocks]
    pending = {}
    for step in range(len(items) + ATTN_SKEW):
        if step < len(items):
            bi, hp = items[step]
            b = blocks[bi]
            pending[step] = _attn_scores(b.get_q(hp), b.get_k(hp), bias_ref, hp, b.first, lane)
        done = step - ATTN_SKEW
        if done >= 0:
            bi, hp = items[done]
            b = blocks[bi]
            o_pair, lse[bi] = _attn_values(*pending.pop(done), b.get_v(hp), hp, lane, lse[bi])
            b.put_o(hp, o_pair)
            if hp == n_pairs - 1:
                b.put_lse(lse[bi])


def _attn_kernel(q_ref, kp_ref, kc_ref, vp_ref, vc_ref, bias_ref, o_ref, lse_ref, *scratch, dil):
    n = pl.program_id(1)
    blk_rows = ATTN_BLOCK
    lane = lax.broadcasted_iota(jnp.int32, (blk_rows, V7X_LANES), 1)
    prev_rows = kp_ref.shape[0]
    scratch = list(scratch)
    o_scr = scratch.pop(0) if dil > 1 else None
    if scratch:
        kext, vext = scratch
        kext[0:prev_rows] = kp_ref[...]
        kext[prev_rows:] = kc_ref[...]
        vext[0:prev_rows] = vp_ref[...]
        vext[prev_rows:] = vc_ref[...]
        k_prev, k_cur, v_prev, v_cur, cur_off = kext, kext, vext, vext, prev_rows
    else:
        k_prev, k_cur, v_prev, v_cur, cur_off = kp_ref, kc_ref, vp_ref, vc_ref, 0

    def make_block(blk):
        if dil == 1:
            slab = blk_rows
            offs = [blk * blk_rows]
            first = jnp.logical_and(n == 0, blk == 0)
        elif dil == PERM_DIL:
            slab = PERM_SLAB
            offs = [t * PERM_TILE + blk * PERM_SLAB for t in range(blk_rows // slab)]
            tok0 = [t * PERM_TILE + blk for t in range(blk_rows // slab)]
            first = n == 0
        else:
            slab = PERM_SLAB
            tile = lax.shift_right_logical(blk, 2)
            res = jnp.bitwise_and(blk, dil - 1)
            offs = [tile * PERM_TILE + (res + dil * j) * PERM_SLAB for j in range(blk_rows // slab)]
            tok0 = [tile * PERM_TILE + res + dil * j for j in range(blk_rows // slab)]
            first = jnp.logical_and(n == 0, tile == 0)
        offs = [pl.multiple_of(o, slab) for o in offs]

        def slabs(ref, hp, shift=0, rows=slab):
            return [ref[pl.ds(o + shift, rows), _pair_lanes(hp)] for o in offs]

        def put_o(hp, o_pair):
            if dil == 1:
                o_ref[pl.ds(offs[0], slab), _pair_lanes(hp)] = o_pair.astype(BF16)
                return
            for j, t0 in enumerate(tok0):
                o_scr[hp, pl.ds(t0, slab, stride=PERM_DIL), :] = o_pair[j * slab:(j + 1) * slab]

        def put_lse(lse_all):
            if dil == 1:
                lse_ref[pl.ds(offs[0], slab), :] = lse_all
                return
            for j, t0 in enumerate(tok0):
                lse_ref[pl.ds(t0, slab, stride=PERM_DIL), :] = lse_all[j * slab:(j + 1) * slab]

        if dil == 1:
            get_k = lambda hp: slabs(k_cur, hp, rows=2 * slab)[0]
            get_v = lambda hp: slabs(v_cur, hp, rows=2 * slab)[0]
        else:
            get_k = lambda hp: jnp.concatenate(slabs(k_prev, hp) + slabs(k_cur, hp, cur_off), axis=0)
            get_v = lambda hp: jnp.concatenate(slabs(v_prev, hp) + slabs(v_cur, hp, cur_off), axis=0)
        return types.SimpleNamespace(
            get_q=lambda hp: jnp.concatenate(slabs(q_ref, hp), axis=0), get_k=get_k, get_v=get_v,
            put_o=put_o, put_lse=put_lse, first=first.astype(jnp.int32))

    def body(it, carry):
        _attn_pipeline([make_block(it * ATTN_BLOCKS_PER_ITER + u) for u in range(ATTN_BLOCKS_PER_ITER)],
                       bias_ref, lane)
        return carry

    lax.fori_loop(0, ATTN_SUPER // ATTN_BLOCK // ATTN_BLOCKS_PER_ITER, body, 0)
    if dil > 1:
        for hp in range(ATTN_HEADS // 2):
            o_ref[:, _pair_lanes(hp)] = o_scr[hp].astype(BF16)


def _roll_rows(a, shift):
    rows = a.shape[0]
    if abs(shift) >= V7X_SUBLANES:
        return pltpu.roll(a, shift % rows, 0)
    grouped = a.reshape(rows // V7X_SUBLANES, V7X_SUBLANES, a.shape[1])
    return pltpu.roll(grouped, shift % V7X_SUBLANES, 1).reshape(a.shape)


def _hgrn_kernel(q_ref, f_ref, i_ref, z_ref, g_ref, o_ref, state_ref, *, chunk, n_chunks):
    @pl.when(pl.program_id(1) == 0)
    def _():
        state_ref[...] = jnp.zeros_like(state_ref)

    C = chunk
    n_levels = C.bit_length() - 1
    g_on = g_ref[...]
    nt = (((1,), (1,)), ((), ()))

    def chunk_body(ci, carry):
        row = lax.broadcasted_iota(jnp.int32, (C, HGRN_KEY_DIM), 0)
        odds = [jnp.bitwise_and(row, 1 << lvl) != 0 for lvl in range(n_levels)]
        tt = lax.broadcasted_iota(jnp.int32, (C, C), 0)
        ss = lax.broadcasted_iota(jnp.int32, (C, C), 1)
        owner = jnp.where(tt > ss, 32 - lax.clz(jnp.bitwise_xor(tt, ss)),
                          jnp.where(tt == ss, 0, -1))
        owned = [owner == lvl for lvl in range(n_levels + 1)]

        def chunk_rows(u):
            return pl.ds(pl.multiple_of((ci * HGRN_CHUNKS_PER_ITER + u) * C, C), C)

        def scan(u, h):
            rows = chunk_rows(u)
            hs = slice(h * HGRN_KEY_DIM, (h + 1) * HGRN_KEY_DIM)
            f = f_ref[rows, hs]
            q = q_ref[rows, hs]
            k = (1.0 - f).astype(BF16)
            a = jnp.where(owned[0], lax.dot_general(q, k, nt, preferred_element_type=F32), 0.0)
            x = jnp.where(odds[0], f, 1.0)
            y = jnp.where(odds[0], 1.0, f)
            for lvl in range(n_levels):
                m = 1 << lvl
                e = x.astype(BF16)
                if m % BF16_ROWS_PER_VREG:
                    pm = lax.dot_general(q * e, k * e, nt, preferred_element_type=F32)
                    a = jnp.where(owned[lvl + 1], pm, a)
                else:
                    odd_rows = [slice(b * m, (b + 1) * m) for b in range(1, C // m, 2)]
                    qe = jnp.concatenate([q[r] * e[r] for r in odd_rows], axis=0)
                    pm = lax.dot_general(qe, k * e, nt, preferred_element_type=F32)
                    slabs = [a[b * m:(b + 1) * m] for b in range(C // m)]
                    for i, r in enumerate(odd_rows):
                        slabs[2 * i + 1] = jnp.where(owned[lvl + 1][r], pm[i * m:(i + 1) * m], a[r])
                    a = jnp.concatenate(slabs, axis=0)
                tot = x * y if lvl else f
                if 2 * m == V7X_SUBLANES:
                    partner = _roll_rows(tot, m)
                else:
                    partner = jnp.where(odds[lvl], _roll_rows(tot, m), _roll_rows(tot, -m))
                z = x * partner
                keep = odds[lvl] == odds[lvl + 1] if lvl + 1 < n_levels else jnp.logical_not(odds[lvl])
                x, y = jnp.where(keep, z, y), jnp.where(keep, y, z)
            return a.astype(BF16), q * y.astype(BF16), k * x.astype(BF16), x[0:1, :] * y[0:1, :]

        def finish(u, h, a, q_dec, k_dec, decay):
            rows = chunk_rows(u)
            hs = slice(h * HGRN_KEY_DIM, (h + 1) * HGRN_KEY_DIM)
            v = i_ref[rows, hs]
            st = state_ref[h]
            o = (jnp.dot(a, v, preferred_element_type=F32)
                 + lax.dot_general(q_dec, st.astype(BF16), nt, preferred_element_type=F32))
            upd = lax.dot_general(v, k_dec, (((0,), (0,)), ((), ())), preferred_element_type=F32)
            state_ref[h] = st * decay + upd
            ms = jnp.mean(o * o, axis=-1, keepdims=True)
            y = o * lax.rsqrt(ms + EPS) * g_on
            o_ref[rows, hs] = (y * z_ref[rows, hs].astype(F32)).astype(BF16)

        items = [(u, h) for u in range(HGRN_CHUNKS_PER_ITER) for h in range(HGRN_HEADS)]
        pending = {}
        for step in range(len(items) + HGRN_SKEW):
            if step < len(items):
                pending[step] = scan(*items[step])
            done = step - HGRN_SKEW
            if done >= 0:
                finish(*items[done], *pending.pop(done))
        return carry

    assert n_chunks % HGRN_CHUNKS_PER_ITER == 0
    lax.fori_loop(0, n_chunks // HGRN_CHUNKS_PER_ITER, chunk_body, 0)


def _merge_kernel(o1_ref, o2_ref, o3_ref, l1_ref, l2_ref, l3_ref, za_ref, ob_ref,
                  sga_ref, sgb_ref, x_ref, mod_ref, wa_ref, wb_ref, wo_ref, fg_ref,
                  ex_ref, out_ref):
    ex = ex_ref[...]
    gate = mod_ref[:, 2 * D_MODEL:3 * D_MODEL]

    def expand(w):
        hi = w.astype(BF16)
        lo = (w - hi.astype(F32)).astype(BF16)
        return jnp.dot(jnp.concatenate([hi, lo], axis=1), ex, preferred_element_type=F32)

    def mix(rows):
        l1, l2, l3 = l1_ref[rows, :], l2_ref[rows, :], l3_ref[rows, :]
        mx = jnp.maximum(jnp.maximum(l1, l2), l3)
        e1, e2, e3 = jnp.exp(l1 - mx), jnp.exp(l2 - mx), jnp.exp(l3 - mx)
        inv = 1.0 / (e1 + e2 + e3)
        oa = (expand(e1 * inv) * o1_ref[rows, :].astype(F32)
              + expand(e2 * inv) * o2_ref[rows, :].astype(F32)
              + expand(e3 * inv) * o3_ref[rows, :].astype(F32))
        return (oa * za_ref[rows, :].astype(F32)).astype(BF16)

    def branches(rows, oa):
        ya = jnp.dot(oa, wa_ref[...], preferred_element_type=F32)
        yb = jnp.dot(ob_ref[rows, :], wb_ref[...], preferred_element_type=F32)
        return (sga_ref[rows, :].astype(F32) * ya + sgb_ref[rows, :].astype(F32) * yb).astype(BF16)

    def project(rows, y):
        z = jnp.dot(y, wo_ref[...], preferred_element_type=F32)
        xo = x_ref[rows, :] + gate * z
        ms = jnp.mean(xo * xo, axis=-1, keepdims=True)
        out_ref[rows, :] = xo * lax.rsqrt(ms + EPS) * fg_ref[...]

    pieces = [slice(r, r + MERGE_PIECE) for r in range(0, x_ref.shape[0], MERGE_PIECE)]
    oas = [mix(rows) for rows in pieces]
    ys = [branches(rows, oa) for rows, oa in zip(pieces, oas)]
    for rows, y in zip(pieces, ys):
        project(rows, y)


_VMEM_MIB = {"inproj": 56, "attn": 48, "hgrn": 32, "merge": 40}


def _cparams(sem, vmem_mib):
    return pltpu.CompilerParams(dimension_semantics=sem,
                                vmem_limit_bytes=vmem_mib * 1024 * 1024)


def kernel(x, c, w_ada, b_ada, norm_g, w_in, hgrn_onorm_g, w_branch_a, w_branch_b, w_out,
           rel_bias, hgrn_lb, final_g):
    B, S, D = x.shape
    assert D == D_MODEL and w_ada.shape[0] == 1, "single-layer kernel"
    N = B * S
    x2 = x.reshape(N, D)

    c8 = jnp.pad(c, ((0, 8 - B), (0, 0)))
    mod = pl.pallas_call(
        _mod_kernel,
        grid=(3 * D // 512,),
        in_specs=[pl.BlockSpec((8, D), lambda j: (0, 0)),
                  pl.BlockSpec((D, 512), lambda j: (0, j)),
                  pl.BlockSpec((1, 512), lambda j: (0, j))],
        out_specs=pl.BlockSpec((8, 512), lambda j: (0, j)),
        out_shape=jax.ShapeDtypeStruct((8, 3 * D), F32),
        name="adaln_mod",
    )(c8, w_ada[0], b_ada[0].reshape(1, 3 * D))
    mod3 = mod.reshape(8, 1, 3 * D)

    lb = pl.pallas_call(
        _lower_bound_kernel,
        out_shape=jax.ShapeDtypeStruct((1, HGRN_WIDTH), F32),
        name="hgrn_lower_bound",
    )(hgrn_lb)

    n_pat = len(DILATED_PATTERNS)
    bias_tab = pl.pallas_call(
        _bias_table_kernel,
        grid=(n_pat,),
        in_specs=[pl.BlockSpec(memory_space=pltpu.SMEM),
                  pl.BlockSpec((None, ATTN_BLOCK, 2 * ATTN_BLOCK), lambda g: (g, 0, 0))],
        out_specs=pl.BlockSpec((None, 2, ATTN_HEADS, ATTN_BLOCK, 2 * ATTN_BLOCK),
                               lambda g: (g, 0, 0, 0, 0)),
        out_shape=jax.ShapeDtypeStruct((n_pat, 2, ATTN_HEADS, ATTN_BLOCK, 2 * ATTN_BLOCK), F32),
        name="rel_bias_table",
    )(rel_bias, jnp.asarray(_bucket_tables()))

    tm = PERM_TILE
    tiles_per_b = S // tm
    in_width = len(_PROJ_KINDS) * PROJ_TN
    assert PROJ_TN == ATTN_WIDTH and w_in.shape[2] == in_width
    resident = dict(pipeline_mode=pl.Buffered(1))
    main, fgate, qkv_p = pl.pallas_call(
        _inproj_kernel,
        grid=(N // tm,),
        in_specs=[pl.BlockSpec((tm, D), lambda i: (i, 0)),
                  pl.BlockSpec((None, 1, 3 * D), lambda i: (i // tiles_per_b, 0, 0)),
                  pl.BlockSpec((1, D), lambda i: (0, 0)),
                  pl.BlockSpec((1, HGRN_WIDTH), lambda i: (0, 0)),
                  pl.BlockSpec((D, in_width), lambda i: (0, 0), **resident)],
        out_specs=[pl.BlockSpec((tm, MAIN_WIDTH), lambda i: (i, 0)),
                   pl.BlockSpec((tm, HGRN_WIDTH), lambda i: (i, 0)),
                   pl.BlockSpec((tm, _QKV_TILES * ATTN_WIDTH), lambda i: (i, 0))],
        out_shape=[jax.ShapeDtypeStruct((N, MAIN_WIDTH), BF16),
                   jax.ShapeDtypeStruct((N, HGRN_WIDTH), F32),
                   jax.ShapeDtypeStruct((N, _QKV_TILES * ATTN_WIDTH), BF16)],
        scratch_shapes=[pltpu.VMEM((tm, D), BF16),
                        pltpu.VMEM((2, PROJ_PIECE_N // V7X_LANES, PROJ_PIECE_M, V7X_LANES), F32)],
        compiler_params=_cparams(("arbitrary",), _VMEM_MIB["inproj"]),
        name="inproj",
    )(x2, mod3, norm_g[0].reshape(1, D), lb, w_in[0].astype(BF16))

    qa_t = _MAIN_COLS["qa"][0] // ATTN_WIDTH
    ka_t = _MAIN_COLS["ka"][0] // ATTN_WIDTH
    va_t = _MAIN_COLS["va"][0] // ATTN_WIDTH
    attn_outs = []
    main_b = main.reshape(B, S, MAIN_WIDTH)
    qkv_pv = qkv_p.reshape(B, S, _QKV_TILES * ATTN_WIDTH)
    n_super = S // ATTN_SUPER
    for g, (window, dil) in enumerate(DILATED_PATTERNS):
        assert window // dil == ATTN_BLOCK
        src, tiles = (main_b, (qa_t, ka_t, va_t)) if dil == 1 else (qkv_pv, (0, 1, 2))
        prev_rows = {1: ATTN_BLOCK, PERM_DIL: ATTN_SUPER}.get(dil, PERM_TILE)
        per_step = ATTN_SUPER // prev_rows

        def cur_spec(t):
            return pl.BlockSpec((None, ATTN_SUPER, ATTN_WIDTH), lambda b, n, t=t: (b, n, t))

        def prev_spec(t, prev_rows=prev_rows, per_step=per_step):
            return pl.BlockSpec((None, prev_rows, ATTN_WIDTH),
                                lambda b, n, t=t: (b, jnp.maximum(n * per_step - 1, 0), t))

        scratch = []
        if dil > 1:
            scratch += [pltpu.VMEM((ATTN_WIDTH // V7X_LANES, ATTN_SUPER, V7X_LANES), F32)]
        if dil != PERM_DIL:
            scratch += [pltpu.VMEM((prev_rows + ATTN_SUPER, ATTN_WIDTH), BF16)] * 2
        o_g, lse_g = pl.pallas_call(
            functools.partial(_attn_kernel, dil=dil),
            grid=(B, n_super),
            in_specs=[cur_spec(tiles[0]), prev_spec(tiles[1]), cur_spec(tiles[1]),
                      prev_spec(tiles[2]), cur_spec(tiles[2]),
                      pl.BlockSpec((None, 2, ATTN_HEADS, ATTN_BLOCK, 2 * ATTN_BLOCK),
                                   lambda b, n, g=g: (g, 0, 0, 0, 0))],
            out_specs=[pl.BlockSpec((None, ATTN_SUPER, ATTN_WIDTH), lambda b, n: (b, n, 0)),
                       pl.BlockSpec((None, ATTN_SUPER, V7X_LANES), lambda b, n: (b, n, 0))],
            out_shape=[jax.ShapeDtypeStruct((B, S, ATTN_WIDTH), BF16),
                       jax.ShapeDtypeStruct((B, S, V7X_LANES), F32)],
            scratch_shapes=scratch,
            compiler_params=_cparams(("arbitrary", "arbitrary"), _VMEM_MIB["attn"]),
            name=f"dilated_attn_d{dil}",
        )(src, src, src, src, src, bias_tab)
        attn_outs.append((o_g.reshape(N, ATTN_WIDTH), lse_g.reshape(N, V7X_LANES)))

    th = HGRN_TILE
    chunk = HGRN_CHUNK
    hw_t = HGRN_WIDTH
    qb_t = _MAIN_COLS["qb"][0] // hw_t
    ib_t = _MAIN_COLS["ib"][0] // hw_t
    zb_t = _MAIN_COLS["zb"][0] // hw_t
    ob = pl.pallas_call(
        functools.partial(_hgrn_kernel, chunk=chunk, n_chunks=th // chunk),
        grid=(B, S // th),
        in_specs=[pl.BlockSpec((None, th, hw_t), lambda b, s: (b, s, qb_t)),
                  pl.BlockSpec((None, th, hw_t), lambda b, s: (b, s, 0)),
                  pl.BlockSpec((None, th, hw_t), lambda b, s: (b, s, ib_t)),
                  pl.BlockSpec((None, th, hw_t), lambda b, s: (b, s, zb_t)),
                  pl.BlockSpec((1, HGRN_VAL_DIM), lambda b, s: (0, 0))],
        out_specs=pl.BlockSpec((None, th, hw_t), lambda b, s: (b, s, 0)),
        out_shape=jax.ShapeDtypeStruct((B, S, hw_t), BF16),
        scratch_shapes=[pltpu.VMEM((HGRN_HEADS, HGRN_VAL_DIM, HGRN_KEY_DIM), F32)],
        compiler_params=_cparams(("arbitrary", "arbitrary"), _VMEM_MIB["hgrn"]),
        name="hgrn2",
    )(main_b, fgate.reshape(B, S, hw_t), main_b, main_b, hgrn_onorm_g[0].reshape(1, HGRN_VAL_DIM))
    ob = ob.reshape(N, hw_t)

    tk = MERGE_TILE
    tiles_per_b5 = S // tk
    za_t = _MAIN_COLS["za"][0] // ATTN_WIDTH
    ga_t = _MAIN_COLS["ga"][0] // D
    gb_t = _MAIN_COLS["gb"][0] // D
    expand_mat = np.zeros((2 * V7X_LANES, ATTN_WIDTH), np.float32)
    for h in range(ATTN_HEADS):
        expand_mat[h, h * ATTN_HEAD_DIM:(h + 1) * ATTN_HEAD_DIM] = 1.0
        expand_mat[V7X_LANES + h, h * ATTN_HEAD_DIM:(h + 1) * ATTN_HEAD_DIM] = 1.0
    (o1, l1), (o2, l2), (o3, l3) = attn_outs
    def row_spec(w, t=0, deep=False):
        mode = dict(pipeline_mode=pl.Buffered(3)) if deep else {}
        return pl.BlockSpec((tk, w), lambda i, t=t: (i, t), **mode)

    stream_specs = [row_spec(ATTN_WIDTH, deep=True), row_spec(ATTN_WIDTH, deep=True),
                    row_spec(ATTN_WIDTH, deep=True),
                    row_spec(V7X_LANES, deep=True), row_spec(V7X_LANES, deep=True),
                    row_spec(V7X_LANES, deep=True),
                    row_spec(ATTN_WIDTH, za_t, deep=True), row_spec(HGRN_WIDTH, deep=True),
                    row_spec(D, ga_t, deep=True), row_spec(D, gb_t, deep=True), row_spec(D, deep=True),
                    pl.BlockSpec((None, 1, 3 * D), lambda i: (i // tiles_per_b5, 0, 0))]
    n_stream = len(stream_specs)

    def merge_call(*refs):
        streams, (wa, wb, wo, fg, ex, out_hbm) = refs[:n_stream], refs[n_stream:]

        def step(*blocks):
            _merge_kernel(*blocks[:n_stream], wa, wb, wo, fg, ex, blocks[n_stream])

        pltpu.emit_pipeline(step, grid=(N // tk,), in_specs=stream_specs,
                            out_specs=[pl.BlockSpec((tk, D), lambda i: (i, 0))])(*streams, out_hbm)

    hbm = pl.BlockSpec(memory_space=pl.ANY)
    vmem = pl.BlockSpec(memory_space=pltpu.VMEM)
    out = pl.pallas_call(
        merge_call,
        in_specs=[hbm] * n_stream + [vmem] * 5,
        out_specs=hbm,
        out_shape=jax.ShapeDtypeStruct((N, D), F32),
        compiler_params=pltpu.CompilerParams(vmem_limit_bytes=_VMEM_MIB["merge"] * 1024 * 1024),
        name="gated_merge",
    )(o1, o2, o3, l1, l2, l3, main, ob, main, main, x2, mod3,
      w_branch_a[0].astype(BF16), w_branch_b[0].astype(BF16), w_out[0].astype(BF16),
      final_g.reshape(1, D), jnp.asarray(expand_mat, BF16))
    return out.reshape(B, S, D)
```
